```python
import jax, jax.numpy as jnp
from jax import lax
import numpy as np

D_MODEL = 1024
BATCH = 2
SEQ = 16384
DEPTH = 1
DEC_BATCH = 4
DEC_SEQ = 8192
PAST_LEN = 128

HEAD_DIM = 64
A_Q_HEADS = 8
A_KV_HEADS = 2
B_Q_HEADS = 8
B_KV_HEADS = 2
A_Q_W = A_Q_HEADS * HEAD_DIM
A_KV_W = A_KV_HEADS * HEAD_DIM
B_Q_W = B_Q_HEADS * HEAD_DIM
B_KV_W = B_KV_HEADS * HEAD_DIM
IN_W = A_Q_W + 2 * A_KV_W + B_Q_W + 2 * B_KV_W + 2 * D_MODEL
WINDOW = 128
BLOCK = 128
GRID_W = 64
AXIS_ROPE_DIM = HEAD_DIM // 2
ROPE_THETA = 10000.0
N_GROUPS = 4
EXPERTS_PER_GROUP = 4
N_EXPERTS = N_GROUPS * EXPERTS_PER_GROUP
TOP_K_IN_GROUP = 2
D_EXPERT = 512
EPS = 1e-6
NEG_INF = -1e30

kernel_name = 'hybrid_window_axial_gqa_hmoe_encoder'


def _rmsnorm(x, g):
    xf = x.astype(jnp.float32)
    y = xf * lax.rsqrt(jnp.mean(xf * xf, axis=-1, keepdims=True) + EPS)
    return (y * g.astype(jnp.float32)).astype(x.dtype)


def _alibi_slopes(n_heads):
    return jnp.exp2(-8.0 * jnp.arange(1, n_heads + 1, dtype=jnp.float32) / n_heads)


def _window_attention(q, k, v, sink):
    bsz, s_len = q.shape[0], q.shape[1]
    nb = s_len // BLOCK
    grp = A_Q_HEADS // A_KV_HEADS
    qb = q.reshape(bsz, nb, BLOCK, A_KV_HEADS, grp, HEAD_DIM)
    pad = ((0, 0), (BLOCK, BLOCK), (0, 0), (0, 0))

    def windows(t):
        tb = jnp.pad(t, pad).reshape(bsz, nb + 2, BLOCK, A_KV_HEADS, HEAD_DIM)
        return jnp.concatenate([tb[:, :-2], tb[:, 1:-1], tb[:, 2:]], axis=2)

    kw, vw = windows(k), windows(v)
    scores = jnp.einsum('bnqhgd,bnkhd->bnhgqk', qb, kw).astype(jnp.float32) * (HEAD_DIM ** -0.5)
    blk = jnp.arange(nb)[:, None] * BLOCK
    q_pos = blk + jnp.arange(BLOCK)[None, :]
    k_pos = blk - BLOCK + jnp.arange(3 * BLOCK)[None, :]
    dist = jnp.abs(q_pos[:, :, None] - k_pos[:, None, :])
    valid = (dist <= WINDOW) & ((k_pos >= 0) & (k_pos < s_len))[:, None, :]
    slopes = _alibi_slopes(A_Q_HEADS).reshape(A_KV_HEADS, grp)
    logits = scores - slopes[None, None, :, :, None, None] * dist.astype(jnp.float32)[None, :, None, None]
    logits = jnp.where(valid[None, :, None, None], logits, NEG_INF)
    sink_b = sink.astype(jnp.float32).reshape(A_KV_HEADS, grp)[None, None, :, :, None, None]
    m = jnp.maximum(jnp.max(logits, axis=-1, keepdims=True), sink_b)
    p = jnp.exp(logits - m)
    denom = jnp.sum(p, axis=-1, keepdims=True) + jnp.exp(sink_b - m)
    out = jnp.einsum('bnhgqk,bnkhd->bnqhgd', (p / denom).astype(v.dtype), vw)
    return out.reshape(bsz, s_len, A_Q_HEADS * HEAD_DIM)


def _axial_rope_tables(seq_len):
    rows = seq_len // GRID_W
    row = jnp.repeat(jnp.arange(rows, dtype=jnp.float32), GRID_W)
    col = jnp.tile(jnp.arange(GRID_W, dtype=jnp.float32), rows)
    inv = ROPE_THETA ** (-jnp.arange(0, AXIS_ROPE_DIM, 2, dtype=jnp.float32) / AXIS_ROPE_DIM)
    ang = jnp.concatenate([row[:, None] * inv, col[:, None] * inv], axis=-1)
    return jnp.cos(ang), jnp.sin(ang)


def _apply_axial_rope(x, cos, sin):
    xf = x.astype(jnp.float32).reshape(x.shape[:-1] + (HEAD_DIM // 2, 2))
    x0, x1 = xf[..., 0], xf[..., 1]
    c = cos[None, :, None, :]
    s = sin[None, :, None, :]
    out = jnp.stack([x0 * c - x1 * s, x0 * s + x1 * c], axis=-1)
    return out.reshape(x.shape).astype(x.dtype)


def _global_attention(q, k, v):
    bsz, s_len = q.shape[0], q.shape[1]
    nb = s_len // BLOCK
    grp = B_Q_HEADS // B_KV_HEADS
    qb = q.reshape(bsz, nb, BLOCK, B_KV_HEADS, grp, HEAD_DIM).transpose(1, 0, 2, 3, 4, 5)

    def one_block(q_blk):
        s = jnp.einsum('bqhgd,bkhd->bhgqk', q_blk, k).astype(jnp.float32) * (HEAD_DIM ** -0.5)
        p = jax.nn.softmax(s, axis=-1)
        return jnp.einsum('bhgqk,bkhd->bqhgd', p.astype(v.dtype), v)

    out = lax.map(one_block, qb)
    return out.transpose(1, 0, 2, 3, 4, 5).reshape(bsz, s_len, B_Q_HEADS * HEAD_DIM)


def _hier_moe(h, w_rg, b_rg, w_re, b_re, w_eg, w_eu, w_ed):
    bsz, s_len, d = h.shape
    t = h.reshape(-1, d)
    g_logits = (t @ w_rg).astype(jnp.float32) + b_rg.astype(jnp.float32)
    g_prob = jax.nn.softmax(g_logits, axis=-1)
    g_idx = jnp.argmax(g_logits, axis=-1)
    g_w = jnp.take_along_axis(g_prob, g_idx[:, None], axis=-1)
    e_logits = ((t @ w_re).astype(jnp.float32) + b_re.astype(jnp.float32)).reshape(-1, N_GROUPS, EXPERTS_PER_GROUP)
    e_sel = jnp.take_along_axis(e_logits, g_idx[:, None, None], axis=1)[:, 0]
    e_prob = jax.nn.softmax(e_sel, axis=-1)
    top_w, top_i = lax.top_k(e_prob, TOP_K_IN_GROUP)
    top_w = top_w / jnp.sum(top_w, axis=-1, keepdims=True)
    gate = g_w * top_w
    expert_id = g_idx[:, None] * EXPERTS_PER_GROUP + top_i
    combine = jnp.sum(jax.nn.one_hot(expert_id, N_EXPERTS, dtype=jnp.float32) * gate[..., None], axis=1)
    y = jnp.zeros((t.shape[0], d), jnp.float32)
    for e in range(N_EXPERTS):
        a = jax.nn.silu(t @ w_eg[e]) * (t @ w_eu[e])
        y = y + combine[:, e:e + 1] * (a @ w_ed[e]).astype(jnp.float32)
    return y.astype(h.dtype).reshape(bsz, s_len, d)


def _layer(x, attn_norm_g, w_in, a_sink, b_q_norm_g, b_k_norm_g, w_oa, w_ob, w_out, ffn_norm_g,
           w_rg, b_rg, w_re, b_re, w_eg, w_eu, w_ed):
    bsz, s_len, _ = x.shape
    h = _rmsnorm(x, attn_norm_g)
    proj = h @ w_in
    sizes = [A_Q_W, A_KV_W, A_KV_W, B_Q_W, B_KV_W, B_KV_W, D_MODEL, D_MODEL]
    offsets = [int(o) for o in np.cumsum(sizes)[:-1]]
    aq, ak, av, bq, bk, bv, gate_a, gate_b = jnp.split(proj, offsets, axis=-1)
    aq = aq.reshape(bsz, s_len, A_Q_HEADS, HEAD_DIM)
    ak = ak.reshape(bsz, s_len, A_KV_HEADS, HEAD_DIM)
    av = av.reshape(bsz, s_len, A_KV_HEADS, HEAD_DIM)
    ya = _window_attention(aq, ak, av, a_sink)
    cos, sin = _axial_rope_tables(s_len)
    bq = _apply_axial_rope(_rmsnorm(bq.reshape(bsz, s_len, B_Q_HEADS, HEAD_DIM), b_q_norm_g), cos, sin)
    bk = _apply_axial_rope(_rmsnorm(bk.reshape(bsz, s_len, B_KV_HEADS, HEAD_DIM), b_k_norm_g), cos, sin)
    bv = bv.reshape(bsz, s_len, B_KV_HEADS, HEAD_DIM)
    yb = _global_attention(bq, bk, bv)
    merged = jax.nn.sigmoid(gate_a) * (ya @ w_oa) + jax.nn.sigmoid(gate_b) * (yb @ w_ob)
    x = x + merged @ w_out
    x = x + _hier_moe(_rmsnorm(x, ffn_norm_g), w_rg, b_rg, w_re, b_re, w_eg, w_eu, w_ed)
    return x


def setup_inputs(seed: int = 0) -> dict:
    key = jax.random.key(seed)
    ks = jax.random.split(key, 20)
    f32 = jnp.float32

    def nrm(k, shape, scale):
        return jax.random.normal(k, shape, f32) * scale

    return {
        'x_prompt': nrm(ks[0], (BATCH, SEQ, D_MODEL), 1.0),
        'x_sample': nrm(ks[1], (DEC_BATCH, DEC_SEQ, D_MODEL), 1.0),
        'attn_norm_g': 1.0 + nrm(ks[2], (DEPTH, D_MODEL), 0.05),
        'w_in': nrm(ks[3], (DEPTH, D_MODEL, IN_W), D_MODEL ** -0.5),
        'a_sink': nrm(ks[4], (DEPTH, A_Q_HEADS), 0.5),
        'b_q_norm_g': 1.0 + nrm(ks[5], (DEPTH, HEAD_DIM), 0.05),
        'b_k_norm_g': 1.0 + nrm(ks[6], (DEPTH, HEAD_DIM), 0.05),
        'w_oa': nrm(ks[7], (DEPTH, A_Q_W, D_MODEL), A_Q_W ** -0.5),
        'w_ob': nrm(ks[8], (DEPTH, B_Q_W, D_MODEL), B_Q_W ** -0.5),
        'w_out': nrm(ks[9], (DEPTH, D_MODEL, D_MODEL), D_MODEL ** -0.5),
        'ffn_norm_g': 1.0 + nrm(ks[10], (DEPTH, D_MODEL), 0.05),
        'w_router_group': nrm(ks[11], (DEPTH, D_MODEL, N_GROUPS), D_MODEL ** -0.5),
        'b_router_group': nrm(ks[12], (DEPTH, N_GROUPS), 0.01),
        'w_router_expert': nrm(ks[13], (DEPTH, D_MODEL, N_EXPERTS), D_MODEL ** -0.5),
        'b_router_expert': nrm(ks[14], (DEPTH, N_EXPERTS), 0.01),
        'w_expert_gate': nrm(ks[15], (DEPTH, N_EXPERTS, D_MODEL, D_EXPERT), D_MODEL ** -0.5),
        'w_expert_up': nrm(ks[16], (DEPTH, N_EXPERTS, D_MODEL, D_EXPERT), D_MODEL ** -0.5),
        'w_expert_down': nrm(ks[17], (DEPTH, N_EXPERTS, D_EXPERT, D_MODEL), D_EXPERT ** -0.5),
        'final_norm_g': 1.0 + nrm(ks[18], (D_MODEL,), 0.05),
    }


def reference(x_prompt, x_sample, attn_norm_g, w_in, a_sink, b_q_norm_g, b_k_norm_g, w_oa, w_ob, w_out,
              ffn_norm_g, w_router_group, b_router_group, w_router_expert, b_router_expert,
              w_expert_gate, w_expert_up, w_expert_down, final_norm_g):
    def trunk(x):
        for l in range(DEPTH):
            x = _layer(x, attn_norm_g[l], w_in[l], a_sink[l], b_q_norm_g[l], b_k_norm_g[l], w_oa[l], w_ob[l],
                       w_out[l], ffn_norm_g[l], w_router_group[l], b_router_group[l], w_router_expert[l],
                       b_router_expert[l], w_expert_gate[l], w_expert_up[l], w_expert_down[l])
        return _rmsnorm(x, final_norm_g)

    y_prompt = trunk(x_prompt)
    y_sample = trunk(x_sample)
    return (y_prompt, y_sample)
```

```python
import functools

import jax
import jax.numpy as jnp
import numpy as np
from jax import lax
from jax.experimental import pallas as pl
from jax.experimental.pallas import tpu as pltpu

HEAD_DIM = 64
Q_HEADS = 8
KV_HEADS = 2
GROUP = Q_HEADS // KV_HEADS
Q_W = Q_HEADS * HEAD_DIM
KV_W = KV_HEADS * HEAD_DIM
WINDOW = 128
BLOCK = 128
GRID_W = 64
ROPE_THETA = 10000.0
N_GROUPS = 4
EXPERTS_PER_GROUP = 4
N_EXPERTS = N_GROUPS * EXPERTS_PER_GROUP
EPS = 1e-6
NEG_INF = -1e30
SCALE = HEAD_DIM ** -0.5
LANES = 128

VMEM_LIMIT = 56 * 1024 * 1024

F32 = jnp.float32
BF16 = jnp.bfloat16


def _params(sem):
    return pltpu.CompilerParams(dimension_semantics=sem, vmem_limit_bytes=VMEM_LIMIT)


def _full(shape):
    return pl.BlockSpec(shape, lambda *_: (0,) * len(shape))


def _split_dot(y, ones_bd):
    hi = y.astype(BF16)
    lo = (y - hi.astype(F32)).astype(BF16)
    return jnp.dot(hi, ones_bd, preferred_element_type=F32) + jnp.dot(lo, ones_bd, preferred_element_type=F32)


def _pair_swap(y):
    n = y.shape[-1]
    lane = lax.broadcasted_iota(jnp.int32, y.shape, y.ndim - 1)
    nxt = pltpu.roll(y, n - 1, y.ndim - 1)
    prv = pltpu.roll(y, 1, y.ndim - 1)
    return jnp.where((lane & 1) == 0, nxt, prv)


def _norm_rope(y, gain, ones_bd, cos, sin):
    ms = _split_dot(y * y, ones_bd) * (1.0 / HEAD_DIM)
    yn = y * lax.rsqrt(ms + EPS) * gain
    return yn * cos + _pair_swap(yn) * sin


def _in_proj_kernel(x_ref, g_ref, w_ref, cos_ref, sin_ref, gq_ref, gk_ref, oq_ref, ok_ref,
                    qa_ref, ka_ref, va_ref, qb_ref, kb_ref, vb_ref, ga_ref, gb_ref):
    x = x_ref[...]
    ms = jnp.mean(x * x, axis=-1, keepdims=True)
    h = (x * lax.rsqrt(ms + EPS) * g_ref[...]).astype(BF16)

    def proj(lo, width):
        return jnp.dot(h, w_ref[:, lo:lo + width], preferred_element_type=F32)

    o = 0
    qa_ref[...] = (proj(o, Q_W) * SCALE).astype(BF16); o += Q_W
    ka_ref[...] = proj(o, KV_W).astype(BF16); o += KV_W
    va_ref[...] = proj(o, KV_W).astype(BF16); o += KV_W
    cos = cos_ref[...]
    sin = sin_ref[...]
    qb = proj(o, Q_W); o += Q_W
    cos_q = jnp.concatenate([cos] * (Q_W // LANES), axis=1)
    sin_q = jnp.concatenate([sin] * (Q_W // LANES), axis=1)
    qb_ref[...] = (_norm_rope(qb, gq_ref[...], oq_ref[...], cos_q, sin_q) * SCALE).astype(BF16)
    kb = proj(o, KV_W); o += KV_W
    kb_ref[...] = _norm_rope(kb, gk_ref[...], ok_ref[...], cos, sin).astype(BF16)
    vb_ref[...] = proj(o, KV_W).astype(BF16); o += KV_W
    d = x.shape[-1]
    ga_ref[...] = jax.nn.sigmoid(proj(o, d)).astype(BF16); o += d
    gb_ref[...] = jax.nn.sigmoid(proj(o, d)).astype(BF16)


def _rope_tables(seq_len):
    rows = seq_len // GRID_W
    row = jnp.repeat(jnp.arange(rows, dtype=F32), GRID_W)
    col = jnp.tile(jnp.arange(GRID_W, dtype=F32), rows)
    half = HEAD_DIM // 2
    inv = ROPE_THETA ** (-jnp.arange(0, half, 2, dtype=F32) / half)
    ang = jnp.concatenate([row[:, None] * inv, col[:, None] * inv], axis=-1)
    cos = jnp.repeat(jnp.cos(ang), 2, axis=-1)
    sin = jnp.repeat(jnp.sin(ang), 2, axis=-1) * jnp.tile(jnp.array([-1.0, 1.0], F32), half)
    reps = LANES // HEAD_DIM
    return jnp.tile(cos, (1, reps)), jnp.tile(sin, (1, reps))


def _block_ones(width):
    idx = np.arange(width) // HEAD_DIM
    return jnp.asarray(idx[:, None] == idx[None, :], dtype=BF16)


def _in_proj(x2, seq_len, attn_g, w_in, gq, gk, tm):
    t, d = x2.shape
    in_w = w_in.shape[1]
    cos, sin = _rope_tables(seq_len)
    nseq = seq_len // tm
    row = lambda i: (i, 0)
    pos = lambda i: (i % nseq, 0)
    outs = [(Q_W, BF16), (KV_W, BF16), (KV_W, BF16), (Q_W, BF16), (KV_W, BF16), (KV_W, BF16), (d, BF16), (d, BF16)]
    return pl.pallas_call(
        _in_proj_kernel,
        grid=(t // tm,),
        in_specs=[pl.BlockSpec((tm, d), row), _full((1, d)), _full((d, in_w)),
                  pl.BlockSpec((tm, LANES), pos), pl.BlockSpec((tm, LANES), pos),
                  _full((1, Q_W)), _full((1, KV_W)), _full((Q_W, Q_W)), _full((KV_W, KV_W))],
        out_specs=[pl.BlockSpec((tm, w), row) for w, _ in outs],
        out_shape=[jax.ShapeDtypeStruct((t, w), dt) for w, dt in outs],
        compiler_params=_params(("parallel",)),
        name="in_proj",
    )(x2, attn_g.reshape(1, d), w_in.astype(BF16), cos, sin,
      jnp.tile(gq.astype(F32), Q_HEADS).reshape(1, Q_W), jnp.tile(gk.astype(F32), KV_HEADS).reshape(1, KV_W),
      _block_ones(Q_W), _block_ones(KV_W))


def _alibi_bias():
    slopes = np.exp2(-8.0 * np.arange(1, Q_HEADS + 1, dtype=np.float32) / Q_HEADS).astype(np.float32)
    qi = np.arange(BLOCK)[:, None]
    kj = np.arange(3 * BLOCK)[None, :] - BLOCK
    dist = np.abs(qi - kj)
    bias = -slopes[:, None, None] * dist[None].astype(np.float32)
    return jnp.asarray(np.where(dist[None] <= WINDOW, bias, np.float32(NEG_INF)), dtype=F32)


def _win_attn_kernel(sink_ref, q_ref, kc_ref, kp_ref, kn_ref, vc_ref, vp_ref, vn_ref, bias_ref, o_ref,
                     kwin, vwin, *, blocks_per_seq, sub):
    i = pl.program_id(0)
    tq = q_ref.shape[0]
    kwin[0:BLOCK] = kp_ref[...]
    kwin[BLOCK:BLOCK + tq] = kc_ref[...]
    kwin[BLOCK + tq:] = kn_ref[...]
    vwin[0:BLOCK] = vp_ref[...]
    vwin[BLOCK:BLOCK + tq] = vc_ref[...]
    vwin[BLOCK + tq:] = vn_ref[...]
    col = lax.broadcasted_iota(jnp.int32, (BLOCK, 3 * BLOCK), 1)
    for j in range(sub):
        blk = (i * sub + j) % blocks_per_seq
        dead = ((col < BLOCK) & (blk == 0)) | ((col >= 2 * BLOCK) & (blk == blocks_per_seq - 1))
        outs = []
        for h in range(Q_HEADS):
            g = h // GROUP
            q = q_ref[j * BLOCK:(j + 1) * BLOCK, h * HEAD_DIM:(h + 1) * HEAD_DIM]
            k = kwin[j * BLOCK:(j + 3) * BLOCK, g * HEAD_DIM:(g + 1) * HEAD_DIM]
            v = vwin[j * BLOCK:(j + 3) * BLOCK, g * HEAD_DIM:(g + 1) * HEAD_DIM]
            s = lax.dot_general(q, k, (((1,), (1,)), ((), ())), preferred_element_type=F32)
            logits = jnp.where(dead, NEG_INF, s + bias_ref[h])
            sink = sink_ref[h]
            m = jnp.maximum(jnp.max(logits, axis=-1, keepdims=True), sink)
            p = jnp.exp(logits - m)
            denom = jnp.sum(p, axis=-1, keepdims=True) + jnp.exp(sink - m)
            pv = jnp.dot(p.astype(BF16), v, preferred_element_type=F32)
            outs.append(pv / denom)
        o_ref[j * BLOCK:(j + 1) * BLOCK, :] = jnp.concatenate(outs, axis=1).astype(o_ref.dtype)


def _win_attn(qa, ka, va, sink, seq_len, tq):
    t = qa.shape[0]
    sub = tq // BLOCK
    nblk = t // BLOCK
    row = lambda i: (i, 0)
    prev = lambda i: (jnp.maximum(i * sub - 1, 0), 0)
    nxt = lambda i: (jnp.minimum((i + 1) * sub, nblk - 1), 0)
    kv_specs = [pl.BlockSpec((tq, KV_W), row), pl.BlockSpec((BLOCK, KV_W), prev), pl.BlockSpec((BLOCK, KV_W), nxt)]
    return pl.pallas_call(
        functools.partial(_win_attn_kernel, blocks_per_seq=seq_len // BLOCK, sub=sub),
        grid=(t // tq,),
        in_specs=[pl.BlockSpec(memory_space=pltpu.SMEM), pl.BlockSpec((tq, Q_W), row)] + kv_specs + kv_specs
                 + [_full((Q_HEADS, BLOCK, 3 * BLOCK))],
        out_specs=pl.BlockSpec((tq, Q_W), row),
        out_shape=jax.ShapeDtypeStruct((t, Q_W), BF16),
        scratch_shapes=[pltpu.VMEM((tq + 2 * BLOCK, KV_W), BF16), pltpu.VMEM((tq + 2 * BLOCK, KV_W), BF16)],
        compiler_params=_params(("parallel",)),
        name="win_attn",
    )(sink.astype(F32), qa, ka, ka, ka, va, va, va, _alibi_bias())


def _glob_attn_kernel(q_ref, k_ref, v_ref, o_ref, *, tk):
    tq = q_ref.shape[1]
    nk = k_ref.shape[1] // tk
    outs = []
    for h in range(Q_HEADS):
        g = h // GROUP
        q = q_ref[0, :, h * HEAD_DIM:(h + 1) * HEAD_DIM]

        def body(c, carry, q=q, g=g):
            m, l, acc = carry
            k = k_ref[0, pl.ds(pl.multiple_of(c * tk, tk), tk), g * HEAD_DIM:(g + 1) * HEAD_DIM]
            v = v_ref[0, pl.ds(pl.multiple_of(c * tk, tk), tk), g * HEAD_DIM:(g + 1) * HEAD_DIM]
            s = lax.dot_general(q, k, (((1,), (1,)), ((), ())), preferred_element_type=F32)
            m_new = jnp.maximum(m, jnp.max(s, axis=-1, keepdims=True))
            alpha = jnp.exp(m - m_new)
            p = jnp.exp(s - m_new)
            l = alpha * l + jnp.sum(p, axis=-1, keepdims=True)
            acc = alpha * acc + jnp.dot(p.astype(BF16), v, preferred_element_type=F32)
            return m_new, l, acc

        init = (jnp.full((tq, 1), -jnp.inf, F32), jnp.zeros((tq, 1), F32), jnp.zeros((tq, HEAD_DIM), F32))
        _, l, acc = lax.fori_loop(0, nk, body, init)
        outs.append(acc / l)
    o_ref[0] = jnp.concatenate(outs, axis=1).astype(o_ref.dtype)


def _glob_attn(qb, kb, vb, tq, tk):
    b, s, _ = qb.shape
    return pl.pallas_call(
        functools.partial(_glob_attn_kernel, tk=tk),
        grid=(b, s // tq),
        in_specs=[pl.BlockSpec((1, tq, Q_W), lambda bi, qi: (bi, qi, 0)),
                  pl.BlockSpec((1, s, KV_W), lambda bi, qi: (bi, 0, 0)),
                  pl.BlockSpec((1, s, KV_W), lambda bi, qi: (bi, 0, 0))],
        out_specs=pl.BlockSpec((1, tq, Q_W), lambda bi, qi: (bi, qi, 0)),
        out_shape=jax.ShapeDtypeStruct((b, s, Q_W), BF16),
        compiler_params=_params(("parallel", "parallel")),
        name="glob_attn",
    )(qb, kb, vb)


def _first_argmax(vals, idx, big):
    m = jnp.max(vals, axis=-1, keepdims=True)
    return m, jnp.min(jnp.where(vals == m, idx, big), axis=-1, keepdims=True)


def _route(logits):
    rows = logits.shape[0]
    gl = logits[:, 0:N_GROUPS]
    gidx = lax.broadcasted_iota(jnp.int32, (rows, N_GROUPS), 1)
    gmax, g_sel = _first_argmax(gl, gidx, N_GROUPS)
    g_w = 1.0 / jnp.sum(jnp.exp(gl - gmax), axis=-1, keepdims=True)
    e_sel = jnp.zeros((rows, EXPERTS_PER_GROUP), F32)
    for g in range(N_GROUPS):
        lo = N_GROUPS + g * EXPERTS_PER_GROUP
        e_sel = jnp.where(g_sel == g, logits[:, lo:lo + EXPERTS_PER_GROUP], e_sel)
    eidx = lax.broadcasted_iota(jnp.int32, (rows, EXPERTS_PER_GROUP), 1)
    ex = jnp.exp(e_sel - jnp.max(e_sel, axis=-1, keepdims=True))
    e_prob = ex / jnp.sum(ex, axis=-1, keepdims=True)
    p1, i1 = _first_argmax(e_prob, eidx, EXPERTS_PER_GROUP)
    rest = jnp.where(eidx == i1, -1.0, e_prob)
    p2, i2 = _first_argmax(rest, eidx, EXPERTS_PER_GROUP)
    tot = p1 + p2
    lane = lax.broadcasted_iota(jnp.int32, (rows, LANES), 1)
    id1 = g_sel * EXPERTS_PER_GROUP + i1
    id2 = g_sel * EXPERTS_PER_GROUP + i2
    return jnp.where(lane == id1, g_w * (p1 / tot), 0.0) + jnp.where(lane == id2, g_w * (p2 / tot), 0.0)


def _merge_kernel(x_ref, ya_ref, yb_ref, ga_ref, gb_ref, woa_ref, wob_ref, wout_ref, gf_ref, wr_ref, br_ref,
                  x1_ref, comb_ref):
    a = jnp.dot(ya_ref[...], woa_ref[...], preferred_element_type=F32)
    b = jnp.dot(yb_ref[...], wob_ref[...], preferred_element_type=F32)
    merged = ga_ref[...].astype(F32) * a + gb_ref[...].astype(F32) * b
    x1 = x_ref[...] + jnp.dot(merged.astype(BF16), wout_ref[...], preferred_element_type=F32)
    x1_ref[...] = x1
    ms = jnp.mean(x1 * x1, axis=-1, keepdims=True)
    t = x1 * lax.rsqrt(ms + EPS) * gf_ref[...]
    logits = jnp.dot(t, wr_ref[...], preferred_element_type=F32, precision=lax.Precision.HIGHEST) + br_ref[...]
    comb_ref[...] = _route(logits)


def _merge(x2, ya, yb, ga, gb, w_oa, w_ob, w_out, ffn_g, w_rg, b_rg, w_re, b_re, tm):
    t, d = x2.shape
    wr = jnp.zeros((d, LANES), F32).at[:, :N_GROUPS].set(w_rg).at[:, N_GROUPS:N_GROUPS + N_EXPERTS].set(w_re)
    br = jnp.zeros((1, LANES), F32).at[0, :N_GROUPS].set(b_rg).at[0, N_GROUPS:N_GROUPS + N_EXPERTS].set(b_re)
    row = lambda i: (i, 0)
    return pl.pallas_call(
        _merge_kernel,
        grid=(t // tm,),
        in_specs=[pl.BlockSpec((tm, d), row), pl.BlockSpec((tm, Q_W), row), pl.BlockSpec((tm, Q_W), row),
                  pl.BlockSpec((tm, d), row), pl.BlockSpec((tm, d), row),
                  _full((Q_W, d)), _full((Q_W, d)), _full((d, d)), _full((1, d)), _full((d, LANES)),
                  _full((1, LANES))],
        out_specs=[pl.BlockSpec((tm, d), row), pl.BlockSpec((tm, LANES), row)],
        out_shape=[jax.ShapeDtypeStruct((t, d), F32), jax.ShapeDtypeStruct((t, LANES), F32)],
        compiler_params=_params(("parallel",)),
        name="merge",
    )(x2, ya, yb, ga, gb, w_oa.astype(BF16), w_ob.astype(BF16), w_out.astype(BF16), ffn_g.reshape(1, d), wr, br)


def _moe_kernel(x1_ref, comb_ref, gf_ref, wg_ref, wu_ref, wd_ref, gfin_ref, o_ref, t_scr, y_scr):
    e = pl.program_id(1)

    @pl.when(e == 0)
    def _():
        x1 = x1_ref[...]
        ms = jnp.mean(x1 * x1, axis=-1, keepdims=True)
        t_scr[...] = (x1 * lax.rsqrt(ms + EPS) * gf_ref[...]).astype(BF16)
        y_scr[...] = jnp.zeros_like(y_scr)

    t = t_scr[...]
    a = jax.nn.silu(jnp.dot(t, wg_ref[0], preferred_element_type=F32)) * jnp.dot(t, wu_ref[0],
                                                                                preferred_element_type=F32)
    lane = lax.broadcasted_iota(jnp.int32, comb_ref.shape, 1)
    w = jnp.sum(jnp.where(lane == e, comb_ref[...], 0.0), axis=-1, keepdims=True)
    y_scr[...] += w * jnp.dot(a.astype(BF16), wd_ref[0], preferred_element_type=F32)

    @pl.when(e == pl.num_programs(1) - 1)
    def _():
        x2 = x1_ref[...] + y_scr[...]
        ms = jnp.mean(x2 * x2, axis=-1, keepdims=True)
        o_ref[...] = x2 * lax.rsqrt(ms + EPS) * gfin_ref[...]


def _moe(x1, comb, ffn_g, w_eg, w_eu, w_ed, final_g, tm):
    t, d = x1.shape
    de = w_eg.shape[-1]
    row = lambda i, e: (i, 0)
    return pl.pallas_call(
        _moe_kernel,
        grid=(t // tm, N_EXPERTS),
        in_specs=[pl.BlockSpec((tm, d), row), pl.BlockSpec((tm, LANES), row), _full((1, d)),
                  pl.BlockSpec((1, d, de), lambda i, e: (e, 0, 0)), pl.BlockSpec((1, d, de), lambda i, e: (e, 0, 0)),
                  pl.BlockSpec((1, de, d), lambda i, e: (e, 0, 0)), _full((1, d))],
        out_specs=pl.BlockSpec((tm, d), row),
        out_shape=jax.ShapeDtypeStruct((t, d), F32),
        scratch_shapes=[pltpu.VMEM((tm, d), BF16), pltpu.VMEM((tm, d), F32)],
        compiler_params=_params(("parallel", "arbitrary")),
        name="moe",
    )(x1, comb, ffn_g.reshape(1, d), w_eg.astype(BF16), w_eu.astype(BF16), w_ed.astype(BF16), final_g.reshape(1, d))


def _tile(n, want):
    while n % want:
        want //= 2
    return want


def _trunk(x, attn_g, w_in, sink, gq, gk, w_oa, w_ob, w_out, ffn_g, w_rg, b_rg, w_re, b_re, w_eg, w_eu, w_ed,
           final_g):
    b, s, d = x.shape
    x2 = x.reshape(b * s, d)
    qa, ka, va, qb, kb, vb, ga, gb = _in_proj(x2, s, attn_g, w_in, gq, gk, _tile(s, 512))
    ya = _win_attn(qa, ka, va, sink, s, _tile(s, 512))
    yb = _glob_attn(qb.reshape(b, s, Q_W), kb.reshape(b, s, KV_W), vb.reshape(b, s, KV_W),
                    _tile(s, 512), _tile(s, 1024)).reshape(b * s, Q_W)
    x1, comb = _merge(x2, ya, yb, ga, gb, w_oa, w_ob, w_out, ffn_g, w_rg, b_rg, w_re, b_re, _tile(b * s, 512))
    y = _moe(x1, comb, ffn_g, w_eg, w_eu, w_ed, final_g, _tile(b * s, 1024))
    return y.reshape(b, s, d)


def kernel(x_prompt, x_sample, attn_norm_g, w_in, a_sink, b_q_norm_g, b_k_norm_g, w_oa, w_ob, w_out, ffn_norm_g,
           w_router_group, b_router_group, w_router_expert, b_router_expert, w_expert_gate, w_expert_up,
           w_expert_down, final_norm_g):
    assert attn_norm_g.shape[0] == 1, "single-layer trunk"
    weights = (attn_norm_g[0], w_in[0], a_sink[0], b_q_norm_g[0], b_k_norm_g[0], w_oa[0], w_ob[0], w_out[0],
               ffn_norm_g[0], w_router_group[0], b_router_group[0], w_router_expert[0], b_router_expert[0],
               w_expert_gate[0], w_expert_up[0], w_expert_down[0], final_norm_g)
    return (_trunk(x_prompt, *weights), _trunk(x_sample, *weights))
```

```python
import functools

import jax
import jax.numpy as jnp
import numpy as np
from jax import lax
from jax.experimental import pallas as pl
from jax.experimental.pallas import tpu as pltpu

HEAD_DIM = 64
Q_HEADS = 8
KV_HEADS = 2
GROUP = Q_HEADS // KV_HEADS
Q_W = Q_HEADS * HEAD_DIM
KV_W = KV_HEADS * HEAD_DIM
WINDOW = 128
BLOCK = 128
GRID_W = 64
ROPE_THETA = 10000.0
N_GROUPS = 4
EXPERTS_PER_GROUP = 4
N_EXPERTS = N_GROUPS * EXPERTS_PER_GROUP
EPS = 1e-6
NEG_INF = -1e30
SCALE = HEAD_DIM ** -0.5
LOG2E = 1.4426950408889634
ONES_ROWS = 16
VT_ROWS = HEAD_DIM + ONES_ROWS
LANES = 128

VMEM_LIMIT = 56 * 1024 * 1024

F32 = jnp.float32
BF16 = jnp.bfloat16


def _params(sem):
    return pltpu.CompilerParams(dimension_semantics=sem, vmem_limit_bytes=VMEM_LIMIT)


def _full(shape):
    return pl.BlockSpec(shape, lambda *_: (0,) * len(shape))


def _split_dot(y, ones_bd):
    hi = y.astype(BF16)
    lo = (y - hi.astype(F32)).astype(BF16)
    return jnp.dot(hi, ones_bd, preferred_element_type=F32) + jnp.dot(lo, ones_bd, preferred_element_type=F32)


def _pair_swap(y):
    n = y.shape[-1]
    lane = lax.broadcasted_iota(jnp.int32, y.shape, y.ndim - 1)
    nxt = pltpu.roll(y, n - 1, y.ndim - 1)
    prv = pltpu.roll(y, 1, y.ndim - 1)
    return jnp.where((lane & 1) == 0, nxt, prv)


def _norm_rope(y, gain, ones_bd, cos, sin):
    ms = _split_dot(y * y, ones_bd) * (1.0 / HEAD_DIM)
    yn = y * lax.rsqrt(ms + EPS) * gain
    return yn * cos + _pair_swap(yn) * sin


def _in_proj_kernel(x_ref, g_ref, w_ref, cos_ref, sin_ref, gq_ref, gk_ref, oq_ref, ok_ref,
                    qa_ref, ka_ref, va_ref, qb_ref, kb_ref, vb_ref, ga_ref, gb_ref):
    x = x_ref[...]
    ms = jnp.mean(x * x, axis=-1, keepdims=True)
    h = (x * lax.rsqrt(ms + EPS) * g_ref[...]).astype(BF16)

    def proj(lo, width):
        return jnp.dot(h, w_ref[:, lo:lo + width], preferred_element_type=F32)

    o = 0
    qa_ref[...] = (proj(o, Q_W) * SCALE).astype(BF16); o += Q_W
    ka_ref[...] = proj(o, KV_W).astype(BF16); o += KV_W
    va_ref[...] = proj(o, KV_W).astype(BF16); o += KV_W
    cos = cos_ref[...]
    sin = sin_ref[...]
    qb = proj(o, Q_W); o += Q_W
    cos_q = jnp.concatenate([cos] * (Q_W // LANES), axis=1)
    sin_q = jnp.concatenate([sin] * (Q_W // LANES), axis=1)
    qb_ref[...] = (_norm_rope(qb, gq_ref[...], oq_ref[...], cos_q, sin_q) * (SCALE * LOG2E)).astype(BF16)
    kb = proj(o, KV_W); o += KV_W
    kb_ref[...] = _norm_rope(kb, gk_ref[...], ok_ref[...], cos, sin).astype(BF16)
    vb_ref[...] = proj(o, KV_W).astype(BF16); o += KV_W
    d = x.shape[-1]
    ga_ref[...] = jax.nn.sigmoid(proj(o, d)).astype(BF16); o += d
    gb_ref[...] = jax.nn.sigmoid(proj(o, d)).astype(BF16)


def _rope_tables(seq_len):
    rows = seq_len // GRID_W
    row = jnp.repeat(jnp.arange(rows, dtype=F32), GRID_W)
    col = jnp.tile(jnp.arange(GRID_W, dtype=F32), rows)
    half = HEAD_DIM // 2
    inv = ROPE_THETA ** (-jnp.arange(0, half, 2, dtype=F32) / half)
    ang = jnp.concatenate([row[:, None] * inv, col[:, None] * inv], axis=-1)
    cos = jnp.repeat(jnp.cos(ang), 2, axis=-1)
    sin = jnp.repeat(jnp.sin(ang), 2, axis=-1) * jnp.tile(jnp.array([-1.0, 1.0], F32), half)
    reps = LANES // HEAD_DIM
    return jnp.tile(cos, (1, reps)), jnp.tile(sin, (1, reps))


def _block_ones(width):
    idx = np.arange(width) // HEAD_DIM
    return jnp.asarray(idx[:, None] == idx[None, :], dtype=BF16)


def _in_proj(x2, seq_len, attn_g, w_in, gq, gk, tm):
    t, d = x2.shape
    in_w = w_in.shape[1]
    cos, sin = _rope_tables(seq_len)
    nseq = seq_len // tm
    row = lambda i: (i, 0)
    pos = lambda i: (i % nseq, 0)
    outs = [(Q_W, BF16), (KV_W, BF16), (KV_W, BF16), (Q_W, BF16), (KV_W, BF16), (KV_W, BF16), (d, BF16), (d, BF16)]
    return pl.pallas_call(
        _in_proj_kernel,
        grid=(t // tm,),
        in_specs=[pl.BlockSpec((tm, d), row), _full((1, d)), _full((d, in_w)),
                  pl.BlockSpec((tm, LANES), pos), pl.BlockSpec((tm, LANES), pos),
                  _full((1, Q_W)), _full((1, KV_W)), _full((Q_W, Q_W)), _full((KV_W, KV_W))],
        out_specs=[pl.BlockSpec((tm, w), row) for w, _ in outs],
        out_shape=[jax.ShapeDtypeStruct((t, w), dt) for w, dt in outs],
        compiler_params=_params(("parallel",)),
        name="in_proj",
    )(x2, attn_g.reshape(1, d), w_in.astype(BF16), cos, sin,
      jnp.tile(gq.astype(F32), Q_HEADS).reshape(1, Q_W), jnp.tile(gk.astype(F32), KV_HEADS).reshape(1, KV_W),
      _block_ones(Q_W), _block_ones(KV_W))


def _alibi_bias():
    slopes = np.exp2(-8.0 * np.arange(1, Q_HEADS + 1, dtype=np.float32) / Q_HEADS).astype(np.float32)
    qi = np.arange(BLOCK)[:, None]
    kj = np.arange(3 * BLOCK)[None, :] - BLOCK
    dist = np.abs(qi - kj)
    bias = -slopes[:, None, None] * dist[None].astype(np.float32)
    return jnp.asarray(np.where(dist[None] <= WINDOW, bias, np.float32(NEG_INF)), dtype=F32)


def _win_attn_kernel(sink_ref, q_ref, kc_ref, kp_ref, kn_ref, vc_ref, vp_ref, vn_ref, bias_ref, o_ref,
                     kwin, vwin, *, blocks_per_seq, sub):
    i = pl.program_id(0)
    tq = q_ref.shape[0]
    kwin[0:BLOCK] = kp_ref[...]
    kwin[BLOCK:BLOCK + tq] = kc_ref[...]
    kwin[BLOCK + tq:] = kn_ref[...]
    vwin[0:BLOCK] = vp_ref[...]
    vwin[BLOCK:BLOCK + tq] = vc_ref[...]
    vwin[BLOCK + tq:] = vn_ref[...]
    col = lax.broadcasted_iota(jnp.int32, (BLOCK, 3 * BLOCK), 1)
    for j in range(sub):
        blk = (i * sub + j) % blocks_per_seq
        dead = ((col < BLOCK) & (blk == 0)) | ((col >= 2 * BLOCK) & (blk == blocks_per_seq - 1))
        outs = []
        for h in range(Q_HEADS):
            g = h // GROUP
            q = q_ref[j * BLOCK:(j + 1) * BLOCK, h * HEAD_DIM:(h + 1) * HEAD_DIM]
            k = kwin[j * BLOCK:(j + 3) * BLOCK, g * HEAD_DIM:(g + 1) * HEAD_DIM]
            v = vwin[j * BLOCK:(j + 3) * BLOCK, g * HEAD_DIM:(g + 1) * HEAD_DIM]
            s = lax.dot_general(q, k, (((1,), (1,)), ((), ())), preferred_element_type=F32)
            logits = jnp.where(dead, NEG_INF, s + bias_ref[h])
            sink = sink_ref[h]
            m = jnp.maximum(jnp.max(logits, axis=-1, keepdims=True), sink)
            p = jnp.exp(logits - m)
            denom = jnp.sum(p, axis=-1, keepdims=True) + jnp.exp(sink - m)
            pv = jnp.dot(p.astype(BF16), v, preferred_element_type=F32)
            outs.append(pv / denom)
        o_ref[j * BLOCK:(j + 1) * BLOCK, :] = jnp.concatenate(outs, axis=1).astype(o_ref.dtype)


def _win_attn(qa, ka, va, sink, seq_len, tq):
    t = qa.shape[0]
    sub = tq // BLOCK
    nblk = t // BLOCK
    row = lambda i: (i, 0)
    prev = lambda i: (jnp.maximum(i * sub - 1, 0), 0)
    nxt = lambda i: (jnp.minimum((i + 1) * sub, nblk - 1), 0)
    kv_specs = [pl.BlockSpec((tq, KV_W), row), pl.BlockSpec((BLOCK, KV_W), prev), pl.BlockSpec((BLOCK, KV_W), nxt)]
    return pl.pallas_call(
        functools.partial(_win_attn_kernel, blocks_per_seq=seq_len // BLOCK, sub=sub),
        grid=(t // tq,),
        in_specs=[pl.BlockSpec(memory_space=pltpu.SMEM), pl.BlockSpec((tq, Q_W), row)] + kv_specs + kv_specs
                 + [_full((Q_HEADS, BLOCK, 3 * BLOCK))],
        out_specs=pl.BlockSpec((tq, Q_W), row),
        out_shape=jax.ShapeDtypeStruct((t, Q_W), BF16),
        scratch_shapes=[pltpu.VMEM((tq + 2 * BLOCK, KV_W), BF16), pltpu.VMEM((tq + 2 * BLOCK, KV_W), BF16)],
        compiler_params=_params(("parallel",)),
        name="win_attn",
    )(sink.astype(F32), qa, ka, ka, ka, va, va, va, _alibi_bias())


def _glob_attn_kernel(q_ref, k_ref, v_ref, o_ref, vt_ref, qt_scr, s_scr, p_scr, cm_scr, m_scr, a_scr, acc_scr, ot_scr,
                      *, tk):
    tq = q_ref.shape[1]
    nk = k_ref.shape[1] // tk

    @pl.when(pl.program_id(1) == 0)
    def _():
        ones = jnp.ones((ONES_ROWS, tk), BF16)

        def fill(c, carry):
            off = pl.multiple_of(c * tk, tk)
            vt = v_ref[0, pl.ds(off, tk), :].astype(F32).T.astype(BF16)
            for g in range(KV_HEADS):
                vt_ref[g * VT_ROWS:g * VT_ROWS + HEAD_DIM, pl.ds(off, tk)] = vt[g * HEAD_DIM:(g + 1) * HEAD_DIM, :]
                vt_ref[g * VT_ROWS + HEAD_DIM:(g + 1) * VT_ROWS, pl.ds(off, tk)] = ones
            return carry

        lax.fori_loop(0, nk, fill, 0)

    qt = q_ref[0].astype(F32).T.astype(BF16)
    qt_scr[...] = jnp.zeros_like(qt_scr)
    for h in range(Q_HEADS):
        g = h // GROUP
        qt_scr[h, g * HEAD_DIM:(g + 1) * HEAD_DIM, :] = qt[h * HEAD_DIM:(h + 1) * HEAD_DIM, :]

    for g in range(KV_HEADS):
        heads = range(g * GROUP, (g + 1) * GROUP)

        def stage_a(c, x, heads=heads):
            kc = k_ref[0, pl.ds(pl.multiple_of(c * tk, tk), tk), :]
            for i, h in enumerate(heads):
                s = jnp.dot(kc, qt_scr[h], preferred_element_type=F32)
                s_scr[x, i] = s
                cm_scr[x, i] = jnp.max(s, axis=0, keepdims=True)

        def stage_b(x):
            for i in range(GROUP):
                m = m_scr[i]
                m_new = jnp.maximum(m, cm_scr[x, i])
                a_scr[x, i] = jnp.exp2(m - m_new)
                m_scr[i] = m_new
                p_scr[x, i] = jnp.exp2(s_scr[x, i] - m_new).astype(BF16)

        def stage_c(c, x, g=g):
            vt = vt_ref[g * VT_ROWS:(g + 1) * VT_ROWS, pl.ds(pl.multiple_of(c * tk, tk), tk)]
            for i in range(GROUP):
                acc_scr[i] = a_scr[x, i] * acc_scr[i] + jnp.dot(vt, p_scr[x, i], preferred_element_type=F32)

        def step(c, x, last=False):
            stage_c(c - 1, 1 - x)
            stage_b(x)
            if not last:
                stage_a(c + 1, 1 - x)

        for i in range(GROUP):
            m_scr[i] = jnp.full((1, tq), -jnp.inf, F32)
            acc_scr[i] = jnp.zeros((VT_ROWS, tq), F32)
        stage_a(0, 0)
        stage_b(0)
        stage_a(1, 1)

        def pair(j, carry):
            step(2 * j + 1, 1)
            step(2 * j + 2, 0)
            return carry

        lax.fori_loop(0, (nk - 2) // 2, pair, 0)
        step(nk - 1, 1, last=True)
        stage_c(nk - 1, 1)
        for i, h in enumerate(heads):
            acc = acc_scr[i]
            ot_scr[h * HEAD_DIM:(h + 1) * HEAD_DIM, :] = acc[0:HEAD_DIM] / acc[HEAD_DIM:HEAD_DIM + 1]
    o_ref[0] = ot_scr[...].T.astype(o_ref.dtype)


def _glob_attn(qb, kb, vb, tq, tk):
    b, s, _ = qb.shape
    assert s % (2 * tk) == 0, "the chunk pipeline walks key chunks in pairs"
    return pl.pallas_call(
        functools.partial(_glob_attn_kernel, tk=tk),
        grid=(b, s // tq),
        in_specs=[pl.BlockSpec((1, tq, Q_W), lambda bi, qi: (bi, qi, 0)),
                  pl.BlockSpec((1, s, KV_W), lambda bi, qi: (bi, 0, 0)),
                  pl.BlockSpec((1, s, KV_W), lambda bi, qi: (bi, 0, 0))],
        out_specs=pl.BlockSpec((1, tq, Q_W), lambda bi, qi: (bi, qi, 0)),
        out_shape=jax.ShapeDtypeStruct((b, s, Q_W), BF16),
        scratch_shapes=[pltpu.VMEM((KV_HEADS * VT_ROWS, s), BF16), pltpu.VMEM((Q_HEADS, KV_W, tq), BF16),
                        pltpu.VMEM((2, GROUP, tk, tq), F32), pltpu.VMEM((2, GROUP, tk, tq), BF16),
                        pltpu.VMEM((2, GROUP, 1, tq), F32), pltpu.VMEM((GROUP, 1, tq), F32),
                        pltpu.VMEM((2, GROUP, 1, tq), F32), pltpu.VMEM((GROUP, VT_ROWS, tq), F32),
                        pltpu.VMEM((Q_W, tq), F32)],
        compiler_params=_params(("arbitrary", "arbitrary")),
        name="glob_attn",
    )(qb, kb, vb)


def _first_argmax(vals, idx, big):
    m = jnp.max(vals, axis=-1, keepdims=True)
    return m, jnp.min(jnp.where(vals == m, idx, big), axis=-1, keepdims=True)


def _route(logits):
    rows = logits.shape[0]
    gl = logits[:, 0:N_GROUPS]
    gidx = lax.broadcasted_iota(jnp.int32, (rows, N_GROUPS), 1)
    gmax, g_sel = _first_argmax(gl, gidx, N_GROUPS)
    g_w = 1.0 / jnp.sum(jnp.exp(gl - gmax), axis=-1, keepdims=True)
    e_sel = jnp.zeros((rows, EXPERTS_PER_GROUP), F32)
    for g in range(N_GROUPS):
        lo = N_GROUPS + g * EXPERTS_PER_GROUP
        e_sel = jnp.where(g_sel == g, logits[:, lo:lo + EXPERTS_PER_GROUP], e_sel)
    eidx = lax.broadcasted_iota(jnp.int32, (rows, EXPERTS_PER_GROUP), 1)
    ex = jnp.exp(e_sel - jnp.max(e_sel, axis=-1, keepdims=True))
    e_prob = ex / jnp.sum(ex, axis=-1, keepdims=True)
    p1, i1 = _first_argmax(e_prob, eidx, EXPERTS_PER_GROUP)
    rest = jnp.where(eidx == i1, -1.0, e_prob)
    p2, i2 = _first_argmax(rest, eidx, EXPERTS_PER_GROUP)
    tot = p1 + p2
    lane = lax.broadcasted_iota(jnp.int32, (rows, LANES), 1)
    id1 = g_sel * EXPERTS_PER_GROUP + i1
    id2 = g_sel * EXPERTS_PER_GROUP + i2
    return jnp.where(lane == id1, g_w * (p1 / tot), 0.0) + jnp.where(lane == id2, g_w * (p2 / tot), 0.0)


def _merge_kernel(x_ref, ya_ref, yb_ref, ga_ref, gb_ref, woa_ref, wob_ref, wout_ref, gf_ref, wr_ref, br_ref,
                  x1_ref, comb_ref):
    a = jnp.dot(ya_ref[...], woa_ref[...], preferred_element_type=F32)
    b = jnp.dot(yb_ref[...], wob_ref[...], preferred_element_type=F32)
    merged = ga_ref[...].astype(F32) * a + gb_ref[...].astype(F32) * b
    x1 = x_ref[...] + jnp.dot(merged.astype(BF16), wout_ref[...], preferred_element_type=F32)
    x1_ref[...] = x1
    ms = jnp.mean(x1 * x1, axis=-1, keepdims=True)
    t = x1 * lax.rsqrt(ms + EPS) * gf_ref[...]
    logits = jnp.dot(t, wr_ref[...], preferred_element_type=F32, precision=lax.Precision.HIGHEST) + br_ref[...]
    comb_ref[...] = _route(logits)


def _merge(x2, ya, yb, ga, gb, w_oa, w_ob, w_out, ffn_g, w_rg, b_rg, w_re, b_re, tm):
    t, d = x2.shape
    wr = jnp.zeros((d, LANES), F32).at[:, :N_GROUPS].set(w_rg).at[:, N_GROUPS:N_GROUPS + N_EXPERTS].set(w_re)
    br = jnp.zeros((1, LANES), F32).at[0, :N_GROUPS].set(b_rg).at[0, N_GROUPS:N_GROUPS + N_EXPERTS].set(b_re)
    row = lambda i: (i, 0)
    return pl.pallas_call(
        _merge_kernel,
        grid=(t // tm,),
        in_specs=[pl.BlockSpec((tm, d), row), pl.BlockSpec((tm, Q_W), row), pl.BlockSpec((tm, Q_W), row),
                  pl.BlockSpec((tm, d), row), pl.BlockSpec((tm, d), row),
                  _full((Q_W, d)), _full((Q_W, d)), _full((d, d)), _full((1, d)), _full((d, LANES)),
                  _full((1, LANES))],
        out_specs=[pl.BlockSpec((tm, d), row), pl.BlockSpec((tm, LANES), row)],
        out_shape=[jax.ShapeDtypeStruct((t, d), F32), jax.ShapeDtypeStruct((t, LANES), F32)],
        compiler_params=_params(("parallel",)),
        name="merge",
    )(x2, ya, yb, ga, gb, w_oa.astype(BF16), w_ob.astype(BF16), w_out.astype(BF16), ffn_g.reshape(1, d), wr, br)


def _moe_kernel(x1_ref, comb_ref, gf_ref, wg_ref, wu_ref, wd_ref, gfin_ref, o_ref, t_scr, y_scr):
    e = pl.program_id(1)

    @pl.when(e == 0)
    def _():
        x1 = x1_ref[...]
        ms = jnp.mean(x1 * x1, axis=-1, keepdims=True)
        t_scr[...] = (x1 * lax.rsqrt(ms + EPS) * gf_ref[...]).astype(BF16)
        y_scr[...] = jnp.zeros_like(y_scr)

    t = t_scr[...]
    a = jax.nn.silu(jnp.dot(t, wg_ref[0], preferred_element_type=F32)) * jnp.dot(t, wu_ref[0],
                                                                                preferred_element_type=F32)
    lane = lax.broadcasted_iota(jnp.int32, comb_ref.shape, 1)
    w = jnp.sum(jnp.where(lane == e, comb_ref[...], 0.0), axis=-1, keepdims=True)
    y_scr[...] += w * jnp.dot(a.astype(BF16), wd_ref[0], preferred_element_type=F32)

    @pl.when(e == pl.num_programs(1) - 1)
    def _():
        x2 = x1_ref[...] + y_scr[...]
        ms = jnp.mean(x2 * x2, axis=-1, keepdims=True)
        o_ref[...] = x2 * lax.rsqrt(ms + EPS) * gfin_ref[...]


def _moe(x1, comb, ffn_g, w_eg, w_eu, w_ed, final_g, tm):
    t, d = x1.shape
    de = w_eg.shape[-1]
    row = lambda i, e: (i, 0)
    return pl.pallas_call(
        _moe_kernel,
        grid=(t // tm, N_EXPERTS),
        in_specs=[pl.BlockSpec((tm, d), row), pl.BlockSpec((tm, LANES), row), _full((1, d)),
                  pl.BlockSpec((1, d, de), lambda i, e: (e, 0, 0)), pl.BlockSpec((1, d, de), lambda i, e: (e, 0, 0)),
                  pl.BlockSpec((1, de, d), lambda i, e: (e, 0, 0)), _full((1, d))],
        out_specs=pl.BlockSpec((tm, d), row),
        out_shape=jax.ShapeDtypeStruct((t, d), F32),
        scratch_shapes=[pltpu.VMEM((tm, d), BF16), pltpu.VMEM((tm, d), F32)],
        compiler_params=_params(("parallel", "arbitrary")),
        name="moe",
    )(x1, comb, ffn_g.reshape(1, d), w_eg.astype(BF16), w_eu.astype(BF16), w_ed.astype(BF16), final_g.reshape(1, d))


def _tile(n, want):
    while n % want:
        want //= 2
    return want


def _trunk(x, attn_g, w_in, sink, gq, gk, w_oa, w_ob, w_out, ffn_g, w_rg, b_rg, w_re, b_re, w_eg, w_eu, w_ed,
           final_g):
    b, s, d = x.shape
    x2 = x.reshape(b * s, d)
    qa, ka, va, qb, kb, vb, ga, gb = _in_proj(x2, s, attn_g, w_in, gq, gk, _tile(s, 512))
    ya = _win_attn(qa, ka, va, sink, s, _tile(s, 512))
    yb = _glob_attn(qb.reshape(b, s, Q_W), kb.reshape(b, s, KV_W), vb.reshape(b, s, KV_W),
                    256, 256).reshape(b * s, Q_W)
    x1, comb = _merge(x2, ya, yb, ga, gb, w_oa, w_ob, w_out, ffn_g, w_rg, b_rg, w_re, b_re, _tile(b * s, 512))
    y = _moe(x1, comb, ffn_g, w_eg, w_eu, w_ed, final_g, _tile(b * s, 1024))
    return y.reshape(b, s, d)


def kernel(x_prompt, x_sample, attn_norm_g, w_in, a_sink, b_q_norm_g, b_k_norm_g, w_oa, w_ob, w_out, ffn_norm_g,
           w_router_group, b_router_group, w_router_expert, b_router_expert, w_expert_gate, w_expert_up,
           w_expert_down, final_norm_g):
    assert attn_norm_g.shape[0] == 1, "single-layer trunk"
    weights = (attn_norm_g[0], w_in[0], a_sink[0], b_q_norm_g[0], b_k_norm_g[0], w_oa[0], w_ob[0], w_out[0],
               ffn_norm_g[0], w_router_group[0], b_router_group[0], w_router_expert[0], b_router_expert[0],
               w_expert_gate[0], w_expert_up[0], w_expert_down[0], final_norm_g)
    return (_trunk(x_prompt, *weights), _trunk(x_sample, *weights))
```

```python
import functools

import jax
import jax.numpy as jnp
import numpy as np
from jax import lax
from jax.experimental import pallas as pl
from jax.experimental.pallas import tpu as pltpu

HEAD_DIM = 64
Q_HEADS = 8
KV_HEADS = 2
GROUP = Q_HEADS // KV_HEADS
Q_W = Q_HEADS * HEAD_DIM
KV_W = KV_HEADS * HEAD_DIM
WINDOW = 128
BLOCK = 128
GRID_W = 64
ROPE_THETA = 10000.0
N_GROUPS = 4
EXPERTS_PER_GROUP = 4
N_EXPERTS = N_GROUPS * EXPERTS_PER_GROUP
EPS = 1e-6
NEG_INF = -1e30
SCALE = HEAD_DIM ** -0.5
LOG2E = 1.4426950408889634
ONES_ROWS = 16
VT_ROWS = HEAD_DIM + ONES_ROWS
LANES = 128

VMEM_LIMIT = 56 * 1024 * 1024

F32 = jnp.float32
BF16 = jnp.bfloat16


def _params(sem):
    return pltpu.CompilerParams(dimension_semantics=sem, vmem_limit_bytes=VMEM_LIMIT)


def _full(shape):
    return pl.BlockSpec(shape, lambda *_: (0,) * len(shape))


def _split_dot(y, ones_bd):
    hi = y.astype(BF16)
    lo = (y - hi.astype(F32)).astype(BF16)
    return jnp.dot(hi, ones_bd, preferred_element_type=F32) + jnp.dot(lo, ones_bd, preferred_element_type=F32)


def _pair_swap(y):
    n = y.shape[-1]
    lane = lax.broadcasted_iota(jnp.int32, y.shape, y.ndim - 1)
    nxt = pltpu.roll(y, n - 1, y.ndim - 1)
    prv = pltpu.roll(y, 1, y.ndim - 1)
    return jnp.where((lane & 1) == 0, nxt, prv)


def _norm_rope(y, gain, ones_bd, cos, sin):
    ms = _split_dot(y * y, ones_bd) * (1.0 / HEAD_DIM)
    yn = y * lax.rsqrt(ms + EPS) * gain
    return yn * cos + _pair_swap(yn) * sin


def _in_proj_kernel(x_ref, g_ref, w_ref, cos_ref, sin_ref, gq_ref, gk_ref, oq_ref, ok_ref,
                    qa_ref, ka_ref, va_ref, qb_ref, kb_ref, vb_ref, ga_ref, gb_ref):
    x = x_ref[...]
    ms = jnp.mean(x * x, axis=-1, keepdims=True)
    h = (x * lax.rsqrt(ms + EPS) * g_ref[...]).astype(BF16)

    def proj(lo, width):
        return jnp.dot(h, w_ref[:, lo:lo + width], preferred_element_type=F32)

    o = 0
    qa_ref[...] = (proj(o, Q_W) * SCALE).astype(BF16); o += Q_W
    ka_ref[...] = proj(o, KV_W).astype(BF16); o += KV_W
    va_ref[...] = proj(o, KV_W).astype(BF16); o += KV_W
    cos = cos_ref[...]
    sin = sin_ref[...]
    qb = proj(o, Q_W); o += Q_W
    cos_q = jnp.concatenate([cos] * (Q_W // LANES), axis=1)
    sin_q = jnp.concatenate([sin] * (Q_W // LANES), axis=1)
    qb_ref[...] = (_norm_rope(qb, gq_ref[...], oq_ref[...], cos_q, sin_q) * (SCALE * LOG2E)).astype(BF16)
    kb = proj(o, KV_W); o += KV_W
    kb_ref[...] = _norm_rope(kb, gk_ref[...], ok_ref[...], cos, sin).astype(BF16)
    vb_ref[...] = proj(o, KV_W).astype(BF16); o += KV_W
    d = x.shape[-1]
    ga_ref[...] = jax.nn.sigmoid(proj(o, d)).astype(BF16); o += d
    gb_ref[...] = jax.nn.sigmoid(proj(o, d)).astype(BF16)


def _rope_tables(seq_len):
    rows = seq_len // GRID_W
    row = jnp.repeat(jnp.arange(rows, dtype=F32), GRID_W)
    col = jnp.tile(jnp.arange(GRID_W, dtype=F32), rows)
    half = HEAD_DIM // 2
    inv = ROPE_THETA ** (-jnp.arange(0, half, 2, dtype=F32) / half)
    ang = jnp.concatenate([row[:, None] * inv, col[:, None] * inv], axis=-1)
    cos = jnp.repeat(jnp.cos(ang), 2, axis=-1)
    sin = jnp.repeat(jnp.sin(ang), 2, axis=-1) * jnp.tile(jnp.array([-1.0, 1.0], F32), half)
    reps = LANES // HEAD_DIM
    return jnp.tile(cos, (1, reps)), jnp.tile(sin, (1, reps))


def _block_ones(width):
    idx = np.arange(width) // HEAD_DIM
    return jnp.asarray(idx[:, None] == idx[None, :], dtype=BF16)


def _in_proj(x2, seq_len, attn_g, w_in, gq, gk, tm):
    t, d = x2.shape
    in_w = w_in.shape[1]
    cos, sin = _rope_tables(seq_len)
    nseq = seq_len // tm
    row = lambda i: (i, 0)
    pos = lambda i: (i % nseq, 0)
    outs = [(Q_W, BF16), (KV_W, BF16), (KV_W, BF16), (Q_W, BF16), (KV_W, BF16), (KV_W, BF16), (d, BF16), (d, BF16)]
    return pl.pallas_call(
        _in_proj_kernel,
        grid=(t // tm,),
        in_specs=[pl.BlockSpec((tm, d), row), _full((1, d)), _full((d, in_w)),
                  pl.BlockSpec((tm, LANES), pos), pl.BlockSpec((tm, LANES), pos),
                  _full((1, Q_W)), _full((1, KV_W)), _full((Q_W, Q_W)), _full((KV_W, KV_W))],
        out_specs=[pl.BlockSpec((tm, w), row) for w, _ in outs],
        out_shape=[jax.ShapeDtypeStruct((t, w), dt) for w, dt in outs],
        compiler_params=_params(("parallel",)),
        name="in_proj",
    )(x2, attn_g.reshape(1, d), w_in.astype(BF16), cos, sin,
      jnp.tile(gq.astype(F32), Q_HEADS).reshape(1, Q_W), jnp.tile(gk.astype(F32), KV_HEADS).reshape(1, KV_W),
      _block_ones(Q_W), _block_ones(KV_W))


def _alibi_bias():
    slopes = np.exp2(-8.0 * np.arange(1, Q_HEADS + 1, dtype=np.float32) / Q_HEADS).astype(np.float32)
    qi = np.arange(BLOCK)[:, None]
    kj = np.arange(3 * BLOCK)[None, :] - BLOCK
    dist = np.abs(qi - kj)
    bias = -slopes[:, None, None] * dist[None].astype(np.float32)
    return jnp.asarray(np.where(dist[None] <= WINDOW, bias, np.float32(NEG_INF)), dtype=F32)


def _win_attn_kernel(sink_ref, q_ref, kc_ref, kp_ref, kn_ref, vc_ref, vp_ref, vn_ref, bias_ref, o_ref,
                     kwin, vwin, *, blocks_per_seq, sub):
    i = pl.program_id(0)
    tq = q_ref.shape[0]
    kwin[0:BLOCK] = kp_ref[...]
    kwin[BLOCK:BLOCK + tq] = kc_ref[...]
    kwin[BLOCK + tq:] = kn_ref[...]
    vwin[0:BLOCK] = vp_ref[...]
    vwin[BLOCK:BLOCK + tq] = vc_ref[...]
    vwin[BLOCK + tq:] = vn_ref[...]
    col = lax.broadcasted_iota(jnp.int32, (BLOCK, 3 * BLOCK), 1)
    for j in range(sub):
        blk = (i * sub + j) % blocks_per_seq
        dead = ((col < BLOCK) & (blk == 0)) | ((col >= 2 * BLOCK) & (blk == blocks_per_seq - 1))
        outs = []
        for h in range(Q_HEADS):
            g = h // GROUP
            q = q_ref[j * BLOCK:(j + 1) * BLOCK, h * HEAD_DIM:(h + 1) * HEAD_DIM]
            k = kwin[j * BLOCK:(j + 3) * BLOCK, g * HEAD_DIM:(g + 1) * HEAD_DIM]
            v = vwin[j * BLOCK:(j + 3) * BLOCK, g * HEAD_DIM:(g + 1) * HEAD_DIM]
            s = lax.dot_general(q, k, (((1,), (1,)), ((), ())), preferred_element_type=F32)
            logits = jnp.where(dead, NEG_INF, s + bias_ref[h])
            sink = sink_ref[h]
            m = jnp.maximum(jnp.max(logits, axis=-1, keepdims=True), sink)
            p = jnp.exp(logits - m)
            denom = jnp.sum(p, axis=-1, keepdims=True) + jnp.exp(sink - m)
            pv = jnp.dot(p.astype(BF16), v, preferred_element_type=F32)
            outs.append(pv / denom)
        o_ref[j * BLOCK:(j + 1) * BLOCK, :] = jnp.concatenate(outs, axis=1).astype(o_ref.dtype)


def _win_attn(qa, ka, va, sink, seq_len, tq):
    t = qa.shape[0]
    sub = tq // BLOCK
    nblk = t // BLOCK
    row = lambda i: (i, 0)
    prev = lambda i: (jnp.maximum(i * sub - 1, 0), 0)
    nxt = lambda i: (jnp.minimum((i + 1) * sub, nblk - 1), 0)
    kv_specs = [pl.BlockSpec((tq, KV_W), row), pl.BlockSpec((BLOCK, KV_W), prev), pl.BlockSpec((BLOCK, KV_W), nxt)]
    return pl.pallas_call(
        functools.partial(_win_attn_kernel, blocks_per_seq=seq_len // BLOCK, sub=sub),
        grid=(t // tq,),
        in_specs=[pl.BlockSpec(memory_space=pltpu.SMEM), pl.BlockSpec((tq, Q_W), row)] + kv_specs + kv_specs
                 + [_full((Q_HEADS, BLOCK, 3 * BLOCK))],
        out_specs=pl.BlockSpec((tq, Q_W), row),
        out_shape=jax.ShapeDtypeStruct((t, Q_W), BF16),
        scratch_shapes=[pltpu.VMEM((tq + 2 * BLOCK, KV_W), BF16), pltpu.VMEM((tq + 2 * BLOCK, KV_W), BF16)],
        compiler_params=_params(("parallel",)),
        name="win_attn",
    )(sink.astype(F32), qa, ka, ka, ka, va, va, va, _alibi_bias())


def _glob_attn_kernel(q_ref, k_ref, v_ref, o_ref, vt_ref, qt_scr, s_scr, p_scr, cm_scr, m_scr, a_scr, acc_scr, ot_scr,
                      *, tk):
    tq = q_ref.shape[1]
    nk = k_ref.shape[1] // tk

    @pl.when(pl.program_id(1) == 0)
    def _():
        ones = jnp.ones((ONES_ROWS, tk), BF16)

        def fill(c, carry):
            off = pl.multiple_of(c * tk, tk)
            vt = v_ref[0, pl.ds(off, tk), :].astype(F32).T.astype(BF16)
            for g in range(KV_HEADS):
                vt_ref[g * VT_ROWS:g * VT_ROWS + HEAD_DIM, pl.ds(off, tk)] = vt[g * HEAD_DIM:(g + 1) * HEAD_DIM, :]
                vt_ref[g * VT_ROWS + HEAD_DIM:(g + 1) * VT_ROWS, pl.ds(off, tk)] = ones
            return carry

        lax.fori_loop(0, nk, fill, 0)

    qt = q_ref[0].astype(F32).T.astype(BF16)
    qt_scr[...] = jnp.zeros_like(qt_scr)
    for h in range(Q_HEADS):
        g = h // GROUP
        qt_scr[h, g * HEAD_DIM:(g + 1) * HEAD_DIM, :] = qt[h * HEAD_DIM:(h + 1) * HEAD_DIM, :]

    for g in range(KV_HEADS):
        heads = range(g * GROUP, (g + 1) * GROUP)

        def stage_a(c, x, heads=heads):
            kc = k_ref[0, pl.ds(pl.multiple_of(c * tk, tk), tk), :]
            for i, h in enumerate(heads):
                s = jnp.dot(kc, qt_scr[h], preferred_element_type=F32)
                s_scr[x, i] = s
                cm_scr[x, i] = jnp.max(s, axis=0, keepdims=True)

        def stage_b(x):
            for i in range(GROUP):
                m = m_scr[i]
                m_new = jnp.maximum(m, cm_scr[x, i])
                a_scr[x, i] = jnp.exp2(m - m_new)
                m_scr[i] = m_new
                p_scr[x, i] = jnp.exp2(s_scr[x, i] - m_new).astype(BF16)

        def stage_c(c, x, g=g):
            vt = vt_ref[g * VT_ROWS:(g + 1) * VT_ROWS, pl.ds(pl.multiple_of(c * tk, tk), tk)]
            for i in range(GROUP):
                acc_scr[i] = a_scr[x, i] * acc_scr[i] + jnp.dot(vt, p_scr[x, i], preferred_element_type=F32)

        def step(c, x, last=False):
            stage_c(c - 1, 1 - x)
            stage_b(x)
            if not last:
                stage_a(c + 1, 1 - x)

        for i in range(GROUP):
            m_scr[i] = jnp.full((1, tq), -jnp.inf, F32)
            acc_scr[i] = jnp.zeros((VT_ROWS, tq), F32)
        stage_a(0, 0)
        stage_b(0)
        stage_a(1, 1)

        def pair(j, carry):
            step(2 * j + 1, 1)
            step(2 * j + 2, 0)
            return carry

        lax.fori_loop(0, (nk - 2) // 2, pair, 0)
        step(nk - 1, 1, last=True)
        stage_c(nk - 1, 1)
        for i, h in enumerate(heads):
            acc = acc_scr[i]
            ot_scr[h * HEAD_DIM:(h + 1) * HEAD_DIM, :] = acc[0:HEAD_DIM] / acc[HEAD_DIM:HEAD_DIM + 1]
    o_ref[0] = ot_scr[...].T.astype(o_ref.dtype)


def _glob_attn(qb, kb, vb, tq, tk):
    b, s, _ = qb.shape
    assert s % (2 * tk) == 0, "the chunk pipeline walks key chunks in pairs"
    return pl.pallas_call(
        functools.partial(_glob_attn_kernel, tk=tk),
        grid=(b, s // tq),
        in_specs=[pl.BlockSpec((1, tq, Q_W), lambda bi, qi: (bi, qi, 0)),
                  pl.BlockSpec((1, s, KV_W), lambda bi, qi: (bi, 0, 0)),
                  pl.BlockSpec((1, s, KV_W), lambda bi, qi: (bi, 0, 0))],
        out_specs=pl.BlockSpec((1, tq, Q_W), lambda bi, qi: (bi, qi, 0)),
        out_shape=jax.ShapeDtypeStruct((b, s, Q_W), BF16),
        scratch_shapes=[pltpu.VMEM((KV_HEADS * VT_ROWS, s), BF16), pltpu.VMEM((Q_HEADS, KV_W, tq), BF16),
                        pltpu.VMEM((2, GROUP, tk, tq), F32), pltpu.VMEM((2, GROUP, tk, tq), BF16),
                        pltpu.VMEM((2, GROUP, 1, tq), F32), pltpu.VMEM((GROUP, 1, tq), F32),
                        pltpu.VMEM((2, GROUP, 1, tq), F32), pltpu.VMEM((GROUP, VT_ROWS, tq), F32),
                        pltpu.VMEM((Q_W, tq), F32)],
        compiler_params=_params(("arbitrary", "arbitrary")),
        name="glob_attn",
    )(qb, kb, vb)


def _first_argmax(vals, idx, big):
    m = jnp.max(vals, axis=-1, keepdims=True)
    return m, jnp.min(jnp.where(vals == m, idx, big), axis=-1, keepdims=True)


def _route(logits):
    rows = logits.shape[0]
    gl = logits[:, 0:N_GROUPS]
    gidx = lax.broadcasted_iota(jnp.int32, (rows, N_GROUPS), 1)
    gmax, g_sel = _first_argmax(gl, gidx, N_GROUPS)
    g_w = 1.0 / jnp.sum(jnp.exp(gl - gmax), axis=-1, keepdims=True)
    e_sel = jnp.zeros((rows, EXPERTS_PER_GROUP), F32)
    for g in range(N_GROUPS):
        lo = N_GROUPS + g * EXPERTS_PER_GROUP
        e_sel = jnp.where(g_sel == g, logits[:, lo:lo + EXPERTS_PER_GROUP], e_sel)
    eidx = lax.broadcasted_iota(jnp.int32, (rows, EXPERTS_PER_GROUP), 1)
    ex = jnp.exp(e_sel - jnp.max(e_sel, axis=-1, keepdims=True))
    e_prob = ex / jnp.sum(ex, axis=-1, keepdims=True)
    p1, i1 = _first_argmax(e_prob, eidx, EXPERTS_PER_GROUP)
    rest = jnp.where(eidx == i1, -1.0, e_prob)
    p2, i2 = _first_argmax(rest, eidx, EXPERTS_PER_GROUP)
    tot = p1 + p2
    lane = lax.broadcasted_iota(jnp.int32, (rows, LANES), 1)
    out = jnp.where(lane == g_sel, 1.0, 0.0)
    out = out + jnp.where(lane == N_GROUPS + i1, g_w * (p1 / tot), 0.0)
    return out + jnp.where(lane == N_GROUPS + i2, g_w * (p2 / tot), 0.0)


def _hi_lo(v):
    hi = v.astype(BF16)
    return hi, (v - hi.astype(F32)).astype(BF16)


def _merge_kernel(x_ref, ya_ref, yb_ref, ga_ref, gb_ref, woa_ref, wob_ref, wout_ref, gf_ref, wr_ref, br_ref,
                  x1_ref, route_ref, routet_ref, cnt_ref):
    a = jnp.dot(ya_ref[...], woa_ref[...], preferred_element_type=F32)
    b = jnp.dot(yb_ref[...], wob_ref[...], preferred_element_type=F32)
    merged = ga_ref[...].astype(F32) * a + gb_ref[...].astype(F32) * b
    x1 = x_ref[...] + jnp.dot(merged.astype(BF16), wout_ref[...], preferred_element_type=F32)
    x1_ref[...] = x1
    ms = jnp.mean(x1 * x1, axis=-1, keepdims=True)
    t = x1 * lax.rsqrt(ms + EPS) * gf_ref[...]
    t_hi, t_lo = _hi_lo(t)
    big = jnp.dot(t_hi, wr_ref[...], preferred_element_type=F32)
    small = jnp.dot(t_lo, wr_ref[:, 0:LANES], preferred_element_type=F32)
    logits = big[:, 0:LANES] + big[:, LANES:2 * LANES] + small + br_ref[...]
    route = _route(logits)
    route_ref[...] = route
    routet_ref[...] = route.T
    cnt_ref[0] = jnp.broadcast_to(jnp.sum(route, axis=0, keepdims=True), cnt_ref.shape[1:]).astype(jnp.int32)


def _merge(x2, ya, yb, ga, gb, w_oa, w_ob, w_out, ffn_g, w_rg, b_rg, w_re, b_re, tm):
    t, d = x2.shape
    wr = jnp.zeros((d, LANES), F32).at[:, :N_GROUPS].set(w_rg).at[:, N_GROUPS:N_GROUPS + N_EXPERTS].set(w_re)
    wr_hi, wr_lo = _hi_lo(wr)
    br = jnp.zeros((1, LANES), F32).at[0, :N_GROUPS].set(b_rg).at[0, N_GROUPS:N_GROUPS + N_EXPERTS].set(b_re)
    row = lambda i: (i, 0)
    return pl.pallas_call(
        _merge_kernel,
        grid=(t // tm,),
        in_specs=[pl.BlockSpec((tm, d), row), pl.BlockSpec((tm, Q_W), row), pl.BlockSpec((tm, Q_W), row),
                  pl.BlockSpec((tm, d), row), pl.BlockSpec((tm, d), row),
                  _full((Q_W, d)), _full((Q_W, d)), _full((d, d)), _full((1, d)), _full((d, 2 * LANES)),
                  _full((1, LANES))],
        out_specs=[pl.BlockSpec((tm, d), row), pl.BlockSpec((tm, LANES), row),
                   pl.BlockSpec((LANES, tm), lambda i: (0, i)), pl.BlockSpec((1, 8, LANES), lambda i: (i, 0, 0))],
        out_shape=[jax.ShapeDtypeStruct((t, d), F32), jax.ShapeDtypeStruct((t, LANES), F32),
                   jax.ShapeDtypeStruct((LANES, t), F32), jax.ShapeDtypeStruct((t // tm, 8, LANES), jnp.int32)],
        compiler_params=_params(("parallel",)),
        name="merge",
    )(x2, ya, yb, ga, gb, w_oa.astype(BF16), w_ob.astype(BF16), w_out.astype(BF16), ffn_g.reshape(1, d),
      jnp.concatenate([wr_hi, wr_lo], axis=1), br)


ROW_ALIGN = 128
BIG_BLOCK = 2 * ROW_ALIGN


def _rmsnorm(x, g):
    ms = jnp.mean(x * x, axis=-1, keepdims=True)
    return x * lax.rsqrt(ms + EPS) * g


def _moe_kernel(cnt_ref, x1_ref, route_ref, routet_ref, gf_ref, wg_ref, wu_ref, wd_ref, gfin_ref, o_ref,
                t_scr, p_scr, q_scr, ys_scr, rh_scr, rl_scr):
    i = pl.program_id(0)
    g = pl.program_id(1)
    tm = x1_ref.shape[0]
    rows = p_scr.shape[0]
    offs, off = [], 0
    for gg in range(N_GROUPS):
        offs.append(off)
        off = off + (cnt_ref[i, gg] + (ROW_ALIGN - 1)) // ROW_ALIGN * ROW_ALIGN

    @pl.when(g == 0)
    def _():
        t_scr[...] = _rmsnorm(x1_ref[...], gf_ref[...]).astype(BF16)
        ys_scr[...] = jnp.zeros_like(ys_scr)
        route = route_ref[...]
        rh, rl = _hi_lo(route)
        rh_scr[...] = rh
        rl_scr[...] = rl
        lane = lax.broadcasted_iota(jnp.int32, (tm, LANES), 1)
        onehot = jnp.where(lane < N_GROUPS, route, 0.0)
        r_i = lax.broadcasted_iota(jnp.int32, (tm, tm), 0)
        c_i = lax.broadcasted_iota(jnp.int32, (tm, tm), 1)
        before = jnp.dot(jnp.where(c_i < r_i, 1.0, 0.0).astype(BF16), onehot.astype(BF16),
                         preferred_element_type=F32)
        goff = jnp.zeros((tm, LANES), F32)
        for gg in range(N_GROUPS):
            goff = jnp.where(lane == gg, jnp.asarray(offs[gg], jnp.int32).astype(F32), goff)
        dest_col = jnp.sum((before + goff) * onehot, axis=1, keepdims=True).astype(jnp.int32)
        q_scr[...] = jnp.where(dest_col == lax.broadcasted_iota(jnp.int32, (tm, rows), 1), 1.0, 0.0).astype(BF16)
        sub = lax.broadcasted_iota(jnp.int32, (16, tm), 0)
        onehot_t = jnp.where(sub < N_GROUPS, routet_ref[0:16, :], 0.0)
        before_t = jnp.dot(onehot_t.astype(BF16), jnp.where(r_i < c_i, 1.0, 0.0).astype(BF16),
                           preferred_element_type=F32)
        goff_t = jnp.zeros((16, tm), F32)
        for gg in range(N_GROUPS):
            goff_t = jnp.where(sub == gg, jnp.asarray(offs[gg], jnp.int32).astype(F32), goff_t)
        dest_row = jnp.sum((before_t + goff_t) * onehot_t, axis=0, keepdims=True).astype(jnp.int32)
        for r0 in range(0, rows, BIG_BLOCK):
            rid = lax.broadcasted_iota(jnp.int32, (BIG_BLOCK, tm), 0) + r0
            p_scr[r0:r0 + BIG_BLOCK, :] = jnp.where(dest_row == rid, 1.0, 0.0).astype(BF16)

    def ffn(r, m):
        pb = p_scr[pl.ds(pl.multiple_of(r, ROW_ALIGN), m), :]
        tb = jnp.dot(pb, t_scr[...], preferred_element_type=F32).astype(BF16)
        wb = (jnp.dot(pb, rh_scr[...], preferred_element_type=F32)
              + jnp.dot(pb, rl_scr[...], preferred_element_type=F32))
        y = jnp.zeros((m, o_ref.shape[1]), F32)
        for e in range(EXPERTS_PER_GROUP):
            a = jax.nn.silu(jnp.dot(tb, wg_ref[0, e], preferred_element_type=F32)) * jnp.dot(
                tb, wu_ref[0, e], preferred_element_type=F32)
            y = y + wb[:, N_GROUPS + e:N_GROUPS + e + 1] * jnp.dot(a.astype(BF16), wd_ref[0, e],
                                                                   preferred_element_type=F32)
        ys_scr[pl.ds(pl.multiple_of(r, ROW_ALIGN), m), :] = y.astype(BF16)

    start = offs[0]
    for gg in range(1, N_GROUPS):
        start = jnp.where(g == gg, offs[gg], start)
    n_small = (cnt_ref[i, g] + (ROW_ALIGN - 1)) // ROW_ALIGN
    n_big = n_small // 2

    def big(j, carry):
        ffn(start + j * BIG_BLOCK, BIG_BLOCK)
        return carry

    lax.fori_loop(0, n_big, big, 0)

    @pl.when(n_small % 2 == 1)
    def _():
        ffn(start + n_big * BIG_BLOCK, ROW_ALIGN)

    @pl.when(g == N_GROUPS - 1)
    def _():
        x2 = x1_ref[...] + jnp.dot(q_scr[...], ys_scr[...], preferred_element_type=F32)
        o_ref[...] = _rmsnorm(x2, gfin_ref[...])


def _moe(x1, route, routet, cnt, ffn_g, w_eg, w_eu, w_ed, final_g, tm):
    t, d = x1.shape
    de = w_eg.shape[-1]
    rows = tm + N_GROUPS * ROW_ALIGN
    grouped = lambda w: w.astype(BF16).reshape((N_GROUPS, EXPERTS_PER_GROUP) + w.shape[1:])
    row = lambda i, g, c: (i, 0)
    const = lambda i, g, c: (0, 0)
    wspec = lambda a, b: pl.BlockSpec((1, EXPERTS_PER_GROUP, a, b), lambda i, g, c: (g, 0, 0, 0))
    once = pl.Buffered(1)
    return pl.pallas_call(
        _moe_kernel,
        grid_spec=pltpu.PrefetchScalarGridSpec(
            num_scalar_prefetch=1,
            grid=(t // tm, N_GROUPS),
            in_specs=[pl.BlockSpec((tm, d), row, pipeline_mode=once), pl.BlockSpec((tm, LANES), row),
                      pl.BlockSpec((LANES, tm), lambda i, g, c: (0, i)), pl.BlockSpec((1, d), const),
                      wspec(d, de), wspec(d, de), wspec(de, d), pl.BlockSpec((1, d), const)],
            out_specs=pl.BlockSpec((tm, d), row, pipeline_mode=once),
            scratch_shapes=[pltpu.VMEM((tm, d), BF16), pltpu.VMEM((rows, tm), BF16), pltpu.VMEM((tm, rows), BF16),
                            pltpu.VMEM((rows, d), BF16), pltpu.VMEM((tm, LANES), BF16),
                            pltpu.VMEM((tm, LANES), BF16)]),
        out_shape=jax.ShapeDtypeStruct((t, d), F32),
        compiler_params=_params(("arbitrary", "arbitrary")),
        name="moe",
    )(cnt, x1, route, routet, ffn_g.reshape(1, d), grouped(w_eg), grouped(w_eu), grouped(w_ed),
      final_g.reshape(1, d))


def _tile(n, want):
    while n % want:
        want //= 2
    return want


def _trunk(x, attn_g, w_in, sink, gq, gk, w_oa, w_ob, w_out, ffn_g, w_rg, b_rg, w_re, b_re, w_eg, w_eu, w_ed,
           final_g):
    b, s, d = x.shape
    x2 = x.reshape(b * s, d)
    qa, ka, va, qb, kb, vb, ga, gb = _in_proj(x2, s, attn_g, w_in, gq, gk, _tile(s, 512))
    ya = _win_attn(qa, ka, va, sink, s, _tile(s, 512))
    yb = _glob_attn(qb.reshape(b, s, Q_W), kb.reshape(b, s, KV_W), vb.reshape(b, s, KV_W),
                    256, 256).reshape(b * s, Q_W)
    tm_merge, tm_moe = _tile(b * s, 512), _tile(b * s, 1024)
    x1, route, routet, cnt = _merge(x2, ya, yb, ga, gb, w_oa, w_ob, w_out, ffn_g, w_rg, b_rg, w_re, b_re, tm_merge)
    cnt = cnt[:, 0, :N_GROUPS].reshape(b * s // tm_moe, tm_moe // tm_merge, N_GROUPS).sum(axis=1)
    y = _moe(x1, route, routet, cnt, ffn_g, w_eg, w_eu, w_ed, final_g, tm_moe)
    return y.reshape(b, s, d)


def kernel(x_prompt, x_sample, attn_norm_g, w_in, a_sink, b_q_norm_g, b_k_norm_g, w_oa, w_ob, w_out, ffn_norm_g,
           w_router_group, b_router_group, w_router_expert, b_router_expert, w_expert_gate, w_expert_up,
           w_expert_down, final_norm_g):
    assert attn_norm_g.shape[0] == 1, "single-layer trunk"
    weights = (attn_norm_g[0], w_in[0], a_sink[0], b_q_norm_g[0], b_k_norm_g[0], w_oa[0], w_ob[0], w_out[0],
               ffn_norm_g[0], w_router_group[0], b_router_group[0], w_router_expert[0], b_router_expert[0],
               w_expert_gate[0], w_expert_up[0], w_expert_down[0], final_norm_g)
    return (_trunk(x_prompt, *weights), _trunk(x_sample, *weights))
```

```python
import functools

import jax
import jax.numpy as jnp
import numpy as np
from jax import lax
from jax.experimental import pallas as pl
from jax.experimental.pallas import tpu as pltpu

HEAD_DIM = 64
Q_HEADS = 8
KV_HEADS = 2
GROUP = Q_HEADS // KV_HEADS
Q_W = Q_HEADS * HEAD_DIM
KV_W = KV_HEADS * HEAD_DIM
WINDOW = 128
BLOCK = 128
GRID_W = 64
ROPE_THETA = 10000.0
N_GROUPS = 4
EXPERTS_PER_GROUP = 4
N_EXPERTS = N_GROUPS * EXPERTS_PER_GROUP
EPS = 1e-6
NEG_INF = -1e30
SCALE = HEAD_DIM ** -0.5
LOG2E = 1.4426950408889634
ONES_ROWS = 16
VT_ROWS = HEAD_DIM + ONES_ROWS
LANES = 128

VMEM_LIMIT = 56 * 1024 * 1024

F32 = jnp.float32
BF16 = jnp.bfloat16


def _params(sem):
    return pltpu.CompilerParams(dimension_semantics=sem, vmem_limit_bytes=VMEM_LIMIT)


def _full(shape):
    return pl.BlockSpec(shape, lambda *_: (0,) * len(shape))


def _hi_lo(v):
    top = lax.bitcast_convert_type(lax.bitcast_convert_type(v, jnp.uint32) & jnp.uint32(0xFFFF0000), F32)
    return top.astype(BF16), (v - top).astype(BF16)


def _split_dot(y, ones_bd):
    hi, lo = _hi_lo(y)
    return jnp.dot(hi, ones_bd, preferred_element_type=F32) + jnp.dot(lo, ones_bd, preferred_element_type=F32)


def _pair_swap(y):
    n = y.shape[-1]
    lane = lax.broadcasted_iota(jnp.int32, y.shape, y.ndim - 1)
    nxt = pltpu.roll(y, n - 1, y.ndim - 1)
    prv = pltpu.roll(y, 1, y.ndim - 1)
    return jnp.where((lane & 1) == 0, nxt, prv)


def _norm_rope(y, gain, ones_bd, cos, sin):
    ms = _split_dot(y * y, ones_bd) * (1.0 / HEAD_DIM)
    yn = y * lax.rsqrt(ms + EPS) * gain
    return yn * cos + _pair_swap(yn) * sin


def _in_proj_kernel(x_ref, g_ref, w_ref, cos_ref, sin_ref, gq_ref, gk_ref, oq_ref, ok_ref,
                    qa_ref, ka_ref, va_ref, qb_ref, kb_ref, vb_ref, ga_ref, gb_ref):
    x = x_ref[...]
    ms = jnp.mean(x * x, axis=-1, keepdims=True)
    h = (x * lax.rsqrt(ms + EPS) * g_ref[...]).astype(BF16)

    def proj(lo, width):
        return jnp.dot(h, w_ref[:, lo:lo + width], preferred_element_type=F32)

    o = 0
    qa_ref[...] = (proj(o, Q_W) * (SCALE * LOG2E)).astype(BF16); o += Q_W
    ka_ref[...] = proj(o, KV_W).astype(BF16); o += KV_W
    va_ref[...] = proj(o, KV_W).astype(BF16); o += KV_W
    cos = cos_ref[...]
    sin = sin_ref[...]
    qb = proj(o, Q_W); o += Q_W
    cos_q = jnp.concatenate([cos] * (Q_W // LANES), axis=1)
    sin_q = jnp.concatenate([sin] * (Q_W // LANES), axis=1)
    qb_ref[...] = (_norm_rope(qb, gq_ref[...], oq_ref[...], cos_q, sin_q) * (SCALE * LOG2E)).astype(BF16)
    kb = proj(o, KV_W); o += KV_W
    kb_ref[...] = _norm_rope(kb, gk_ref[...], ok_ref[...], cos, sin).astype(BF16)
    vb_ref[...] = proj(o, KV_W).astype(BF16); o += KV_W
    d = x.shape[-1]
    ga_ref[...] = jax.nn.sigmoid(proj(o, d)).astype(BF16); o += d
    gb_ref[...] = jax.nn.sigmoid(proj(o, d)).astype(BF16)


def _rope_tables(seq_len):
    rows = seq_len // GRID_W
    row = jnp.repeat(jnp.arange(rows, dtype=F32), GRID_W)
    col = jnp.tile(jnp.arange(GRID_W, dtype=F32), rows)
    half = HEAD_DIM // 2
    inv = ROPE_THETA ** (-jnp.arange(0, half, 2, dtype=F32) / half)
    ang = jnp.concatenate([row[:, None] * inv, col[:, None] * inv], axis=-1)
    cos = jnp.repeat(jnp.cos(ang), 2, axis=-1)
    sin = jnp.repeat(jnp.sin(ang), 2, axis=-1) * jnp.tile(jnp.array([-1.0, 1.0], F32), half)
    reps = LANES // HEAD_DIM
    return jnp.tile(cos, (1, reps)), jnp.tile(sin, (1, reps))


def _block_ones(width):
    idx = np.arange(width) // HEAD_DIM
    return jnp.asarray(idx[:, None] == idx[None, :], dtype=BF16)


def _in_proj(x2, seq_len, attn_g, w_in, gq, gk, tm):
    t, d = x2.shape
    in_w = w_in.shape[1]
    cos, sin = _rope_tables(seq_len)
    nseq = seq_len // tm
    row = lambda i: (i, 0)
    pos = lambda i: (i % nseq, 0)
    outs = [(Q_W, BF16), (KV_W, BF16), (KV_W, BF16), (Q_W, BF16), (KV_W, BF16), (KV_W, BF16), (d, BF16), (d, BF16)]
    return pl.pallas_call(
        _in_proj_kernel,
        grid=(t // tm,),
        in_specs=[pl.BlockSpec((tm, d), row), _full((1, d)), _full((d, in_w)),
                  pl.BlockSpec((tm, LANES), pos), pl.BlockSpec((tm, LANES), pos),
                  _full((1, Q_W)), _full((1, KV_W)), _full((Q_W, Q_W)), _full((KV_W, KV_W))],
        out_specs=[pl.BlockSpec((tm, w), row) for w, _ in outs],
        out_shape=[jax.ShapeDtypeStruct((t, w), dt) for w, dt in outs],
        compiler_params=_params(("parallel",)),
        name="in_proj",
    )(x2, attn_g.reshape(1, d), w_in.astype(BF16), cos, sin,
      jnp.tile(gq.astype(F32), Q_HEADS).reshape(1, Q_W), jnp.tile(gk.astype(F32), KV_HEADS).reshape(1, KV_W),
      _block_ones(Q_W), _block_ones(KV_W))


PAIRS = Q_HEADS // 2
N_EDGE = 4


def _win_bias():
    slopes = np.exp2(-8.0 * np.arange(1, Q_HEADS + 1, dtype=np.float32) / Q_HEADS).astype(np.float32)
    key = np.arange(3 * BLOCK)[:, None]
    dist = np.abs(np.arange(BLOCK)[None, :] - (key - BLOCK))
    tabs = []
    for edge in range(N_EDGE):
        alive = dist <= WINDOW
        if edge & 1:
            alive = alive & (key >= BLOCK)
        if edge & 2:
            alive = alive & (key < 2 * BLOCK)
        heads = [np.where(alive, -slopes[h] * np.float32(LOG2E) * dist.astype(np.float32), np.float32(NEG_INF))
                 for h in range(Q_HEADS)]
        tabs.append(np.stack([np.concatenate([heads[2 * p], heads[2 * p + 1]], axis=1) for p in range(PAIRS)]))
    return jnp.asarray(np.stack(tabs), dtype=F32)


def _win_attn_kernel(sink_ref, q_ref, kc_ref, kp_ref, kn_ref, vc_ref, vp_ref, vn_ref, bias_ref, o_ref,
                     kh_scr, vt_scr, qp_scr, s_scr, p_scr, ot_scr, *, blocks_per_seq, sub):
    i = pl.program_id(0)
    tq = q_ref.shape[0]
    ones = jnp.ones((ONES_ROWS, BLOCK), BF16)
    for off, n, kr, vr in ((0, BLOCK, kp_ref, vp_ref), (BLOCK, tq, kc_ref, vc_ref), (BLOCK + tq, BLOCK, kn_ref, vn_ref)):
        k = kr[...]
        vt = vr[...].astype(F32).T.astype(BF16)
        for g in range(KV_HEADS):
            kh_scr[g, off:off + n, :] = k[:, g * HEAD_DIM:(g + 1) * HEAD_DIM]
            vt_scr[g * VT_ROWS:g * VT_ROWS + HEAD_DIM, off:off + n] = vt[g * HEAD_DIM:(g + 1) * HEAD_DIM, :]
            for o in range(off, off + n, BLOCK):
                vt_scr[g * VT_ROWS + HEAD_DIM:(g + 1) * VT_ROWS, o:o + BLOCK] = ones
    qt = q_ref[...].astype(F32).T.astype(BF16)
    for j in range(sub):
        for p in range(PAIRS):
            for u in range(2):
                h = 2 * p + u
                qp_scr[j, p, :, u * BLOCK:(u + 1) * BLOCK] = qt[h * HEAD_DIM:(h + 1) * HEAD_DIM,
                                                                j * BLOCK:(j + 1) * BLOCK]
    lane = lax.broadcasted_iota(jnp.int32, (1, 2 * BLOCK), 1)
    for j in range(sub):
        x = j % 2
        blk = (i * sub + j) % blocks_per_seq
        edge = (blk == 0).astype(jnp.int32) + 2 * (blk == blocks_per_seq - 1).astype(jnp.int32)
        for p in range(PAIRS):
            g = (2 * p) // GROUP
            s = jnp.dot(kh_scr[g, j * BLOCK:(j + 3) * BLOCK, :], qp_scr[j, p], preferred_element_type=F32)
            s_scr[x, p] = s + bias_ref[edge, p]
        sink_terms = []
        for p in range(PAIRS):
            logits = s_scr[x, p]
            sink = jnp.where(lane < BLOCK, sink_ref[2 * p], sink_ref[2 * p + 1]) * LOG2E
            m = jnp.maximum(jnp.max(logits, axis=0, keepdims=True), sink)
            p_scr[x, p] = jnp.exp2(logits - m).astype(BF16)
            sink_terms.append(jnp.exp2(sink - m))
        for p in range(PAIRS):
            g = (2 * p) // GROUP
            ot = jnp.dot(vt_scr[g * VT_ROWS:(g + 1) * VT_ROWS, j * BLOCK:(j + 3) * BLOCK], p_scr[x, p],
                         preferred_element_type=F32)
            out = ot[0:HEAD_DIM] / (ot[HEAD_DIM:HEAD_DIM + 1] + sink_terms[p])
            for u in range(2):
                h = 2 * p + u
                ot_scr[h * HEAD_DIM:(h + 1) * HEAD_DIM, j * BLOCK:(j + 1) * BLOCK] = out[:, u * BLOCK:(u + 1) * BLOCK]
    o_ref[...] = ot_scr[...].T.astype(o_ref.dtype)


def _win_attn(qa, ka, va, sink, seq_len, tq):
    t = qa.shape[0]
    sub = tq // BLOCK
    nblk = t // BLOCK
    win = tq + 2 * BLOCK
    row = lambda i: (i, 0)
    prev = lambda i: (jnp.maximum(i * sub - 1, 0), 0)
    nxt = lambda i: (jnp.minimum((i + 1) * sub, nblk - 1), 0)
    kv_specs = [pl.BlockSpec((tq, KV_W), row), pl.BlockSpec((BLOCK, KV_W), prev), pl.BlockSpec((BLOCK, KV_W), nxt)]
    bias_spec = pl.BlockSpec((N_EDGE, PAIRS, 3 * BLOCK, 2 * BLOCK), lambda i: (0, 0, 0, 0),
                             pipeline_mode=pl.Buffered(1))
    return pl.pallas_call(
        functools.partial(_win_attn_kernel, blocks_per_seq=seq_len // BLOCK, sub=sub),
        grid=(t // tq,),
        in_specs=[pl.BlockSpec(memory_space=pltpu.SMEM), pl.BlockSpec((tq, Q_W), row)] + kv_specs + kv_specs
                 + [bias_spec],
        out_specs=pl.BlockSpec((tq, Q_W), row),
        out_shape=jax.ShapeDtypeStruct((t, Q_W), BF16),
        scratch_shapes=[pltpu.VMEM((KV_HEADS, win, HEAD_DIM), BF16), pltpu.VMEM((KV_HEADS * VT_ROWS, win), BF16),
                        pltpu.VMEM((sub, PAIRS, HEAD_DIM, 2 * BLOCK), BF16),
                        pltpu.VMEM((2, PAIRS, 3 * BLOCK, 2 * BLOCK), F32),
                        pltpu.VMEM((2, PAIRS, 3 * BLOCK, 2 * BLOCK), BF16), pltpu.VMEM((Q_W, tq), F32)],
        compiler_params=_params(("parallel",)),
        name="win_attn",
    )(sink.astype(F32), qa, ka, ka, ka, va, va, va, _win_bias())


def _glob_attn_kernel(q_ref, k_ref, v_ref, o_ref, kh_ref, vt_ref, qt_scr, s_scr, p_scr, cm_scr, m_scr, a_scr, acc_scr,
                      ot_scr, *, tk):
    tq = q_ref.shape[1]
    nk = k_ref.shape[1] // tk

    @pl.when(pl.program_id(1) == 0)
    def _():
        ones = jnp.ones((ONES_ROWS, tk), BF16)

        def fill(c, carry):
            off = pl.multiple_of(c * tk, tk)
            vt = v_ref[0, pl.ds(off, tk), :].astype(F32).T.astype(BF16)
            kc = k_ref[0, pl.ds(off, tk), :]
            for g in range(KV_HEADS):
                kh_ref[g, pl.ds(off, tk), :] = kc[:, g * HEAD_DIM:(g + 1) * HEAD_DIM]
                vt_ref[g * VT_ROWS:g * VT_ROWS + HEAD_DIM, pl.ds(off, tk)] = vt[g * HEAD_DIM:(g + 1) * HEAD_DIM, :]
                vt_ref[g * VT_ROWS + HEAD_DIM:(g + 1) * VT_ROWS, pl.ds(off, tk)] = ones
            return carry

        lax.fori_loop(0, nk, fill, 0)

    qt_scr[...] = q_ref[0].astype(F32).T.astype(BF16)

    for g in range(KV_HEADS):
        heads = range(g * GROUP, (g + 1) * GROUP)

        def stage_a(c, x, heads=heads, g=g):
            kc = kh_ref[g, pl.ds(pl.multiple_of(c * tk, tk), tk), :]
            for i, h in enumerate(heads):
                s = jnp.dot(kc, qt_scr[h * HEAD_DIM:(h + 1) * HEAD_DIM, :], preferred_element_type=F32)
                s_scr[x, i] = s
                cm_scr[x, i] = jnp.max(s, axis=0, keepdims=True)

        def stage_b(x):
            for i in range(GROUP):
                m = m_scr[i]
                m_new = jnp.maximum(m, cm_scr[x, i])
                a_scr[x, i] = jnp.exp2(m - m_new)
                m_scr[i] = m_new
                p_scr[x, i] = jnp.exp2(s_scr[x, i] - m_new).astype(BF16)

        def stage_c(c, x, g=g):
            vt = vt_ref[g * VT_ROWS:(g + 1) * VT_ROWS, pl.ds(pl.multiple_of(c * tk, tk), tk)]
            for i in range(GROUP):
                acc_scr[i] = a_scr[x, i] * acc_scr[i] + jnp.dot(vt, p_scr[x, i], preferred_element_type=F32)

        def step(c, x, last=False):
            stage_c(c - 1, 1 - x)
            stage_b(x)
            if not last:
                stage_a(c + 1, 1 - x)

        for i in range(GROUP):
            m_scr[i] = jnp.full((1, tq), -jnp.inf, F32)
            acc_scr[i] = jnp.zeros((VT_ROWS, tq), F32)
        stage_a(0, 0)
        stage_b(0)
        stage_a(1, 1)

        def pair(j, carry):
            step(2 * j + 1, 1)
            step(2 * j + 2, 0)
            return carry

        lax.fori_loop(0, (nk - 2) // 2, pair, 0)
        step(nk - 1, 1, last=True)
        stage_c(nk - 1, 1)
        for i, h in enumerate(heads):
            acc = acc_scr[i]
            ot_scr[h * HEAD_DIM:(h + 1) * HEAD_DIM, :] = acc[0:HEAD_DIM] / acc[HEAD_DIM:HEAD_DIM + 1]
    o_ref[0] = ot_scr[...].T.astype(o_ref.dtype)


def _glob_attn(qb, kb, vb, tq, tk):
    b, s, _ = qb.shape
    assert s % (2 * tk) == 0, "the chunk pipeline walks key chunks in pairs"
    return pl.pallas_call(
        functools.partial(_glob_attn_kernel, tk=tk),
        grid=(b, s // tq),
        in_specs=[pl.BlockSpec((1, tq, Q_W), lambda bi, qi: (bi, qi, 0)),
                  pl.BlockSpec((1, s, KV_W), lambda bi, qi: (bi, 0, 0)),
                  pl.BlockSpec((1, s, KV_W), lambda bi, qi: (bi, 0, 0))],
        out_specs=pl.BlockSpec((1, tq, Q_W), lambda bi, qi: (bi, qi, 0)),
        out_shape=jax.ShapeDtypeStruct((b, s, Q_W), BF16),
        scratch_shapes=[pltpu.VMEM((KV_HEADS, s, HEAD_DIM), BF16), pltpu.VMEM((KV_HEADS * VT_ROWS, s), BF16),
                        pltpu.VMEM((Q_W, tq), BF16),
                        pltpu.VMEM((2, GROUP, tk, tq), F32), pltpu.VMEM((2, GROUP, tk, tq), BF16),
                        pltpu.VMEM((2, GROUP, 1, tq), F32), pltpu.VMEM((GROUP, 1, tq), F32),
                        pltpu.VMEM((2, GROUP, 1, tq), F32), pltpu.VMEM((GROUP, VT_ROWS, tq), F32),
                        pltpu.VMEM((Q_W, tq), F32)],
        compiler_params=_params(("arbitrary", "arbitrary")),
        name="glob_attn",
    )(qb, kb, vb)


def _first_argmax(vals, idx, big):
    m = jnp.max(vals, axis=-1, keepdims=True)
    return m, jnp.min(jnp.where(vals == m, idx, big), axis=-1, keepdims=True)


def _route(logits):
    rows = logits.shape[0]
    gl = logits[:, 0:N_GROUPS]
    gidx = lax.broadcasted_iota(jnp.int32, (rows, N_GROUPS), 1)
    gmax, g_sel = _first_argmax(gl, gidx, N_GROUPS)
    g_w = 1.0 / jnp.sum(jnp.exp(gl - gmax), axis=-1, keepdims=True)
    e_sel = jnp.zeros((rows, EXPERTS_PER_GROUP), F32)
    for g in range(N_GROUPS):
        lo = N_GROUPS + g * EXPERTS_PER_GROUP
        e_sel = jnp.where(g_sel == g, logits[:, lo:lo + EXPERTS_PER_GROUP], e_sel)
    eidx = lax.broadcasted_iota(jnp.int32, (rows, EXPERTS_PER_GROUP), 1)
    ex = jnp.exp(e_sel - jnp.max(e_sel, axis=-1, keepdims=True))
    e_prob = ex / jnp.sum(ex, axis=-1, keepdims=True)
    p1, i1 = _first_argmax(e_prob, eidx, EXPERTS_PER_GROUP)
    rest = jnp.where(eidx == i1, -1.0, e_prob)
    p2, i2 = _first_argmax(rest, eidx, EXPERTS_PER_GROUP)
    tot = p1 + p2
    lane = lax.broadcasted_iota(jnp.int32, (rows, LANES), 1)
    out = jnp.where(lane == g_sel, 1.0, 0.0)
    out = out + jnp.where(lane == N_GROUPS + i1, g_w * (p1 / tot), 0.0)
    return out + jnp.where(lane == N_GROUPS + i2, g_w * (p2 / tot), 0.0)


def _merge_kernel(x_ref, ya_ref, yb_ref, ga_ref, gb_ref, woa_ref, wob_ref, wout_ref, gf_ref, wr_ref, br_ref,
                  x1_ref, route_ref, routet_ref, cnt_ref):
    a = jnp.dot(ya_ref[...], woa_ref[...], preferred_element_type=F32)
    b = jnp.dot(yb_ref[...], wob_ref[...], preferred_element_type=F32)
    merged = ga_ref[...].astype(F32) * a + gb_ref[...].astype(F32) * b
    x1 = x_ref[...] + jnp.dot(merged.astype(BF16), wout_ref[...], preferred_element_type=F32)
    x1_ref[...] = x1
    ms = jnp.mean(x1 * x1, axis=-1, keepdims=True)
    t = x1 * lax.rsqrt(ms + EPS) * gf_ref[...]
    t_hi, t_lo = _hi_lo(t)
    big = jnp.dot(t_hi, wr_ref[...], preferred_element_type=F32)
    small = jnp.dot(t_lo, wr_ref[:, 0:LANES], preferred_element_type=F32)
    logits = big[:, 0:LANES] + big[:, LANES:2 * LANES] + small + br_ref[...]
    route = _route(logits)
    route_ref[...] = route
    routet_ref[...] = route.T
    cnt_ref[0] = jnp.broadcast_to(jnp.sum(route, axis=0, keepdims=True), cnt_ref.shape[1:]).astype(jnp.int32)


def _merge(x2, ya, yb, ga, gb, w_oa, w_ob, w_out, ffn_g, w_rg, b_rg, w_re, b_re, tm):
    t, d = x2.shape
    wr = jnp.zeros((d, LANES), F32).at[:, :N_GROUPS].set(w_rg).at[:, N_GROUPS:N_GROUPS + N_EXPERTS].set(w_re)
    wr_hi, wr_lo = _hi_lo(wr)
    br = jnp.zeros((1, LANES), F32).at[0, :N_GROUPS].set(b_rg).at[0, N_GROUPS:N_GROUPS + N_EXPERTS].set(b_re)
    row = lambda i: (i, 0)
    return pl.pallas_call(
        _merge_kernel,
        grid=(t // tm,),
        in_specs=[pl.BlockSpec((tm, d), row), pl.BlockSpec((tm, Q_W), row), pl.BlockSpec((tm, Q_W), row),
                  pl.BlockSpec((tm, d), row), pl.BlockSpec((tm, d), row),
                  _full((Q_W, d)), _full((Q_W, d)), _full((d, d)), _full((1, d)), _full((d, 2 * LANES)),
                  _full((1, LANES))],
        out_specs=[pl.BlockSpec((tm, d), row), pl.BlockSpec((tm, LANES), row),
                   pl.BlockSpec((LANES, tm), lambda i: (0, i)), pl.BlockSpec((1, 8, LANES), lambda i: (i, 0, 0))],
        out_shape=[jax.ShapeDtypeStruct((t, d), F32), jax.ShapeDtypeStruct((t, LANES), F32),
                   jax.ShapeDtypeStruct((LANES, t), F32), jax.ShapeDtypeStruct((t // tm, 8, LANES), jnp.int32)],
        compiler_params=_params(("parallel",)),
        name="merge",
    )(x2, ya, yb, ga, gb, w_oa.astype(BF16), w_ob.astype(BF16), w_out.astype(BF16), ffn_g.reshape(1, d),
      jnp.concatenate([wr_hi, wr_lo], axis=1), br)


ROW_ALIGN = 128
BIG_BLOCK = 2 * ROW_ALIGN


def _rmsnorm(x, g):
    ms = jnp.mean(x * x, axis=-1, keepdims=True)
    return x * lax.rsqrt(ms + EPS) * g


def _moe_kernel(cnt_ref, x1_ref, route_ref, routet_ref, gf_ref, wg_ref, wu_ref, wd_ref, gfin_ref, o_ref,
                t_scr, p_scr, q_scr, ys_scr, rh_scr, rl_scr):
    i = pl.program_id(0)
    g = pl.program_id(1)
    tm = x1_ref.shape[0]
    rows = p_scr.shape[0]
    offs, off = [], 0
    for gg in range(N_GROUPS):
        offs.append(off)
        off = off + (cnt_ref[i, gg] + (ROW_ALIGN - 1)) // ROW_ALIGN * ROW_ALIGN

    @pl.when(g == 0)
    def _():
        t_scr[...] = _rmsnorm(x1_ref[...], gf_ref[...]).astype(BF16)
        ys_scr[...] = jnp.zeros_like(ys_scr)
        route = route_ref[...]
        rh, rl = _hi_lo(route)
        rh_scr[...] = rh
        rl_scr[...] = rl
        lane = lax.broadcasted_iota(jnp.int32, (tm, LANES), 1)
        onehot = jnp.where(lane < N_GROUPS, route, 0.0)
        r_i = lax.broadcasted_iota(jnp.int32, (tm, tm), 0)
        c_i = lax.broadcasted_iota(jnp.int32, (tm, tm), 1)
        before = jnp.dot(jnp.where(c_i < r_i, 1.0, 0.0).astype(BF16), onehot.astype(BF16),
                         preferred_element_type=F32)
        goff = jnp.zeros((tm, LANES), F32)
        for gg in range(N_GROUPS):
            goff = jnp.where(lane == gg, jnp.asarray(offs[gg], jnp.int32).astype(F32), goff)
        dest_col = jnp.sum((before + goff) * onehot, axis=1, keepdims=True).astype(jnp.int32)
        q_scr[...] = jnp.where(dest_col == lax.broadcasted_iota(jnp.int32, (tm, rows), 1), 1.0, 0.0).astype(BF16)
        sub = lax.broadcasted_iota(jnp.int32, (16, tm), 0)
        onehot_t = jnp.where(sub < N_GROUPS, routet_ref[0:16, :], 0.0)
        before_t = jnp.dot(onehot_t.astype(BF16), jnp.where(r_i < c_i, 1.0, 0.0).astype(BF16),
                           preferred_element_type=F32)
        goff_t = jnp.zeros((16, tm), F32)
        for gg in range(N_GROUPS):
            goff_t = jnp.where(sub == gg, jnp.asarray(offs[gg], jnp.int32).astype(F32), goff_t)
        dest_row = jnp.sum((before_t + goff_t) * onehot_t, axis=0, keepdims=True).astype(jnp.int32)
        for r0 in range(0, rows, BIG_BLOCK):
            rid = lax.broadcasted_iota(jnp.int32, (BIG_BLOCK, tm), 0) + r0
            p_scr[r0:r0 + BIG_BLOCK, :] = jnp.where(dest_row == rid, 1.0, 0.0).astype(BF16)

    def ffn(r, m):
        pb = p_scr[pl.ds(pl.multiple_of(r, ROW_ALIGN), m), :]
        tb = jnp.dot(pb, t_scr[...], preferred_element_type=F32).astype(BF16)
        wb = (jnp.dot(pb, rh_scr[...], preferred_element_type=F32)
              + jnp.dot(pb, rl_scr[...], preferred_element_type=F32))
        y = jnp.zeros((m, o_ref.shape[1]), F32)
        for e in range(EXPERTS_PER_GROUP):
            a = jax.nn.silu(jnp.dot(tb, wg_ref[0, e], preferred_element_type=F32)) * jnp.dot(
                tb, wu_ref[0, e], preferred_element_type=F32)
            y = y + wb[:, N_GROUPS + e:N_GROUPS + e + 1] * jnp.dot(a.astype(BF16), wd_ref[0, e],
                                                                   preferred_element_type=F32)
        ys_scr[pl.ds(pl.multiple_of(r, ROW_ALIGN), m), :] = y.astype(BF16)

    start = offs[0]
    for gg in range(1, N_GROUPS):
        start = jnp.where(g == gg, offs[gg], start)
    n_small = (cnt_ref[i, g] + (ROW_ALIGN - 1)) // ROW_ALIGN
    n_big = n_small // 2

    def big(j, carry):
        ffn(start + j * BIG_BLOCK, BIG_BLOCK)
        return carry

    lax.fori_loop(0, n_big, big, 0)

    @pl.when(n_small % 2 == 1)
    def _():
        ffn(start + n_big * BIG_BLOCK, ROW_ALIGN)

    @pl.when(g == N_GROUPS - 1)
    def _():
        x2 = x1_ref[...] + jnp.dot(q_scr[...], ys_scr[...], preferred_element_type=F32)
        o_ref[...] = _rmsnorm(x2, gfin_ref[...])


def _moe(x1, route, routet, cnt, ffn_g, w_eg, w_eu, w_ed, final_g, tm):
    t, d = x1.shape
    de = w_eg.shape[-1]
    rows = tm + N_GROUPS * ROW_ALIGN
    grouped = lambda w: w.astype(BF16).reshape((N_GROUPS, EXPERTS_PER_GROUP) + w.shape[1:])
    row = lambda i, g, c: (i, 0)
    const = lambda i, g, c: (0, 0)
    wspec = lambda a, b: pl.BlockSpec((1, EXPERTS_PER_GROUP, a, b), lambda i, g, c: (g, 0, 0, 0))
    once = pl.Buffered(1)
    return pl.pallas_call(
        _moe_kernel,
        grid_spec=pltpu.PrefetchScalarGridSpec(
            num_scalar_prefetch=1,
            grid=(t // tm, N_GROUPS),
            in_specs=[pl.BlockSpec((tm, d), row, pipeline_mode=once), pl.BlockSpec((tm, LANES), row),
                      pl.BlockSpec((LANES, tm), lambda i, g, c: (0, i)), pl.BlockSpec((1, d), const),
                      wspec(d, de), wspec(d, de), wspec(de, d), pl.BlockSpec((1, d), const)],
            out_specs=pl.BlockSpec((tm, d), row, pipeline_mode=once),
            scratch_shapes=[pltpu.VMEM((tm, d), BF16), pltpu.VMEM((rows, tm), BF16), pltpu.VMEM((tm, rows), BF16),
                            pltpu.VMEM((rows, d), BF16), pltpu.VMEM((tm, LANES), BF16),
                            pltpu.VMEM((tm, LANES), BF16)]),
        out_shape=jax.ShapeDtypeStruct((t, d), F32),
        compiler_params=_params(("arbitrary", "arbitrary")),
        name="moe",
    )(cnt, x1, route, routet, ffn_g.reshape(1, d), grouped(w_eg), grouped(w_eu), grouped(w_ed),
      final_g.reshape(1, d))


def _tile(n, want):
    while n % want:
        want //= 2
    return want


def _trunk(x, attn_g, w_in, sink, gq, gk, w_oa, w_ob, w_out, ffn_g, w_rg, b_rg, w_re, b_re, w_eg, w_eu, w_ed,
           final_g):
    b, s, d = x.shape
    x2 = x.reshape(b * s, d)
    qa, ka, va, qb, kb, vb, ga, gb = _in_proj(x2, s, attn_g, w_in, gq, gk, _tile(s, 512))
    ya = _win_attn(qa, ka, va, sink, s, _tile(s, 512))
    yb = _glob_attn(qb.reshape(b, s, Q_W), kb.reshape(b, s, KV_W), vb.reshape(b, s, KV_W),
                    256, 256).reshape(b * s, Q_W)
    tm_merge, tm_moe = _tile(b * s, 512), _tile(b * s, 1024)
    x1, route, routet, cnt = _merge(x2, ya, yb, ga, gb, w_oa, w_ob, w_out, ffn_g, w_rg, b_rg, w_re, b_re, tm_merge)
    cnt = cnt[:, 0, :N_GROUPS].reshape(b * s // tm_moe, tm_moe // tm_merge, N_GROUPS).sum(axis=1)
    y = _moe(x1, route, routet, cnt, ffn_g, w_eg, w_eu, w_ed, final_g, tm_moe)
    return y.reshape(b, s, d)


def kernel(x_prompt, x_sample, attn_norm_g, w_in, a_sink, b_q_norm_g, b_k_norm_g, w_oa, w_ob, w_out, ffn_norm_g,
           w_router_group, b_router_group, w_router_expert, b_router_expert, w_expert_gate, w_expert_up,
           w_expert_down, final_norm_g):
    assert attn_norm_g.shape[0] == 1, "single-layer trunk"
    weights = (attn_norm_g[0], w_in[0], a_sink[0], b_q_norm_g[0], b_k_norm_g[0], w_oa[0], w_ob[0], w_out[0],
               ffn_norm_g[0], w_router_group[0], b_router_group[0], w_router_expert[0], b_router_expert[0],
               w_expert_gate[0], w_expert_up[0], w_expert_down[0], final_norm_g)
    return (_trunk(x_prompt, *weights), _trunk(x_sample, *weights))
```

```python
import functools

import jax
import jax.numpy as jnp
import numpy as np
from jax import lax
from jax.experimental import pallas as pl
from jax.experimental.pallas import tpu as pltpu

HEAD_DIM = 64
Q_HEADS = 8
KV_HEADS = 2
GROUP = Q_HEADS // KV_HEADS
Q_W = Q_HEADS * HEAD_DIM
KV_W = KV_HEADS * HEAD_DIM
WINDOW = 128
BLOCK = 128
GRID_W = 64
ROPE_THETA = 10000.0
N_GROUPS = 4
EXPERTS_PER_GROUP = 4
N_EXPERTS = N_GROUPS * EXPERTS_PER_GROUP
EPS = 1e-6
NEG_INF = -1e30
SCALE = HEAD_DIM ** -0.5
LOG2E = 1.4426950408889634
ONES_ROWS = 16
VT_ROWS = HEAD_DIM + ONES_ROWS
LANES = 128

VMEM_LIMIT = 56 * 1024 * 1024

F32 = jnp.float32
BF16 = jnp.bfloat16


def _params(sem):
    return pltpu.CompilerParams(dimension_semantics=sem, vmem_limit_bytes=VMEM_LIMIT)


def _full(shape):
    return pl.BlockSpec(shape, lambda *_: (0,) * len(shape))


def _hi_lo(v):
    top = lax.bitcast_convert_type(lax.bitcast_convert_type(v, jnp.uint32) & jnp.uint32(0xFFFF0000), F32)
    return top.astype(BF16), (v - top).astype(BF16)


def _split_dot(y, ones_bd):
    hi, lo = _hi_lo(y)
    return jnp.dot(hi, ones_bd, preferred_element_type=F32) + jnp.dot(lo, ones_bd, preferred_element_type=F32)


def _pair_swap(y):
    n = y.shape[-1]
    lane = lax.broadcasted_iota(jnp.int32, y.shape, y.ndim - 1)
    nxt = pltpu.roll(y, n - 1, y.ndim - 1)
    prv = pltpu.roll(y, 1, y.ndim - 1)
    return jnp.where((lane & 1) == 0, nxt, prv)


def _norm_rope(y, gain, ones_bd, cos, sin):
    ms = _split_dot(y * y, ones_bd) * (1.0 / HEAD_DIM)
    yn = y * lax.rsqrt(ms + EPS) * gain
    return yn * cos + _pair_swap(yn) * sin


def _in_proj_kernel(x_ref, g_ref, w_ref, cos_ref, sin_ref, gq_ref, gk_ref, oq_ref, ok_ref,
                    qa_ref, ka_ref, va_ref, qb_ref, kb_ref, vb_ref, ga_ref, gb_ref):
    x = x_ref[...]
    ms = jnp.mean(x * x, axis=-1, keepdims=True)
    h = (x * lax.rsqrt(ms + EPS) * g_ref[...]).astype(BF16)

    def proj(lo, width):
        return jnp.dot(h, w_ref[:, lo:lo + width], preferred_element_type=F32)

    o = 0
    qa_ref[...] = (proj(o, Q_W) * (SCALE * LOG2E)).astype(BF16); o += Q_W
    ka_ref[...] = proj(o, KV_W).astype(BF16); o += KV_W
    va_ref[...] = proj(o, KV_W).astype(BF16); o += KV_W
    cos = cos_ref[...]
    sin = sin_ref[...]
    qb = proj(o, Q_W); o += Q_W
    cos_q = jnp.concatenate([cos] * (Q_W // LANES), axis=1)
    sin_q = jnp.concatenate([sin] * (Q_W // LANES), axis=1)
    qb_ref[...] = (_norm_rope(qb, gq_ref[...], oq_ref[...], cos_q, sin_q) * (SCALE * LOG2E)).astype(BF16)
    kb = proj(o, KV_W); o += KV_W
    kb_ref[...] = _norm_rope(kb, gk_ref[...], ok_ref[...], cos, sin).astype(BF16)
    vb_ref[...] = proj(o, KV_W).astype(BF16); o += KV_W
    d = x.shape[-1]
    ga_ref[...] = jax.nn.sigmoid(proj(o, d)).astype(BF16); o += d
    gb_ref[...] = jax.nn.sigmoid(proj(o, d)).astype(BF16)


def _rope_tables(seq_len):
    rows = seq_len // GRID_W
    row = jnp.repeat(jnp.arange(rows, dtype=F32), GRID_W)
    col = jnp.tile(jnp.arange(GRID_W, dtype=F32), rows)
    half = HEAD_DIM // 2
    inv = ROPE_THETA ** (-jnp.arange(0, half, 2, dtype=F32) / half)
    ang = jnp.concatenate([row[:, None] * inv, col[:, None] * inv], axis=-1)
    cos = jnp.repeat(jnp.cos(ang), 2, axis=-1)
    sin = jnp.repeat(jnp.sin(ang), 2, axis=-1) * jnp.tile(jnp.array([-1.0, 1.0], F32), half)
    reps = LANES // HEAD_DIM
    return jnp.tile(cos, (1, reps)), jnp.tile(sin, (1, reps))


def _block_ones(width):
    idx = np.arange(width) // HEAD_DIM
    return jnp.asarray(idx[:, None] == idx[None, :], dtype=BF16)


def _in_proj(x2, seq_len, attn_g, w_in, gq, gk, tm):
    t, d = x2.shape
    in_w = w_in.shape[1]
    cos, sin = _rope_tables(seq_len)
    nseq = seq_len // tm
    row = lambda i: (i, 0)
    pos = lambda i: (i % nseq, 0)
    outs = [(Q_W, BF16), (KV_W, BF16), (KV_W, BF16), (Q_W, BF16), (KV_W, BF16), (KV_W, BF16), (d, BF16), (d, BF16)]
    return pl.pallas_call(
        _in_proj_kernel,
        grid=(t // tm,),
        in_specs=[pl.BlockSpec((tm, d), row), _full((1, d)), _full((d, in_w)),
                  pl.BlockSpec((tm, LANES), pos), pl.BlockSpec((tm, LANES), pos),
                  _full((1, Q_W)), _full((1, KV_W)), _full((Q_W, Q_W)), _full((KV_W, KV_W))],
        out_specs=[pl.BlockSpec((tm, w), row) for w, _ in outs],
        out_shape=[jax.ShapeDtypeStruct((t, w), dt) for w, dt in outs],
        compiler_params=_params(("parallel",)),
        name="in_proj",
    )(x2, attn_g.reshape(1, d), w_in.astype(BF16), cos, sin,
      jnp.tile(gq.astype(F32), Q_HEADS).reshape(1, Q_W), jnp.tile(gk.astype(F32), KV_HEADS).reshape(1, KV_W),
      _block_ones(Q_W), _block_ones(KV_W))


PAIRS = Q_HEADS // 2
N_EDGE = 4


def _win_bias():
    slopes = np.exp2(-8.0 * np.arange(1, Q_HEADS + 1, dtype=np.float32) / Q_HEADS).astype(np.float32)
    key = np.arange(3 * BLOCK)[:, None]
    dist = np.abs(np.arange(BLOCK)[None, :] - (key - BLOCK))
    tabs = []
    for edge in range(N_EDGE):
        alive = dist <= WINDOW
        if edge & 1:
            alive = alive & (key >= BLOCK)
        if edge & 2:
            alive = alive & (key < 2 * BLOCK)
        heads = [np.where(alive, -slopes[h] * np.float32(LOG2E) * dist.astype(np.float32), np.float32(NEG_INF))
                 for h in range(Q_HEADS)]
        tabs.append(np.stack([np.concatenate([heads[2 * p], heads[2 * p + 1]], axis=1) for p in range(PAIRS)]))
    return jnp.asarray(np.stack(tabs), dtype=F32)


def _win_attn_kernel(sink_ref, q_ref, kc_ref, kp_ref, kn_ref, vc_ref, vp_ref, vn_ref, bias_ref, o_ref,
                     kh_scr, vt_scr, qp_scr, s_scr, p_scr, ot_scr, *, blocks_per_seq, sub):
    i = pl.program_id(0)
    tq = q_ref.shape[0]
    ones = jnp.ones((ONES_ROWS, BLOCK), BF16)
    for off, n, kr, vr in ((0, BLOCK, kp_ref, vp_ref), (BLOCK, tq, kc_ref, vc_ref), (BLOCK + tq, BLOCK, kn_ref, vn_ref)):
        k = kr[...]
        vt = vr[...].astype(F32).T.astype(BF16)
        for g in range(KV_HEADS):
            kh_scr[g, off:off + n, :] = k[:, g * HEAD_DIM:(g + 1) * HEAD_DIM]
            vt_scr[g * VT_ROWS:g * VT_ROWS + HEAD_DIM, off:off + n] = vt[g * HEAD_DIM:(g + 1) * HEAD_DIM, :]
            for o in range(off, off + n, BLOCK):
                vt_scr[g * VT_ROWS + HEAD_DIM:(g + 1) * VT_ROWS, o:o + BLOCK] = ones
    qt = q_ref[...].astype(F32).T.astype(BF16)
    for j in range(sub):
        for p in range(PAIRS):
            for u in range(2):
                h = 2 * p + u
                qp_scr[j, p, :, u * BLOCK:(u + 1) * BLOCK] = qt[h * HEAD_DIM:(h + 1) * HEAD_DIM,
                                                                j * BLOCK:(j + 1) * BLOCK]
    lane = lax.broadcasted_iota(jnp.int32, (1, 2 * BLOCK), 1)
    for j in range(sub):
        x = j % 2
        blk = (i * sub + j) % blocks_per_seq
        edge = (blk == 0).astype(jnp.int32) + 2 * (blk == blocks_per_seq - 1).astype(jnp.int32)
        for p in range(PAIRS):
            g = (2 * p) // GROUP
            s = jnp.dot(kh_scr[g, j * BLOCK:(j + 3) * BLOCK, :], qp_scr[j, p], preferred_element_type=F32)
            s_scr[x, p] = s + bias_ref[edge, p]
        sink_terms = []
        for p in range(PAIRS):
            logits = s_scr[x, p]
            sink = jnp.where(lane < BLOCK, sink_ref[2 * p], sink_ref[2 * p + 1]) * LOG2E
            m = jnp.maximum(jnp.max(logits, axis=0, keepdims=True), sink)
            p_scr[x, p] = jnp.exp2(logits - m).astype(BF16)
            sink_terms.append(jnp.exp2(sink - m))
        for p in range(PAIRS):
            g = (2 * p) // GROUP
            ot = jnp.dot(vt_scr[g * VT_ROWS:(g + 1) * VT_ROWS, j * BLOCK:(j + 3) * BLOCK], p_scr[x, p],
                         preferred_element_type=F32)
            out = ot[0:HEAD_DIM] / (ot[HEAD_DIM:HEAD_DIM + 1] + sink_terms[p])
            for u in range(2):
                h = 2 * p + u
                ot_scr[h * HEAD_DIM:(h + 1) * HEAD_DIM, j * BLOCK:(j + 1) * BLOCK] = out[:, u * BLOCK:(u + 1) * BLOCK]
    o_ref[...] = ot_scr[...].T.astype(o_ref.dtype)


def _win_attn(qa, ka, va, sink, seq_len, tq):
    t = qa.shape[0]
    sub = tq // BLOCK
    nblk = t // BLOCK
    win = tq + 2 * BLOCK
    row = lambda i: (i, 0)
    prev = lambda i: (jnp.maximum(i * sub - 1, 0), 0)
    nxt = lambda i: (jnp.minimum((i + 1) * sub, nblk - 1), 0)
    kv_specs = [pl.BlockSpec((tq, KV_W), row), pl.BlockSpec((BLOCK, KV_W), prev), pl.BlockSpec((BLOCK, KV_W), nxt)]
    bias_spec = pl.BlockSpec((N_EDGE, PAIRS, 3 * BLOCK, 2 * BLOCK), lambda i: (0, 0, 0, 0),
                             pipeline_mode=pl.Buffered(1))
    return pl.pallas_call(
        functools.partial(_win_attn_kernel, blocks_per_seq=seq_len // BLOCK, sub=sub),
        grid=(t // tq,),
        in_specs=[pl.BlockSpec(memory_space=pltpu.SMEM), pl.BlockSpec((tq, Q_W), row)] + kv_specs + kv_specs
                 + [bias_spec],
        out_specs=pl.BlockSpec((tq, Q_W), row),
        out_shape=jax.ShapeDtypeStruct((t, Q_W), BF16),
        scratch_shapes=[pltpu.VMEM((KV_HEADS, win, HEAD_DIM), BF16), pltpu.VMEM((KV_HEADS * VT_ROWS, win), BF16),
                        pltpu.VMEM((sub, PAIRS, HEAD_DIM, 2 * BLOCK), BF16),
                        pltpu.VMEM((2, PAIRS, 3 * BLOCK, 2 * BLOCK), F32),
                        pltpu.VMEM((2, PAIRS, 3 * BLOCK, 2 * BLOCK), BF16), pltpu.VMEM((Q_W, tq), F32)],
        compiler_params=_params(("parallel",)),
        name="win_attn",
    )(sink.astype(F32), qa, ka, ka, ka, va, va, va, _win_bias())


def _glob_attn_kernel(q_ref, k_ref, v_ref, o_ref, kh_ref, vt_ref, qt_scr, s_scr, p_scr, cm_scr, m_scr, a_scr, acc_scr,
                      ot_scr, *, tk):
    tq = q_ref.shape[1]
    nk = k_ref.shape[1] // tk

    @pl.when(pl.program_id(1) == 0)
    def _():
        ones = jnp.ones((ONES_ROWS, tk), BF16)

        def fill(c, carry):
            off = pl.multiple_of(c * tk, tk)
            vt = v_ref[0, pl.ds(off, tk), :].astype(F32).T.astype(BF16)
            kc = k_ref[0, pl.ds(off, tk), :]
            for g in range(KV_HEADS):
                kh_ref[g, pl.ds(off, tk), :] = kc[:, g * HEAD_DIM:(g + 1) * HEAD_DIM]
                vt_ref[g * VT_ROWS:g * VT_ROWS + HEAD_DIM, pl.ds(off, tk)] = vt[g * HEAD_DIM:(g + 1) * HEAD_DIM, :]
                vt_ref[g * VT_ROWS + HEAD_DIM:(g + 1) * VT_ROWS, pl.ds(off, tk)] = ones
            return carry

        lax.fori_loop(0, nk, fill, 0)

    qt_scr[...] = q_ref[0].astype(F32).T.astype(BF16)

    for g in range(KV_HEADS):
        heads = range(g * GROUP, (g + 1) * GROUP)

        def stage_a(c, x, heads=heads, g=g):
            kc = kh_ref[g, pl.ds(pl.multiple_of(c * tk, tk), tk), :]
            for i, h in enumerate(heads):
                s = jnp.dot(kc, qt_scr[h * HEAD_DIM:(h + 1) * HEAD_DIM, :], preferred_element_type=F32)
                s_scr[x, i] = s
                cm_scr[x, i] = jnp.max(s, axis=0, keepdims=True)

        def stage_b(x):
            for i in range(GROUP):
                m = m_scr[i]
                m_new = jnp.maximum(m, cm_scr[x, i])
                a_scr[x, i] = jnp.exp2(m - m_new)
                m_scr[i] = m_new
                p_scr[x, i] = jnp.exp2(s_scr[x, i] - m_new).astype(BF16)

        def stage_c(c, x, g=g):
            vt = vt_ref[g * VT_ROWS:(g + 1) * VT_ROWS, pl.ds(pl.multiple_of(c * tk, tk), tk)]
            for i in range(GROUP):
                acc_scr[i] = a_scr[x, i] * acc_scr[i] + jnp.dot(vt, p_scr[x, i], preferred_element_type=F32)

        def step(c, x, last=False):
            stage_c(c - 1, 1 - x)
            stage_b(x)
            if not last:
                stage_a(c + 1, 1 - x)

        for i in range(GROUP):
            m_scr[i] = jnp.full((1, tq), -jnp.inf, F32)
            acc_scr[i] = jnp.zeros((VT_ROWS, tq), F32)
        stage_a(0, 0)
        stage_b(0)
        stage_a(1, 1)

        def pair(j, carry):
            step(2 * j + 1, 1)
            step(2 * j + 2, 0)
            return carry

        lax.fori_loop(0, (nk - 2) // 2, pair, 0, unroll=2)
        step(nk - 1, 1, last=True)
        stage_c(nk - 1, 1)
        for i, h in enumerate(heads):
            acc = acc_scr[i]
            ot_scr[h * HEAD_DIM:(h + 1) * HEAD_DIM, :] = acc[0:HEAD_DIM] / acc[HEAD_DIM:HEAD_DIM + 1]
    o_ref[0] = ot_scr[...].T.astype(o_ref.dtype)


def _glob_attn(qb, kb, vb, tq, tk):
    b, s, _ = qb.shape
    assert s % (2 * tk) == 0, "the chunk pipeline walks key chunks in pairs"
    return pl.pallas_call(
        functools.partial(_glob_attn_kernel, tk=tk),
        grid=(b, s // tq),
        in_specs=[pl.BlockSpec((1, tq, Q_W), lambda bi, qi: (bi, qi, 0)),
                  pl.BlockSpec((1, s, KV_W), lambda bi, qi: (bi, 0, 0)),
                  pl.BlockSpec((1, s, KV_W), lambda bi, qi: (bi, 0, 0))],
        out_specs=pl.BlockSpec((1, tq, Q_W), lambda bi, qi: (bi, qi, 0)),
        out_shape=jax.ShapeDtypeStruct((b, s, Q_W), BF16),
        scratch_shapes=[pltpu.VMEM((KV_HEADS, s, HEAD_DIM), BF16), pltpu.VMEM((KV_HEADS * VT_ROWS, s), BF16),
                        pltpu.VMEM((Q_W, tq), BF16),
                        pltpu.VMEM((2, GROUP, tk, tq), F32), pltpu.VMEM((2, GROUP, tk, tq), BF16),
                        pltpu.VMEM((2, GROUP, 1, tq), F32), pltpu.VMEM((GROUP, 1, tq), F32),
                        pltpu.VMEM((2, GROUP, 1, tq), F32), pltpu.VMEM((GROUP, VT_ROWS, tq), F32),
                        pltpu.VMEM((Q_W, tq), F32)],
        compiler_params=_params(("arbitrary", "arbitrary")),
        name="glob_attn",
    )(qb, kb, vb)


def _first_argmax(vals, idx, big):
    m = jnp.max(vals, axis=-1, keepdims=True)
    return m, jnp.min(jnp.where(vals == m, idx, big), axis=-1, keepdims=True)


def _route(logits):
    rows = logits.shape[0]
    gl = logits[:, 0:N_GROUPS]
    gidx = lax.broadcasted_iota(jnp.int32, (rows, N_GROUPS), 1)
    gmax, g_sel = _first_argmax(gl, gidx, N_GROUPS)
    g_w = 1.0 / jnp.sum(jnp.exp(gl - gmax), axis=-1, keepdims=True)
    e_sel = jnp.zeros((rows, EXPERTS_PER_GROUP), F32)
    for g in range(N_GROUPS):
        lo = N_GROUPS + g * EXPERTS_PER_GROUP
        e_sel = jnp.where(g_sel == g, logits[:, lo:lo + EXPERTS_PER_GROUP], e_sel)
    eidx = lax.broadcasted_iota(jnp.int32, (rows, EXPERTS_PER_GROUP), 1)
    ex = jnp.exp(e_sel - jnp.max(e_sel, axis=-1, keepdims=True))
    e_prob = ex / jnp.sum(ex, axis=-1, keepdims=True)
    p1, i1 = _first_argmax(e_prob, eidx, EXPERTS_PER_GROUP)
    rest = jnp.where(eidx == i1, -1.0, e_prob)
    p2, i2 = _first_argmax(rest, eidx, EXPERTS_PER_GROUP)
    tot = p1 + p2
    lane = lax.broadcasted_iota(jnp.int32, (rows, LANES), 1)
    out = jnp.where(lane == g_sel, 1.0, 0.0)
    out = out + jnp.where(lane == N_GROUPS + i1, g_w * (p1 / tot), 0.0)
    return out + jnp.where(lane == N_GROUPS + i2, g_w * (p2 / tot), 0.0)


def _merge_kernel(x_ref, ya_ref, yb_ref, ga_ref, gb_ref, woa_ref, wob_ref, wout_ref, gf_ref, wr_ref, br_ref,
                  x1_ref, route_ref, routet_ref, cnt_ref):
    tm = x_ref.shape[0]
    half = tm // 2
    cnt = jnp.zeros((1, LANES), F32)
    for r in (0, half):
        rows = slice(r, r + half)
        a = jnp.dot(ya_ref[rows, :], woa_ref[...], preferred_element_type=F32)
        b = jnp.dot(yb_ref[rows, :], wob_ref[...], preferred_element_type=F32)
        merged = ga_ref[rows, :].astype(F32) * a + gb_ref[rows, :].astype(F32) * b
        x1 = x_ref[rows, :] + jnp.dot(merged.astype(BF16), wout_ref[...], preferred_element_type=F32)
        x1_ref[rows, :] = x1
        t = _rmsnorm(x1, gf_ref[...])
        t_hi, t_lo = _hi_lo(t)
        big = jnp.dot(t_hi, wr_ref[...], preferred_element_type=F32)
        small = jnp.dot(t_lo, wr_ref[:, 0:LANES], preferred_element_type=F32)
        logits = big[:, 0:LANES] + big[:, LANES:2 * LANES] + small + br_ref[...]
        route = _route(logits)
        route_ref[rows, :] = route
        routet_ref[:, rows] = route.T
        cnt = cnt + jnp.sum(route, axis=0, keepdims=True)
    cnt_ref[0] = jnp.broadcast_to(cnt, cnt_ref.shape[1:]).astype(jnp.int32)


def _merge(x2, ya, yb, ga, gb, w_oa, w_ob, w_out, ffn_g, w_rg, b_rg, w_re, b_re, tm):
    t, d = x2.shape
    wr = jnp.zeros((d, LANES), F32).at[:, :N_GROUPS].set(w_rg).at[:, N_GROUPS:N_GROUPS + N_EXPERTS].set(w_re)
    wr_hi, wr_lo = _hi_lo(wr)
    br = jnp.zeros((1, LANES), F32).at[0, :N_GROUPS].set(b_rg).at[0, N_GROUPS:N_GROUPS + N_EXPERTS].set(b_re)
    row = lambda i: (i, 0)
    return pl.pallas_call(
        _merge_kernel,
        grid=(t // tm,),
        in_specs=[pl.BlockSpec((tm, d), row), pl.BlockSpec((tm, Q_W), row), pl.BlockSpec((tm, Q_W), row),
                  pl.BlockSpec((tm, d), row), pl.BlockSpec((tm, d), row),
                  _full((Q_W, d)), _full((Q_W, d)), _full((d, d)), _full((1, d)), _full((d, 2 * LANES)),
                  _full((1, LANES))],
        out_specs=[pl.BlockSpec((tm, d), row), pl.BlockSpec((tm, LANES), row),
                   pl.BlockSpec((LANES, tm), lambda i: (0, i)), pl.BlockSpec((1, 8, LANES), lambda i: (i, 0, 0))],
        out_shape=[jax.ShapeDtypeStruct((t, d), F32), jax.ShapeDtypeStruct((t, LANES), F32),
                   jax.ShapeDtypeStruct((LANES, t), F32), jax.ShapeDtypeStruct((t // tm, 8, LANES), jnp.int32)],
        compiler_params=_params(("parallel",)),
        name="merge",
    )(x2, ya, yb, ga, gb, w_oa.astype(BF16), w_ob.astype(BF16), w_out.astype(BF16), ffn_g.reshape(1, d),
      jnp.concatenate([wr_hi, wr_lo], axis=1), br)


ROW_ALIGN = 128
BIG_BLOCK = 2 * ROW_ALIGN
GATHER_BLOCK = 4 * ROW_ALIGN


def _rmsnorm(x, g):
    ms = jnp.mean(x * x, axis=-1, keepdims=True)
    return x * lax.rsqrt(ms + EPS) * g


def _moe_kernel(cnt_ref, x1_ref, route_ref, routet_ref, gf_ref, wg_ref, wu_ref, wd_ref, gfin_ref, o_ref,
                p_scr, q_scr, ts_scr, ws_scr, ys_scr):
    i = pl.program_id(0)
    g = pl.program_id(1)
    tm = x1_ref.shape[0]
    rows = p_scr.shape[0]
    offs, off = [], 0
    for gg in range(N_GROUPS):
        offs.append(off)
        off = off + (cnt_ref[i, gg] + (ROW_ALIGN - 1)) // ROW_ALIGN * ROW_ALIGN

    @pl.when(g == 0)
    def _():
        ys_scr[...] = jnp.zeros_like(ys_scr)
        route = route_ref[...]
        lane = lax.broadcasted_iota(jnp.int32, (tm, LANES), 1)
        onehot = jnp.where(lane < N_GROUPS, route, 0.0)
        r_i = lax.broadcasted_iota(jnp.int32, (tm, tm), 0)
        c_i = lax.broadcasted_iota(jnp.int32, (tm, tm), 1)
        before = jnp.dot(jnp.where(c_i < r_i, 1.0, 0.0).astype(BF16), onehot.astype(BF16),
                         preferred_element_type=F32)
        goff = jnp.zeros((tm, LANES), F32)
        for gg in range(N_GROUPS):
            goff = jnp.where(lane == gg, jnp.asarray(offs[gg], jnp.int32).astype(F32), goff)
        dest_col = jnp.sum((before + goff) * onehot, axis=1, keepdims=True).astype(jnp.int32)
        q_scr[...] = jnp.where(dest_col == lax.broadcasted_iota(jnp.int32, (tm, rows), 1), 1.0, 0.0).astype(BF16)
        sub = lax.broadcasted_iota(jnp.int32, (16, tm), 0)
        onehot_t = jnp.where(sub < N_GROUPS, routet_ref[0:16, :], 0.0)
        before_t = jnp.dot(onehot_t.astype(BF16), jnp.where(r_i < c_i, 1.0, 0.0).astype(BF16),
                           preferred_element_type=F32)
        goff_t = jnp.zeros((16, tm), F32)
        for gg in range(N_GROUPS):
            goff_t = jnp.where(sub == gg, jnp.asarray(offs[gg], jnp.int32).astype(F32), goff_t)
        dest_row = jnp.sum((before_t + goff_t) * onehot_t, axis=0, keepdims=True).astype(jnp.int32)
        for r0 in range(0, rows, BIG_BLOCK):
            rid = lax.broadcasted_iota(jnp.int32, (BIG_BLOCK, tm), 0) + r0
            p_scr[r0:r0 + BIG_BLOCK, :] = jnp.where(dest_row == rid, 1.0, 0.0).astype(BF16)
        t = _rmsnorm(x1_ref[...], gf_ref[...]).astype(BF16)
        rh, rl = _hi_lo(route)
        for r0 in range(0, rows, GATHER_BLOCK):
            pb = p_scr[r0:r0 + GATHER_BLOCK, :]
            ts_scr[r0:r0 + GATHER_BLOCK, :] = jnp.dot(pb, t, preferred_element_type=F32).astype(BF16)
            ws_scr[r0:r0 + GATHER_BLOCK, :] = (jnp.dot(pb, rh, preferred_element_type=F32)
                                               + jnp.dot(pb, rl, preferred_element_type=F32))

    def ffn(r, m):
        tb = ts_scr[pl.ds(pl.multiple_of(r, ROW_ALIGN), m), :]
        wb = ws_scr[pl.ds(pl.multiple_of(r, ROW_ALIGN), m), :]
        y = jnp.zeros((m, o_ref.shape[1]), F32)
        for e in range(EXPERTS_PER_GROUP):
            a = jax.nn.silu(jnp.dot(tb, wg_ref[0, e], preferred_element_type=F32)) * jnp.dot(
                tb, wu_ref[0, e], preferred_element_type=F32)
            y = y + wb[:, N_GROUPS + e:N_GROUPS + e + 1] * jnp.dot(a.astype(BF16), wd_ref[0, e],
                                                                   preferred_element_type=F32)
        ys_scr[pl.ds(pl.multiple_of(r, ROW_ALIGN), m), :] = y.astype(BF16)

    start = offs[0]
    for gg in range(1, N_GROUPS):
        start = jnp.where(g == gg, offs[gg], start)
    n_small = (cnt_ref[i, g] + (ROW_ALIGN - 1)) // ROW_ALIGN
    n_big = n_small // 2

    def big(j, carry):
        ffn(start + j * BIG_BLOCK, BIG_BLOCK)
        return carry

    lax.fori_loop(0, n_big, big, 0)

    @pl.when(n_small % 2 == 1)
    def _():
        ffn(start + n_big * BIG_BLOCK, ROW_ALIGN)

    @pl.when(g == N_GROUPS - 1)
    def _():
        x2 = x1_ref[...] + jnp.dot(q_scr[...], ys_scr[...], preferred_element_type=F32)
        o_ref[...] = _rmsnorm(x2, gfin_ref[...])


def _moe(x1, route, routet, cnt, ffn_g, w_eg, w_eu, w_ed, final_g, tm):
    t, d = x1.shape
    de = w_eg.shape[-1]
    rows = tm + N_GROUPS * ROW_ALIGN
    grouped = lambda w: w.astype(BF16).reshape((N_GROUPS, EXPERTS_PER_GROUP) + w.shape[1:])
    row = lambda i, g, c: (i, 0)
    const = lambda i, g, c: (0, 0)
    wspec = lambda a, b: pl.BlockSpec((1, EXPERTS_PER_GROUP, a, b), lambda i, g, c: (g, 0, 0, 0))
    once = pl.Buffered(1)
    return pl.pallas_call(
        _moe_kernel,
        grid_spec=pltpu.PrefetchScalarGridSpec(
            num_scalar_prefetch=1,
            grid=(t // tm, N_GROUPS),
            in_specs=[pl.BlockSpec((tm, d), row, pipeline_mode=once), pl.BlockSpec((tm, LANES), row),
                      pl.BlockSpec((LANES, tm), lambda i, g, c: (0, i)), pl.BlockSpec((1, d), const),
                      wspec(d, de), wspec(d, de), wspec(de, d), pl.BlockSpec((1, d), const)],
            out_specs=pl.BlockSpec((tm, d), row, pipeline_mode=once),
            scratch_shapes=[pltpu.VMEM((rows, tm), BF16), pltpu.VMEM((tm, rows), BF16), pltpu.VMEM((rows, d), BF16),
                            pltpu.VMEM((rows, LANES), F32), pltpu.VMEM((rows, d), BF16)]),
        out_shape=jax.ShapeDtypeStruct((t, d), F32),
        compiler_params=_params(("arbitrary", "arbitrary")),
        name="moe",
    )(cnt, x1, route, routet, ffn_g.reshape(1, d), grouped(w_eg), grouped(w_eu), grouped(w_ed),
      final_g.reshape(1, d))


def _tile(n, want):
    while n % want:
        want //= 2
    return want


def _trunk(x, attn_g, w_in, sink, gq, gk, w_oa, w_ob, w_out, ffn_g, w_rg, b_rg, w_re, b_re, w_eg, w_eu, w_ed,
           final_g):
    b, s, d = x.shape
    x2 = x.reshape(b * s, d)
    qa, ka, va, qb, kb, vb, ga, gb = _in_proj(x2, s, attn_g, w_in, gq, gk, _tile(s, 512))
    ya = _win_attn(qa, ka, va, sink, s, _tile(s, 512))
    yb = _glob_attn(qb.reshape(b, s, Q_W), kb.reshape(b, s, KV_W), vb.reshape(b, s, KV_W),
                    256, 256).reshape(b * s, Q_W)
    tm_merge, tm_moe = _tile(b * s, 1024), _tile(b * s, 1024)
    x1, route, routet, cnt = _merge(x2, ya, yb, ga, gb, w_oa, w_ob, w_out, ffn_g, w_rg, b_rg, w_re, b_re, tm_merge)
    cnt = cnt[:, 0, :N_GROUPS].reshape(b * s // tm_moe, tm_moe // tm_merge, N_GROUPS).sum(axis=1)
    y = _moe(x1, route, routet, cnt, ffn_g, w_eg, w_eu, w_ed, final_g, tm_moe)
    return y.reshape(b, s, d)


def kernel(x_prompt, x_sample, attn_norm_g, w_in, a_sink, b_q_norm_g, b_k_norm_g, w_oa, w_ob, w_out, ffn_norm_g,
           w_router_group, b_router_group, w_router_expert, b_router_expert, w_expert_gate, w_expert_up,
           w_expert_down, final_norm_g):
    assert attn_norm_g.shape[0] == 1, "single-layer trunk"
    weights = (attn_norm_g[0], w_in[0], a_sink[0], b_q_norm_g[0], b_k_norm_g[0], w_oa[0], w_ob[0], w_out[0],
               ffn_norm_g[0], w_router_group[0], b_router_group[0], w_router_expert[0], b_router_expert[0],
               w_expert_gate[0], w_expert_up[0], w_expert_down[0], final_norm_g)
    return (_trunk(x_prompt, *weights), _trunk(x_sample, *weights))
```

```python
import functools

import jax
import jax.numpy as jnp
import numpy as np
from jax import lax
from jax.experimental import pallas as pl
from jax.experimental.pallas import tpu as pltpu

HEAD_DIM = 64
Q_HEADS = 8
KV_HEADS = 2
GROUP = Q_HEADS // KV_HEADS
Q_W = Q_HEADS * HEAD_DIM
KV_W = KV_HEADS * HEAD_DIM
WINDOW = 128
BLOCK = 128
GRID_W = 64
ROPE_THETA = 10000.0
N_GROUPS = 4
EXPERTS_PER_GROUP = 4
N_EXPERTS = N_GROUPS * EXPERTS_PER_GROUP
EPS = 1e-6
NEG_INF = -1e30
SCALE = HEAD_DIM ** -0.5
LOG2E = 1.4426950408889634
ONES_ROWS = 16
VT_ROWS = HEAD_DIM + ONES_ROWS
LANES = 128

VMEM_LIMIT = 56 * 1024 * 1024

F32 = jnp.float32
BF16 = jnp.bfloat16


def _params(sem):
    return pltpu.CompilerParams(dimension_semantics=sem, vmem_limit_bytes=VMEM_LIMIT)


def _full(shape):
    return pl.BlockSpec(shape, lambda *_: (0,) * len(shape))


def _hi_lo(v):
    top = lax.bitcast_convert_type(lax.bitcast_convert_type(v, jnp.uint32) & jnp.uint32(0xFFFF0000), F32)
    return top.astype(BF16), (v - top).astype(BF16)


def _split_dot(y, ones_bd):
    hi, lo = _hi_lo(y)
    return jnp.dot(hi, ones_bd, preferred_element_type=F32) + jnp.dot(lo, ones_bd, preferred_element_type=F32)


def _pair_swap(y):
    n = y.shape[-1]
    lane = lax.broadcasted_iota(jnp.int32, y.shape, y.ndim - 1)
    nxt = pltpu.roll(y, n - 1, y.ndim - 1)
    prv = pltpu.roll(y, 1, y.ndim - 1)
    return jnp.where((lane & 1) == 0, nxt, prv)


def _norm_rope(y, gain, ones_bd, cos, sin):
    ms = _split_dot(y * y, ones_bd) * (1.0 / HEAD_DIM)
    yn = y * lax.rsqrt(ms + EPS) * gain
    return yn * cos + _pair_swap(yn) * sin


def _in_proj_kernel(x_ref, g_ref, w_ref, cos_ref, sin_ref, gq_ref, gk_ref, oq_ref, ok_ref,
                    qa_ref, ka_ref, va_ref, qb_ref, kb_ref, vb_ref, ga_ref, gb_ref):
    x = x_ref[...]
    ms = jnp.mean(x * x, axis=-1, keepdims=True)
    h = (x * lax.rsqrt(ms + EPS) * g_ref[...]).astype(BF16)

    def proj(lo, width):
        return jnp.dot(h, w_ref[:, lo:lo + width], preferred_element_type=F32)

    o = 0
    qa_ref[...] = (proj(o, Q_W) * (SCALE * LOG2E)).astype(BF16); o += Q_W
    ka_ref[...] = proj(o, KV_W).astype(BF16); o += KV_W
    va_ref[...] = proj(o, KV_W).astype(BF16); o += KV_W
    cos = cos_ref[...]
    sin = sin_ref[...]
    qb = proj(o, Q_W); o += Q_W
    cos_q = jnp.concatenate([cos] * (Q_W // LANES), axis=1)
    sin_q = jnp.concatenate([sin] * (Q_W // LANES), axis=1)
    qb_ref[...] = (_norm_rope(qb, gq_ref[...], oq_ref[...], cos_q, sin_q) * (SCALE * LOG2E)).astype(BF16)
    kb = proj(o, KV_W); o += KV_W
    kb_ref[...] = _norm_rope(kb, gk_ref[...], ok_ref[...], cos, sin).astype(BF16)
    vb_ref[...] = proj(o, KV_W).astype(BF16); o += KV_W
    d = x.shape[-1]
    ga_ref[...] = jax.nn.sigmoid(proj(o, d)).astype(BF16); o += d
    gb_ref[...] = jax.nn.sigmoid(proj(o, d)).astype(BF16)


def _rope_tables(seq_len):
    rows = seq_len // GRID_W
    row = jnp.repeat(jnp.arange(rows, dtype=F32), GRID_W)
    col = jnp.tile(jnp.arange(GRID_W, dtype=F32), rows)
    half = HEAD_DIM // 2
    inv = ROPE_THETA ** (-jnp.arange(0, half, 2, dtype=F32) / half)
    ang = jnp.concatenate([row[:, None] * inv, col[:, None] * inv], axis=-1)
    cos = jnp.repeat(jnp.cos(ang), 2, axis=-1)
    sin = jnp.repeat(jnp.sin(ang), 2, axis=-1) * jnp.tile(jnp.array([-1.0, 1.0], F32), half)
    reps = LANES // HEAD_DIM
    return jnp.tile(cos, (1, reps)), jnp.tile(sin, (1, reps))


def _block_ones(width):
    idx = np.arange(width) // HEAD_DIM
    return jnp.asarray(idx[:, None] == idx[None, :], dtype=BF16)


def _in_proj(x2, seq_len, attn_g, w_in, gq, gk, tm):
    t, d = x2.shape
    in_w = w_in.shape[1]
    cos, sin = _rope_tables(seq_len)
    nseq = seq_len // tm
    row = lambda i: (i, 0)
    pos = lambda i: (i % nseq, 0)
    outs = [(Q_W, BF16), (KV_W, BF16), (KV_W, BF16), (Q_W, BF16), (KV_W, BF16), (KV_W, BF16), (d, BF16), (d, BF16)]
    return pl.pallas_call(
        _in_proj_kernel,
        grid=(t // tm,),
        in_specs=[pl.BlockSpec((tm, d), row), _full((1, d)), _full((d, in_w)),
                  pl.BlockSpec((tm, LANES), pos), pl.BlockSpec((tm, LANES), pos),
                  _full((1, Q_W)), _full((1, KV_W)), _full((Q_W, Q_W)), _full((KV_W, KV_W))],
        out_specs=[pl.BlockSpec((tm, w), row) for w, _ in outs],
        out_shape=[jax.ShapeDtypeStruct((t, w), dt) for w, dt in outs],
        compiler_params=_params(("parallel",)),
        name="in_proj",
    )(x2, attn_g.reshape(1, d), w_in.astype(BF16), cos, sin,
      jnp.tile(gq.astype(F32), Q_HEADS).reshape(1, Q_W), jnp.tile(gk.astype(F32), KV_HEADS).reshape(1, KV_W),
      _block_ones(Q_W), _block_ones(KV_W))


PAIRS = Q_HEADS // 2
N_EDGE = 4


def _win_bias():
    slopes = np.exp2(-8.0 * np.arange(1, Q_HEADS + 1, dtype=np.float32) / Q_HEADS).astype(np.float32)
    key = np.arange(3 * BLOCK)[:, None]
    dist = np.abs(np.arange(BLOCK)[None, :] - (key - BLOCK))
    tabs = []
    for edge in range(N_EDGE):
        alive = dist <= WINDOW
        if edge & 1:
            alive = alive & (key >= BLOCK)
        if edge & 2:
            alive = alive & (key < 2 * BLOCK)
        heads = [np.where(alive, -slopes[h] * np.float32(LOG2E) * dist.astype(np.float32), np.float32(NEG_INF))
                 for h in range(Q_HEADS)]
        tabs.append(np.stack([np.concatenate([heads[2 * p], heads[2 * p + 1]], axis=1) for p in range(PAIRS)]))
    return jnp.asarray(np.stack(tabs), dtype=F32)


def _win_attn_kernel(sink_ref, q_ref, kc_ref, kp_ref, kn_ref, vc_ref, vp_ref, vn_ref, bias_ref, o_ref,
                     kh_scr, vt_scr, qp_scr, s_scr, p_scr, ot_scr, *, blocks_per_seq, sub):
    i = pl.program_id(0)
    tq = q_ref.shape[0]
    ones = jnp.ones((ONES_ROWS, BLOCK), BF16)
    for off, n, kr, vr in ((0, BLOCK, kp_ref, vp_ref), (BLOCK, tq, kc_ref, vc_ref), (BLOCK + tq, BLOCK, kn_ref, vn_ref)):
        k = kr[...]
        vt = vr[...].astype(F32).T.astype(BF16)
        for g in range(KV_HEADS):
            kh_scr[g, off:off + n, :] = k[:, g * HEAD_DIM:(g + 1) * HEAD_DIM]
            vt_scr[g * VT_ROWS:g * VT_ROWS + HEAD_DIM, off:off + n] = vt[g * HEAD_DIM:(g + 1) * HEAD_DIM, :]
            for o in range(off, off + n, BLOCK):
                vt_scr[g * VT_ROWS + HEAD_DIM:(g + 1) * VT_ROWS, o:o + BLOCK] = ones
    qt = q_ref[...].astype(F32).T.astype(BF16)
    for j in range(sub):
        for p in range(PAIRS):
            for u in range(2):
                h = 2 * p + u
                qp_scr[j, p, :, u * BLOCK:(u + 1) * BLOCK] = qt[h * HEAD_DIM:(h + 1) * HEAD_DIM,
                                                                j * BLOCK:(j + 1) * BLOCK]
    lane = lax.broadcasted_iota(jnp.int32, (1, 2 * BLOCK), 1)
    for j in range(sub):
        x = j % 2
        blk = (i * sub + j) % blocks_per_seq
        edge = (blk == 0).astype(jnp.int32) + 2 * (blk == blocks_per_seq - 1).astype(jnp.int32)
        for p in range(PAIRS):
            g = (2 * p) // GROUP
            s = jnp.dot(kh_scr[g, j * BLOCK:(j + 3) * BLOCK, :], qp_scr[j, p], preferred_element_type=F32)
            s_scr[x, p] = s + bias_ref[edge, p]
        sink_terms = []
        for p in range(PAIRS):
            logits = s_scr[x, p]
            sink = jnp.where(lane < BLOCK, sink_ref[2 * p], sink_ref[2 * p + 1]) * LOG2E
            m = jnp.maximum(jnp.max(logits, axis=0, keepdims=True), sink)
            p_scr[x, p] = jnp.exp2(logits - m).astype(BF16)
            sink_terms.append(jnp.exp2(sink - m))
        for p in range(PAIRS):
            g = (2 * p) // GROUP
            ot = jnp.dot(vt_scr[g * VT_ROWS:(g + 1) * VT_ROWS, j * BLOCK:(j + 3) * BLOCK], p_scr[x, p],
                         preferred_element_type=F32)
            out = ot[0:HEAD_DIM] / (ot[HEAD_DIM:HEAD_DIM + 1] + sink_terms[p])
            for u in range(2):
                h = 2 * p + u
                ot_scr[h * HEAD_DIM:(h + 1) * HEAD_DIM, j * BLOCK:(j + 1) * BLOCK] = out[:, u * BLOCK:(u + 1) * BLOCK]
    o_ref[...] = ot_scr[...].T.astype(o_ref.dtype)


def _win_attn(qa, ka, va, sink, seq_len, tq):
    t = qa.shape[0]
    sub = tq // BLOCK
    nblk = t // BLOCK
    win = tq + 2 * BLOCK
    row = lambda i: (i, 0)
    prev = lambda i: (jnp.maximum(i * sub - 1, 0), 0)
    nxt = lambda i: (jnp.minimum((i + 1) * sub, nblk - 1), 0)
    kv_specs = [pl.BlockSpec((tq, KV_W), row), pl.BlockSpec((BLOCK, KV_W), prev), pl.BlockSpec((BLOCK, KV_W), nxt)]
    bias_spec = pl.BlockSpec((N_EDGE, PAIRS, 3 * BLOCK, 2 * BLOCK), lambda i: (0, 0, 0, 0),
                             pipeline_mode=pl.Buffered(1))
    return pl.pallas_call(
        functools.partial(_win_attn_kernel, blocks_per_seq=seq_len // BLOCK, sub=sub),
        grid=(t // tq,),
        in_specs=[pl.BlockSpec(memory_space=pltpu.SMEM), pl.BlockSpec((tq, Q_W), row)] + kv_specs + kv_specs
                 + [bias_spec],
        out_specs=pl.BlockSpec((tq, Q_W), row),
        out_shape=jax.ShapeDtypeStruct((t, Q_W), BF16),
        scratch_shapes=[pltpu.VMEM((KV_HEADS, win, HEAD_DIM), BF16), pltpu.VMEM((KV_HEADS * VT_ROWS, win), BF16),
                        pltpu.VMEM((sub, PAIRS, HEAD_DIM, 2 * BLOCK), BF16),
                        pltpu.VMEM((2, PAIRS, 3 * BLOCK, 2 * BLOCK), F32),
                        pltpu.VMEM((2, PAIRS, 3 * BLOCK, 2 * BLOCK), BF16), pltpu.VMEM((Q_W, tq), F32)],
        compiler_params=_params(("parallel",)),
        name="win_attn",
    )(sink.astype(F32), qa, ka, ka, ka, va, va, va, _win_bias())


def _glob_attn_kernel(q_ref, k_ref, v_ref, o_ref, kh_ref, vt_ref, qt_scr, s_scr, p_scr, cm_scr, m_scr, a_scr, acc_scr,
                      ot_scr, *, tk):
    tq = q_ref.shape[1]
    nk = k_ref.shape[1] // tk

    @pl.when(pl.program_id(1) == 0)
    def _():
        ones = jnp.ones((ONES_ROWS, tk), BF16)

        def fill(c, carry):
            off = pl.multiple_of(c * tk, tk)
            vt = v_ref[0, pl.ds(off, tk), :].astype(F32).T.astype(BF16)
            kc = k_ref[0, pl.ds(off, tk), :]
            for g in range(KV_HEADS):
                kh_ref[g, pl.ds(off, tk), :] = kc[:, g * HEAD_DIM:(g + 1) * HEAD_DIM]
                vt_ref[g * VT_ROWS:g * VT_ROWS + HEAD_DIM, pl.ds(off, tk)] = vt[g * HEAD_DIM:(g + 1) * HEAD_DIM, :]
                vt_ref[g * VT_ROWS + HEAD_DIM:(g + 1) * VT_ROWS, pl.ds(off, tk)] = ones
            return carry

        lax.fori_loop(0, nk, fill, 0)

    qt_scr[...] = q_ref[0].astype(F32).T.astype(BF16)

    for g in range(KV_HEADS):
        heads = range(g * GROUP, (g + 1) * GROUP)

        def stage_a(c, x, heads=heads, g=g):
            kc = kh_ref[g, pl.ds(pl.multiple_of(c * tk, tk), tk), :]
            for i, h in enumerate(heads):
                s = jnp.dot(kc, qt_scr[h * HEAD_DIM:(h + 1) * HEAD_DIM, :], preferred_element_type=F32)
                s_scr[x, i] = s
                cm_scr[x, i] = jnp.max(s, axis=0, keepdims=True)

        def stage_b(x):
            for i in range(GROUP):
                m = m_scr[i]
                m_new = jnp.maximum(m, cm_scr[x, i])
                a_scr[x, i] = jnp.exp2(m - m_new)
                m_scr[i] = m_new
                p_scr[x, i] = jnp.exp2(s_scr[x, i] - m_new).astype(BF16)

        def stage_c(c, x, g=g):
            vt = vt_ref[g * VT_ROWS:(g + 1) * VT_ROWS, pl.ds(pl.multiple_of(c * tk, tk), tk)]
            for i in range(GROUP):
                acc_scr[i] = a_scr[x, i] * acc_scr[i] + jnp.dot(vt, p_scr[x, i], preferred_element_type=F32)

        def step(c, x, last=False):
            stage_c(c - 1, 1 - x)
            stage_b(x)
            if not last:
                stage_a(c + 1, 1 - x)

        for i in range(GROUP):
            m_scr[i] = jnp.full((1, tq), -jnp.inf, F32)
            acc_scr[i] = jnp.zeros((VT_ROWS, tq), F32)
        stage_a(0, 0)
        stage_b(0)
        stage_a(1, 1)

        def pair(j, carry):
            step(2 * j + 1, 1)
            step(2 * j + 2, 0)
            return carry

        lax.fori_loop(0, (nk - 2) // 2, pair, 0, unroll=4)
        step(nk - 1, 1, last=True)
        stage_c(nk - 1, 1)
        for i, h in enumerate(heads):
            acc = acc_scr[i]
            ot_scr[h * HEAD_DIM:(h + 1) * HEAD_DIM, :] = acc[0:HEAD_DIM] / acc[HEAD_DIM:HEAD_DIM + 1]
    o_ref[0] = ot_scr[...].T.astype(o_ref.dtype)


def _glob_attn(qb, kb, vb, tq, tk):
    b, s, _ = qb.shape
    assert s % (2 * tk) == 0, "the chunk pipeline walks key chunks in pairs"
    return pl.pallas_call(
        functools.partial(_glob_attn_kernel, tk=tk),
        grid=(b, s // tq),
        in_specs=[pl.BlockSpec((1, tq, Q_W), lambda bi, qi: (bi, qi, 0)),
                  pl.BlockSpec((1, s, KV_W), lambda bi, qi: (bi, 0, 0)),
                  pl.BlockSpec((1, s, KV_W), lambda bi, qi: (bi, 0, 0))],
        out_specs=pl.BlockSpec((1, tq, Q_W), lambda bi, qi: (bi, qi, 0)),
        out_shape=jax.ShapeDtypeStruct((b, s, Q_W), BF16),
        scratch_shapes=[pltpu.VMEM((KV_HEADS, s, HEAD_DIM), BF16), pltpu.VMEM((KV_HEADS * VT_ROWS, s), BF16),
                        pltpu.VMEM((Q_W, tq), BF16),
                        pltpu.VMEM((2, GROUP, tk, tq), F32), pltpu.VMEM((2, GROUP, tk, tq), BF16),
                        pltpu.VMEM((2, GROUP, 1, tq), F32), pltpu.VMEM((GROUP, 1, tq), F32),
                        pltpu.VMEM((2, GROUP, 1, tq), F32), pltpu.VMEM((GROUP, VT_ROWS, tq), F32),
                        pltpu.VMEM((Q_W, tq), F32)],
        compiler_params=_params(("arbitrary", "arbitrary")),
        name="glob_attn",
    )(qb, kb, vb)


def _first_argmax(vals, idx, big):
    m = jnp.max(vals, axis=-1, keepdims=True)
    return m, jnp.min(jnp.where(vals == m, idx, big), axis=-1, keepdims=True)


def _route(logits):
    rows = logits.shape[0]
    gl = logits[:, 0:N_GROUPS]
    gidx = lax.broadcasted_iota(jnp.int32, (rows, N_GROUPS), 1)
    gmax, g_sel = _first_argmax(gl, gidx, N_GROUPS)
    g_w = 1.0 / jnp.sum(jnp.exp(gl - gmax), axis=-1, keepdims=True)
    e_sel = jnp.zeros((rows, EXPERTS_PER_GROUP), F32)
    for g in range(N_GROUPS):
        lo = N_GROUPS + g * EXPERTS_PER_GROUP
        e_sel = jnp.where(g_sel == g, logits[:, lo:lo + EXPERTS_PER_GROUP], e_sel)
    eidx = lax.broadcasted_iota(jnp.int32, (rows, EXPERTS_PER_GROUP), 1)
    ex = jnp.exp(e_sel - jnp.max(e_sel, axis=-1, keepdims=True))
    e_prob = ex / jnp.sum(ex, axis=-1, keepdims=True)
    p1, i1 = _first_argmax(e_prob, eidx, EXPERTS_PER_GROUP)
    rest = jnp.where(eidx == i1, -1.0, e_prob)
    p2, i2 = _first_argmax(rest, eidx, EXPERTS_PER_GROUP)
    tot = p1 + p2
    lane = lax.broadcasted_iota(jnp.int32, (rows, LANES), 1)
    out = jnp.where(lane == g_sel, 1.0, 0.0)
    out = out + jnp.where(lane == N_GROUPS + i1, g_w * (p1 / tot), 0.0)
    return out + jnp.where(lane == N_GROUPS + i2, g_w * (p2 / tot), 0.0)


def _merge_kernel(x_ref, ya_ref, yb_ref, ga_ref, gb_ref, woa_ref, wob_ref, wout_ref, gf_ref, wr_ref, br_ref,
                  x1_ref, route_ref, routet_ref, cnt_ref):
    tm = x_ref.shape[0]
    half = tm // 2
    cnt = jnp.zeros((1, LANES), F32)
    for r in (0, half):
        rows = slice(r, r + half)
        a = jnp.dot(ya_ref[rows, :], woa_ref[...], preferred_element_type=F32)
        b = jnp.dot(yb_ref[rows, :], wob_ref[...], preferred_element_type=F32)
        merged = ga_ref[rows, :].astype(F32) * a + gb_ref[rows, :].astype(F32) * b
        x1 = x_ref[rows, :] + jnp.dot(merged.astype(BF16), wout_ref[...], preferred_element_type=F32)
        x1_ref[rows, :] = x1
        t = _rmsnorm(x1, gf_ref[...])
        t_hi, t_lo = _hi_lo(t)
        big = jnp.dot(t_hi, wr_ref[...], preferred_element_type=F32)
        small = jnp.dot(t_lo, wr_ref[:, 0:LANES], preferred_element_type=F32)
        logits = big[:, 0:LANES] + big[:, LANES:2 * LANES] + small + br_ref[...]
        route = _route(logits)
        route_ref[rows, :] = route
        routet_ref[:, rows] = route.T
        cnt = cnt + jnp.sum(route, axis=0, keepdims=True)
    cnt_ref[0] = jnp.broadcast_to(cnt, cnt_ref.shape[1:]).astype(jnp.int32)


def _merge(x2, ya, yb, ga, gb, w_oa, w_ob, w_out, ffn_g, w_rg, b_rg, w_re, b_re, tm):
    t, d = x2.shape
    wr = jnp.zeros((d, LANES), F32).at[:, :N_GROUPS].set(w_rg).at[:, N_GROUPS:N_GROUPS + N_EXPERTS].set(w_re)
    wr_hi, wr_lo = _hi_lo(wr)
    br = jnp.zeros((1, LANES), F32).at[0, :N_GROUPS].set(b_rg).at[0, N_GROUPS:N_GROUPS + N_EXPERTS].set(b_re)
    row = lambda i: (i, 0)
    return pl.pallas_call(
        _merge_kernel,
        grid=(t // tm,),
        in_specs=[pl.BlockSpec((tm, d), row), pl.BlockSpec((tm, Q_W), row), pl.BlockSpec((tm, Q_W), row),
                  pl.BlockSpec((tm, d), row), pl.BlockSpec((tm, d), row),
                  _full((Q_W, d)), _full((Q_W, d)), _full((d, d)), _full((1, d)), _full((d, 2 * LANES)),
                  _full((1, LANES))],
        out_specs=[pl.BlockSpec((tm, d), row), pl.BlockSpec((tm, LANES), row),
                   pl.BlockSpec((LANES, tm), lambda i: (0, i)), pl.BlockSpec((1, 8, LANES), lambda i: (i, 0, 0))],
        out_shape=[jax.ShapeDtypeStruct((t, d), F32), jax.ShapeDtypeStruct((t, LANES), F32),
                   jax.ShapeDtypeStruct((LANES, t), F32), jax.ShapeDtypeStruct((t // tm, 8, LANES), jnp.int32)],
        compiler_params=_params(("parallel",)),
        name="merge",
    )(x2, ya, yb, ga, gb, w_oa.astype(BF16), w_ob.astype(BF16), w_out.astype(BF16), ffn_g.reshape(1, d),
      jnp.concatenate([wr_hi, wr_lo], axis=1), br)


ROW_ALIGN = 128
BIG_BLOCK = 2 * ROW_ALIGN
GATHER_BLOCK = 4 * ROW_ALIGN
MAX_WHOLE = 4


def _rmsnorm(x, g):
    ms = jnp.mean(x * x, axis=-1, keepdims=True)
    return x * lax.rsqrt(ms + EPS) * g


def _moe_kernel(cnt_ref, x1_ref, route_ref, routet_ref, gf_ref, wg_ref, wu_ref, wd_ref, gfin_ref, o_ref,
                p_scr, q_scr, ts_scr, ws_scr):
    i = pl.program_id(0)
    g = pl.program_id(1)
    tm = x1_ref.shape[0]
    rows = p_scr.shape[0]
    offs, off = [], 0
    for gg in range(N_GROUPS):
        offs.append(off)
        off = off + (cnt_ref[i, gg] + (ROW_ALIGN - 1)) // ROW_ALIGN * ROW_ALIGN

    @pl.when(g == 0)
    def _():
        route = route_ref[...]
        lane = lax.broadcasted_iota(jnp.int32, (tm, LANES), 1)
        onehot = jnp.where(lane < N_GROUPS, route, 0.0)
        r_i = lax.broadcasted_iota(jnp.int32, (tm, tm), 0)
        c_i = lax.broadcasted_iota(jnp.int32, (tm, tm), 1)
        before = jnp.dot(jnp.where(c_i < r_i, 1.0, 0.0).astype(BF16), onehot.astype(BF16),
                         preferred_element_type=F32)
        goff = jnp.zeros((tm, LANES), F32)
        for gg in range(N_GROUPS):
            goff = jnp.where(lane == gg, jnp.asarray(offs[gg], jnp.int32).astype(F32), goff)
        dest_col = jnp.sum((before + goff) * onehot, axis=1, keepdims=True).astype(jnp.int32)
        q_scr[...] = jnp.where(dest_col == lax.broadcasted_iota(jnp.int32, (tm, rows), 1), 1.0, 0.0).astype(BF16)
        sub = lax.broadcasted_iota(jnp.int32, (16, tm), 0)
        onehot_t = jnp.where(sub < N_GROUPS, routet_ref[0:16, :], 0.0)
        before_t = jnp.dot(onehot_t.astype(BF16), jnp.where(r_i < c_i, 1.0, 0.0).astype(BF16),
                           preferred_element_type=F32)
        goff_t = jnp.zeros((16, tm), F32)
        for gg in range(N_GROUPS):
            goff_t = jnp.where(sub == gg, jnp.asarray(offs[gg], jnp.int32).astype(F32), goff_t)
        dest_row = jnp.sum((before_t + goff_t) * onehot_t, axis=0, keepdims=True).astype(jnp.int32)
        for r0 in range(0, rows, BIG_BLOCK):
            rid = lax.broadcasted_iota(jnp.int32, (BIG_BLOCK, tm), 0) + r0
            p_scr[r0:r0 + BIG_BLOCK, :] = jnp.where(dest_row == rid, 1.0, 0.0).astype(BF16)
        t = _rmsnorm(x1_ref[...], gf_ref[...]).astype(BF16)
        rh, rl = _hi_lo(route)
        for r0 in range(0, rows, GATHER_BLOCK):
            pb = p_scr[r0:r0 + GATHER_BLOCK, :]
            ts_scr[r0:r0 + GATHER_BLOCK, :] = jnp.dot(pb, t, preferred_element_type=F32).astype(BF16)
            ws_scr[r0:r0 + GATHER_BLOCK, :] = (jnp.dot(pb, rh, preferred_element_type=F32)
                                               + jnp.dot(pb, rl, preferred_element_type=F32))

    def ffn(r, m):
        tb = ts_scr[pl.ds(pl.multiple_of(r, ROW_ALIGN), m), :]
        wb = ws_scr[pl.ds(pl.multiple_of(r, ROW_ALIGN), m), :]
        y = jnp.zeros((m, o_ref.shape[1]), F32)
        for e in range(EXPERTS_PER_GROUP):
            a = jax.nn.silu(jnp.dot(tb, wg_ref[0, e], preferred_element_type=F32)) * jnp.dot(
                tb, wu_ref[0, e], preferred_element_type=F32)
            y = y + wb[:, N_GROUPS + e:N_GROUPS + e + 1] * jnp.dot(a.astype(BF16), wd_ref[0, e],
                                                                   preferred_element_type=F32)
        ts_scr[pl.ds(pl.multiple_of(r, ROW_ALIGN), m), :] = y.astype(BF16)

    start = offs[0]
    for gg in range(1, N_GROUPS):
        start = jnp.where(g == gg, offs[gg], start)
    n_small = (cnt_ref[i, g] + (ROW_ALIGN - 1)) // ROW_ALIGN

    for units in range(1, MAX_WHOLE + 1):
        @pl.when(n_small == units)
        def _(units=units):
            ffn(start, units * ROW_ALIGN)

    @pl.when(n_small > MAX_WHOLE)
    def _():
        n_big = n_small // 2

        def big(j, carry):
            ffn(start + j * BIG_BLOCK, BIG_BLOCK)
            return carry

        lax.fori_loop(0, n_big, big, 0)

        @pl.when(n_small % 2 == 1)
        def _():
            ffn(start + n_big * BIG_BLOCK, ROW_ALIGN)

    @pl.when(g == N_GROUPS - 1)
    def _():
        x2 = x1_ref[...] + jnp.dot(q_scr[...], ts_scr[...], preferred_element_type=F32)
        o_ref[...] = _rmsnorm(x2, gfin_ref[...])


def _moe(x1, route, routet, cnt, ffn_g, w_eg, w_eu, w_ed, final_g, tm):
    t, d = x1.shape
    de = w_eg.shape[-1]
    rows = tm + N_GROUPS * ROW_ALIGN
    grouped = lambda w: w.astype(BF16).reshape((N_GROUPS, EXPERTS_PER_GROUP) + w.shape[1:])
    row = lambda i, g, c: (i, 0)
    const = lambda i, g, c: (0, 0)
    wspec = lambda a, b: pl.BlockSpec((1, EXPERTS_PER_GROUP, a, b), lambda i, g, c: (g, 0, 0, 0))
    return pl.pallas_call(
        _moe_kernel,
        grid_spec=pltpu.PrefetchScalarGridSpec(
            num_scalar_prefetch=1,
            grid=(t // tm, N_GROUPS),
            in_specs=[pl.BlockSpec((tm, d), row), pl.BlockSpec((tm, LANES), row),
                      pl.BlockSpec((LANES, tm), lambda i, g, c: (0, i)), pl.BlockSpec((1, d), const),
                      wspec(d, de), wspec(d, de), wspec(de, d), pl.BlockSpec((1, d), const)],
            out_specs=pl.BlockSpec((tm, d), row),
            scratch_shapes=[pltpu.VMEM((rows, tm), BF16), pltpu.VMEM((tm, rows), BF16), pltpu.VMEM((rows, d), BF16),
                            pltpu.VMEM((rows, LANES), F32)]),
        out_shape=jax.ShapeDtypeStruct((t, d), F32),
        compiler_params=_params(("arbitrary", "arbitrary")),
        name="moe",
    )(cnt, x1, route, routet, ffn_g.reshape(1, d), grouped(w_eg), grouped(w_eu), grouped(w_ed),
      final_g.reshape(1, d))


def _tile(n, want):
    while n % want:
        want //= 2
    return want


def _trunk(x, attn_g, w_in, sink, gq, gk, w_oa, w_ob, w_out, ffn_g, w_rg, b_rg, w_re, b_re, w_eg, w_eu, w_ed,
           final_g):
    b, s, d = x.shape
    x2 = x.reshape(b * s, d)
    qa, ka, va, qb, kb, vb, ga, gb = _in_proj(x2, s, attn_g, w_in, gq, gk, _tile(s, 512))
    ya = _win_attn(qa, ka, va, sink, s, _tile(s, 512))
    yb = _glob_attn(qb.reshape(b, s, Q_W), kb.reshape(b, s, KV_W), vb.reshape(b, s, KV_W),
                    256, 256).reshape(b * s, Q_W)
    tm_merge, tm_moe = _tile(b * s, 1024), _tile(b * s, 1024)
    x1, route, routet, cnt = _merge(x2, ya, yb, ga, gb, w_oa, w_ob, w_out, ffn_g, w_rg, b_rg, w_re, b_re, tm_merge)
    cnt = cnt[:, 0, :N_GROUPS].reshape(b * s // tm_moe, tm_moe // tm_merge, N_GROUPS).sum(axis=1)
    y = _moe(x1, route, routet, cnt, ffn_g, w_eg, w_eu, w_ed, final_g, tm_moe)
    return y.reshape(b, s, d)


def kernel(x_prompt, x_sample, attn_norm_g, w_in, a_sink, b_q_norm_g, b_k_norm_g, w_oa, w_ob, w_out, ffn_norm_g,
           w_router_group, b_router_group, w_router_expert, b_router_expert, w_expert_gate, w_expert_up,
           w_expert_down, final_norm_g):
    assert attn_norm_g.shape[0] == 1, "single-layer trunk"
    weights = (attn_norm_g[0], w_in[0], a_sink[0], b_q_norm_g[0], b_k_norm_g[0], w_oa[0], w_ob[0], w_out[0],
               ffn_norm_g[0], w_router_group[0], b_router_group[0], w_router_expert[0], b_router_expert[0],
               w_expert_gate[0], w_expert_up[0], w_expert_down[0], final_norm_g)
    return (_trunk(x_prompt, *weights), _trunk(x_sample, *weights))
```

```python
import functools

import jax
import jax.numpy as jnp
import numpy as np
from jax import lax
from jax.experimental import pallas as pl
from jax.experimental.pallas import tpu as pltpu

HEAD_DIM = 64
Q_HEADS = 8
KV_HEADS = 2
GROUP = Q_HEADS // KV_HEADS
Q_W = Q_HEADS * HEAD_DIM
KV_W = KV_HEADS * HEAD_DIM
WINDOW = 128
BLOCK = 128
GRID_W = 64
ROPE_THETA = 10000.0
N_GROUPS = 4
EXPERTS_PER_GROUP = 4
N_EXPERTS = N_GROUPS * EXPERTS_PER_GROUP
EPS = 1e-6
NEG_INF = -1e30
SCALE = HEAD_DIM ** -0.5
LOG2E = 1.4426950408889634
ONES_ROWS = 16
VT_ROWS = HEAD_DIM + ONES_ROWS
SCORE_DTYPE = jnp.bfloat16
LANES = 128

VMEM_LIMIT = 56 * 1024 * 1024

F32 = jnp.float32
BF16 = jnp.bfloat16


def _params(sem):
    return pltpu.CompilerParams(dimension_semantics=sem, vmem_limit_bytes=VMEM_LIMIT)


def _full(shape):
    return pl.BlockSpec(shape, lambda *_: (0,) * len(shape))


def _hi_lo(v):
    top = lax.bitcast_convert_type(lax.bitcast_convert_type(v, jnp.uint32) & jnp.uint32(0xFFFF0000), F32)
    return top.astype(BF16), (v - top).astype(BF16)


def _split_dot(y, ones_bd):
    hi, lo = _hi_lo(y)
    return jnp.dot(hi, ones_bd, preferred_element_type=F32) + jnp.dot(lo, ones_bd, preferred_element_type=F32)


def _pair_swap(y):
    n = y.shape[-1]
    lane = lax.broadcasted_iota(jnp.int32, y.shape, y.ndim - 1)
    nxt = pltpu.roll(y, n - 1, y.ndim - 1)
    prv = pltpu.roll(y, 1, y.ndim - 1)
    return jnp.where((lane & 1) == 0, nxt, prv)


def _norm_rope(y, gain, ones_bd, cos, sin):
    ms = _split_dot(y * y, ones_bd) * (1.0 / HEAD_DIM)
    yn = y * lax.rsqrt(ms + EPS) * gain
    return yn * cos + _pair_swap(yn) * sin


def _in_proj_kernel(x_ref, g_ref, w_ref, cos_ref, sin_ref, gq_ref, gk_ref, oq_ref, ok_ref,
                    qa_ref, ka_ref, va_ref, qb_ref, kb_ref, vb_ref, ga_ref, gb_ref):
    half = x_ref.shape[0] // 2
    d = x_ref.shape[1]
    for r in (0, half):
        rows = slice(r, r + half)
        x = x_ref[rows, :]
        ms = jnp.mean(x * x, axis=-1, keepdims=True)
        h = (x * lax.rsqrt(ms + EPS) * g_ref[...]).astype(BF16)

        def proj(lo, width, h=h):
            return jnp.dot(h, w_ref[:, lo:lo + width], preferred_element_type=F32)

        o = 0
        qa_ref[rows, :] = (proj(o, Q_W) * (SCALE * LOG2E)).astype(BF16); o += Q_W
        ka_ref[rows, :] = proj(o, KV_W).astype(BF16); o += KV_W
        va_ref[rows, :] = proj(o, KV_W).astype(BF16); o += KV_W
        cos = cos_ref[rows, :]
        sin = sin_ref[rows, :]
        qb = proj(o, Q_W); o += Q_W
        cos_q = jnp.concatenate([cos] * (Q_W // LANES), axis=1)
        sin_q = jnp.concatenate([sin] * (Q_W // LANES), axis=1)
        qb_ref[rows, :] = (_norm_rope(qb, gq_ref[...], oq_ref[...], cos_q, sin_q) * (SCALE * LOG2E)).astype(BF16)
        kb = proj(o, KV_W); o += KV_W
        kb_ref[rows, :] = _norm_rope(kb, gk_ref[...], ok_ref[...], cos, sin).astype(BF16)
        vb_ref[rows, :] = proj(o, KV_W).astype(BF16); o += KV_W
        ga_ref[rows, :] = jax.nn.sigmoid(proj(o, d)).astype(BF16); o += d
        gb_ref[rows, :] = jax.nn.sigmoid(proj(o, d)).astype(BF16)


def _rope_tables(seq_len):
    rows = seq_len // GRID_W
    row = jnp.repeat(jnp.arange(rows, dtype=F32), GRID_W)
    col = jnp.tile(jnp.arange(GRID_W, dtype=F32), rows)
    half = HEAD_DIM // 2
    inv = ROPE_THETA ** (-jnp.arange(0, half, 2, dtype=F32) / half)
    ang = jnp.concatenate([row[:, None] * inv, col[:, None] * inv], axis=-1)
    cos = jnp.repeat(jnp.cos(ang), 2, axis=-1)
    sin = jnp.repeat(jnp.sin(ang), 2, axis=-1) * jnp.tile(jnp.array([-1.0, 1.0], F32), half)
    reps = LANES // HEAD_DIM
    return jnp.tile(cos, (1, reps)), jnp.tile(sin, (1, reps))


def _block_ones(width):
    idx = np.arange(width) // HEAD_DIM
    return jnp.asarray(idx[:, None] == idx[None, :], dtype=BF16)


def _in_proj(x2, seq_len, attn_g, w_in, gq, gk, tm):
    t, d = x2.shape
    in_w = w_in.shape[1]
    cos, sin = _rope_tables(seq_len)
    nseq = seq_len // tm
    row = lambda i: (i, 0)
    pos = lambda i: (i % nseq, 0)
    outs = [(Q_W, BF16), (KV_W, BF16), (KV_W, BF16), (Q_W, BF16), (KV_W, BF16), (KV_W, BF16), (d, BF16), (d, BF16)]
    return pl.pallas_call(
        _in_proj_kernel,
        grid=(t // tm,),
        in_specs=[pl.BlockSpec((tm, d), row), _full((1, d)), _full((d, in_w)),
                  pl.BlockSpec((tm, LANES), pos), pl.BlockSpec((tm, LANES), pos),
                  _full((1, Q_W)), _full((1, KV_W)), _full((Q_W, Q_W)), _full((KV_W, KV_W))],
        out_specs=[pl.BlockSpec((tm, w), row) for w, _ in outs],
        out_shape=[jax.ShapeDtypeStruct((t, w), dt) for w, dt in outs],
        compiler_params=_params(("parallel",)),
        name="in_proj",
    )(x2, attn_g.reshape(1, d), w_in.astype(BF16), cos, sin,
      jnp.tile(gq.astype(F32), Q_HEADS).reshape(1, Q_W), jnp.tile(gk.astype(F32), KV_HEADS).reshape(1, KV_W),
      _block_ones(Q_W), _block_ones(KV_W))


PAIRS = Q_HEADS // 2
N_EDGE = 4


def _win_bias():
    slopes = np.exp2(-8.0 * np.arange(1, Q_HEADS + 1, dtype=np.float32) / Q_HEADS).astype(np.float32)
    key = np.arange(3 * BLOCK)[:, None]
    dist = np.abs(np.arange(BLOCK)[None, :] - (key - BLOCK))
    tabs = []
    for edge in range(N_EDGE):
        alive = dist <= WINDOW
        if edge & 1:
            alive = alive & (key >= BLOCK)
        if edge & 2:
            alive = alive & (key < 2 * BLOCK)
        heads = [np.where(alive, -slopes[h] * np.float32(LOG2E) * dist.astype(np.float32), np.float32(NEG_INF))
                 for h in range(Q_HEADS)]
        tabs.append(np.stack([np.concatenate([heads[2 * p], heads[2 * p + 1]], axis=1) for p in range(PAIRS)]))
    return jnp.asarray(np.stack(tabs), dtype=F32)


def _win_attn_kernel(sink_ref, q_ref, kc_ref, kp_ref, kn_ref, vc_ref, vp_ref, vn_ref, bias_ref, o_ref,
                     kh_scr, vt_scr, qp_scr, s_scr, p_scr, ot_scr, *, blocks_per_seq, sub):
    i = pl.program_id(0)
    tq = q_ref.shape[0]
    ones = jnp.ones((ONES_ROWS, BLOCK), BF16)
    for off, n, kr, vr in ((0, BLOCK, kp_ref, vp_ref), (BLOCK, tq, kc_ref, vc_ref), (BLOCK + tq, BLOCK, kn_ref, vn_ref)):
        k = kr[...]
        vt = vr[...].astype(F32).T.astype(BF16)
        for g in range(KV_HEADS):
            kh_scr[g, off:off + n, :] = k[:, g * HEAD_DIM:(g + 1) * HEAD_DIM]
            vt_scr[g * VT_ROWS:g * VT_ROWS + HEAD_DIM, off:off + n] = vt[g * HEAD_DIM:(g + 1) * HEAD_DIM, :]
            for o in range(off, off + n, BLOCK):
                vt_scr[g * VT_ROWS + HEAD_DIM:(g + 1) * VT_ROWS, o:o + BLOCK] = ones
    qt = q_ref[...].astype(F32).T.astype(BF16)
    for j in range(sub):
        for p in range(PAIRS):
            for u in range(2):
                h = 2 * p + u
                qp_scr[j, p, :, u * BLOCK:(u + 1) * BLOCK] = qt[h * HEAD_DIM:(h + 1) * HEAD_DIM,
                                                                j * BLOCK:(j + 1) * BLOCK]
    lane = lax.broadcasted_iota(jnp.int32, (1, 2 * BLOCK), 1)
    for j in range(sub):
        x = j % 2
        blk = (i * sub + j) % blocks_per_seq
        edge = (blk == 0).astype(jnp.int32) + 2 * (blk == blocks_per_seq - 1).astype(jnp.int32)
        for p in range(PAIRS):
            g = (2 * p) // GROUP
            s = jnp.dot(kh_scr[g, j * BLOCK:(j + 3) * BLOCK, :], qp_scr[j, p], preferred_element_type=F32)
            s_scr[x, p] = s + bias_ref[edge, p]
        sink_terms = []
        for p in range(PAIRS):
            logits = s_scr[x, p]
            sink = jnp.where(lane < BLOCK, sink_ref[2 * p], sink_ref[2 * p + 1]) * LOG2E
            m = jnp.maximum(jnp.max(logits, axis=0, keepdims=True), sink)
            p_scr[x, p] = jnp.exp2(logits - m).astype(BF16)
            sink_terms.append(jnp.exp2(sink - m))
        for p in range(PAIRS):
            g = (2 * p) // GROUP
            ot = jnp.dot(vt_scr[g * VT_ROWS:(g + 1) * VT_ROWS, j * BLOCK:(j + 3) * BLOCK], p_scr[x, p],
                         preferred_element_type=F32)
            out = ot[0:HEAD_DIM] / (ot[HEAD_DIM:HEAD_DIM + 1] + sink_terms[p])
            for u in range(2):
                h = 2 * p + u
                ot_scr[h * HEAD_DIM:(h + 1) * HEAD_DIM, j * BLOCK:(j + 1) * BLOCK] = out[:, u * BLOCK:(u + 1) * BLOCK]
    o_ref[...] = ot_scr[...].T.astype(o_ref.dtype)


def _win_attn(qa, ka, va, sink, seq_len, tq):
    t = qa.shape[0]
    sub = tq // BLOCK
    nblk = t // BLOCK
    win = tq + 2 * BLOCK
    row = lambda i: (i, 0)
    prev = lambda i: (jnp.maximum(i * sub - 1, 0), 0)
    nxt = lambda i: (jnp.minimum((i + 1) * sub, nblk - 1), 0)
    kv_specs = [pl.BlockSpec((tq, KV_W), row), pl.BlockSpec((BLOCK, KV_W), prev), pl.BlockSpec((BLOCK, KV_W), nxt)]
    bias_spec = pl.BlockSpec((N_EDGE, PAIRS, 3 * BLOCK, 2 * BLOCK), lambda i: (0, 0, 0, 0),
                             pipeline_mode=pl.Buffered(1))
    return pl.pallas_call(
        functools.partial(_win_attn_kernel, blocks_per_seq=seq_len // BLOCK, sub=sub),
        grid=(t // tq,),
        in_specs=[pl.BlockSpec(memory_space=pltpu.SMEM), pl.BlockSpec((tq, Q_W), row)] + kv_specs + kv_specs
                 + [bias_spec],
        out_specs=pl.BlockSpec((tq, Q_W), row),
        out_shape=jax.ShapeDtypeStruct((t, Q_W), BF16),
        scratch_shapes=[pltpu.VMEM((KV_HEADS, win, HEAD_DIM), BF16), pltpu.VMEM((KV_HEADS * VT_ROWS, win), BF16),
                        pltpu.VMEM((sub, PAIRS, HEAD_DIM, 2 * BLOCK), BF16),
                        pltpu.VMEM((2, PAIRS, 3 * BLOCK, 2 * BLOCK), F32),
                        pltpu.VMEM((2, PAIRS, 3 * BLOCK, 2 * BLOCK), BF16), pltpu.VMEM((Q_W, tq), F32)],
        compiler_params=_params(("parallel",)),
        name="win_attn",
    )(sink.astype(F32), qa, ka, ka, ka, va, va, va, _win_bias())


def _glob_attn_kernel(q_ref, k_ref, v_ref, o_ref, kh_ref, vt_ref, qt_scr, s_scr, p_scr, cm_scr, m_scr, a_scr, acc_scr,
                      ot_scr, *, tk):
    tq = q_ref.shape[1]
    nk = k_ref.shape[1] // tk

    @pl.when(pl.program_id(1) == 0)
    def _():
        ones = jnp.ones((ONES_ROWS, tk), BF16)

        def fill(c, carry):
            off = pl.multiple_of(c * tk, tk)
            vt = v_ref[0, pl.ds(off, tk), :].astype(F32).T.astype(BF16)
            kc = k_ref[0, pl.ds(off, tk), :]
            for g in range(KV_HEADS):
                kh_ref[g, pl.ds(off, tk), :] = kc[:, g * HEAD_DIM:(g + 1) * HEAD_DIM]
                vt_ref[g * VT_ROWS:g * VT_ROWS + HEAD_DIM, pl.ds(off, tk)] = vt[g * HEAD_DIM:(g + 1) * HEAD_DIM, :]
                vt_ref[g * VT_ROWS + HEAD_DIM:(g + 1) * VT_ROWS, pl.ds(off, tk)] = ones
            return carry

        lax.fori_loop(0, nk, fill, 0)

    qt_scr[...] = q_ref[0].astype(F32).T.astype(BF16)

    for g in range(KV_HEADS):
        heads = range(g * GROUP, (g + 1) * GROUP)

        def stage_a(c, x, heads=heads, g=g):
            kc = kh_ref[g, pl.ds(pl.multiple_of(c * tk, tk), tk), :]
            for i, h in enumerate(heads):
                s = jnp.dot(kc, qt_scr[h * HEAD_DIM:(h + 1) * HEAD_DIM, :], preferred_element_type=F32)
                s_scr[x, i] = s.astype(s_scr.dtype)
                cm_scr[x, i] = jnp.max(s, axis=0, keepdims=True)

        def stage_b(x):
            for i in range(GROUP):
                m = m_scr[i]
                m_new = jnp.maximum(m, cm_scr[x, i])
                a_scr[x, i] = jnp.exp2(m - m_new)
                m_scr[i] = m_new
                p_scr[x, i] = jnp.exp2(s_scr[x, i] - m_new.astype(s_scr.dtype)).astype(BF16)

        def stage_c(c, x, g=g):
            vt = vt_ref[g * VT_ROWS:(g + 1) * VT_ROWS, pl.ds(pl.multiple_of(c * tk, tk), tk)]
            for i in range(GROUP):
                acc_scr[i] = a_scr[x, i] * acc_scr[i] + jnp.dot(vt, p_scr[x, i], preferred_element_type=F32)

        def step(c, x, last=False):
            stage_c(c - 1, 1 - x)
            stage_b(x)
            if not last:
                stage_a(c + 1, 1 - x)

        for i in range(GROUP):
            m_scr[i] = jnp.full((1, tq), -jnp.inf, F32)
            acc_scr[i] = jnp.zeros((VT_ROWS, tq), F32)
        stage_a(0, 0)
        stage_b(0)
        stage_a(1, 1)

        def pair(j, carry):
            step(2 * j + 1, 1)
            step(2 * j + 2, 0)
            return carry

        lax.fori_loop(0, (nk - 2) // 2, pair, 0, unroll=4)
        step(nk - 1, 1, last=True)
        stage_c(nk - 1, 1)
        for i, h in enumerate(heads):
            acc = acc_scr[i]
            ot_scr[h * HEAD_DIM:(h + 1) * HEAD_DIM, :] = acc[0:HEAD_DIM] / acc[HEAD_DIM:HEAD_DIM + 1]
    o_ref[0] = ot_scr[...].T.astype(o_ref.dtype)


def _glob_attn(qb, kb, vb, tq, tk):
    b, s, _ = qb.shape
    assert s % (2 * tk) == 0, "the chunk pipeline walks key chunks in pairs"
    return pl.pallas_call(
        functools.partial(_glob_attn_kernel, tk=tk),
        grid=(b, s // tq),
        in_specs=[pl.BlockSpec((1, tq, Q_W), lambda bi, qi: (bi, qi, 0)),
                  pl.BlockSpec((1, s, KV_W), lambda bi, qi: (bi, 0, 0)),
                  pl.BlockSpec((1, s, KV_W), lambda bi, qi: (bi, 0, 0))],
        out_specs=pl.BlockSpec((1, tq, Q_W), lambda bi, qi: (bi, qi, 0)),
        out_shape=jax.ShapeDtypeStruct((b, s, Q_W), BF16),
        scratch_shapes=[pltpu.VMEM((KV_HEADS, s, HEAD_DIM), BF16), pltpu.VMEM((KV_HEADS * VT_ROWS, s), BF16),
                        pltpu.VMEM((Q_W, tq), BF16),
                        pltpu.VMEM((2, GROUP, tk, tq), SCORE_DTYPE), pltpu.VMEM((2, GROUP, tk, tq), BF16),
                        pltpu.VMEM((2, GROUP, 1, tq), F32), pltpu.VMEM((GROUP, 1, tq), F32),
                        pltpu.VMEM((2, GROUP, 1, tq), F32), pltpu.VMEM((GROUP, VT_ROWS, tq), F32),
                        pltpu.VMEM((Q_W, tq), F32)],
        compiler_params=_params(("arbitrary", "arbitrary")),
        name="glob_attn",
    )(qb, kb, vb)


def _first_argmax(vals, idx, big):
    m = jnp.max(vals, axis=-1, keepdims=True)
    return m, jnp.min(jnp.where(vals == m, idx, big), axis=-1, keepdims=True)


def _route(logits):
    rows = logits.shape[0]
    gl = logits[:, 0:N_GROUPS]
    gidx = lax.broadcasted_iota(jnp.int32, (rows, N_GROUPS), 1)
    gmax, g_sel = _first_argmax(gl, gidx, N_GROUPS)
    g_w = 1.0 / jnp.sum(jnp.exp(gl - gmax), axis=-1, keepdims=True)
    e_sel = jnp.zeros((rows, EXPERTS_PER_GROUP), F32)
    for g in range(N_GROUPS):
        lo = N_GROUPS + g * EXPERTS_PER_GROUP
        e_sel = jnp.where(g_sel == g, logits[:, lo:lo + EXPERTS_PER_GROUP], e_sel)
    eidx = lax.broadcasted_iota(jnp.int32, (rows, EXPERTS_PER_GROUP), 1)
    ex = jnp.exp(e_sel - jnp.max(e_sel, axis=-1, keepdims=True))
    e_prob = ex / jnp.sum(ex, axis=-1, keepdims=True)
    p1, i1 = _first_argmax(e_prob, eidx, EXPERTS_PER_GROUP)
    rest = jnp.where(eidx == i1, -1.0, e_prob)
    p2, i2 = _first_argmax(rest, eidx, EXPERTS_PER_GROUP)
    tot = p1 + p2
    lane = lax.broadcasted_iota(jnp.int32, (rows, LANES), 1)
    out = jnp.where(lane == g_sel, 1.0, 0.0)
    out = out + jnp.where(lane == N_GROUPS + i1, g_w * (p1 / tot), 0.0)
    return out + jnp.where(lane == N_GROUPS + i2, g_w * (p2 / tot), 0.0)


def _merge_kernel(x_ref, ya_ref, yb_ref, ga_ref, gb_ref, woa_ref, wob_ref, wout_ref, gf_ref, wr_ref, br_ref,
                  x1_ref, route_ref, routet_ref, cnt_ref):
    tm = x_ref.shape[0]
    half = tm // 2
    cnt = jnp.zeros((1, LANES), F32)
    for r in (0, half):
        rows = slice(r, r + half)
        a = jnp.dot(ya_ref[rows, :], woa_ref[...], preferred_element_type=F32)
        b = jnp.dot(yb_ref[rows, :], wob_ref[...], preferred_element_type=F32)
        merged = ga_ref[rows, :].astype(F32) * a + gb_ref[rows, :].astype(F32) * b
        x1 = x_ref[rows, :] + jnp.dot(merged.astype(BF16), wout_ref[...], preferred_element_type=F32)
        x1_ref[rows, :] = x1
        t = _rmsnorm(x1, gf_ref[...])
        t_hi, t_lo = _hi_lo(t)
        big = jnp.dot(t_hi, wr_ref[...], preferred_element_type=F32)
        small = jnp.dot(t_lo, wr_ref[:, 0:LANES], preferred_element_type=F32)
        logits = big[:, 0:LANES] + big[:, LANES:2 * LANES] + small + br_ref[...]
        route = _route(logits)
        route_ref[rows, :] = route
        routet_ref[:, rows] = route.T
        cnt = cnt + jnp.sum(route, axis=0, keepdims=True)
    cnt_ref[0] = jnp.broadcast_to(cnt, cnt_ref.shape[1:]).astype(jnp.int32)


def _merge(x2, ya, yb, ga, gb, w_oa, w_ob, w_out, ffn_g, w_rg, b_rg, w_re, b_re, tm):
    t, d = x2.shape
    wr = jnp.zeros((d, LANES), F32).at[:, :N_GROUPS].set(w_rg).at[:, N_GROUPS:N_GROUPS + N_EXPERTS].set(w_re)
    wr_hi, wr_lo = _hi_lo(wr)
    br = jnp.zeros((1, LANES), F32).at[0, :N_GROUPS].set(b_rg).at[0, N_GROUPS:N_GROUPS + N_EXPERTS].set(b_re)
    row = lambda i: (i, 0)
    return pl.pallas_call(
        _merge_kernel,
        grid=(t // tm,),
        in_specs=[pl.BlockSpec((tm, d), row), pl.BlockSpec((tm, Q_W), row), pl.BlockSpec((tm, Q_W), row),
                  pl.BlockSpec((tm, d), row), pl.BlockSpec((tm, d), row),
                  _full((Q_W, d)), _full((Q_W, d)), _full((d, d)), _full((1, d)), _full((d, 2 * LANES)),
                  _full((1, LANES))],
        out_specs=[pl.BlockSpec((tm, d), row), pl.BlockSpec((tm, LANES), row),
                   pl.BlockSpec((LANES, tm), lambda i: (0, i)), pl.BlockSpec((1, 8, LANES), lambda i: (i, 0, 0))],
        out_shape=[jax.ShapeDtypeStruct((t, d), F32), jax.ShapeDtypeStruct((t, LANES), F32),
                   jax.ShapeDtypeStruct((LANES, t), F32), jax.ShapeDtypeStruct((t // tm, 8, LANES), jnp.int32)],
        compiler_params=_params(("parallel",)),
        name="merge",
    )(x2, ya, yb, ga, gb, w_oa.astype(BF16), w_ob.astype(BF16), w_out.astype(BF16), ffn_g.reshape(1, d),
      jnp.concatenate([wr_hi, wr_lo], axis=1), br)


ROW_ALIGN = 128
BIG_BLOCK = 2 * ROW_ALIGN
GATHER_BLOCK = 4 * ROW_ALIGN
MAX_WHOLE = 4


def _rmsnorm(x, g):
    ms = jnp.mean(x * x, axis=-1, keepdims=True)
    return x * lax.rsqrt(ms + EPS) * g


def _moe_kernel(cnt_ref, x1_ref, route_ref, routet_ref, gf_ref, wg_ref, wu_ref, wd_ref, gfin_ref, o_ref,
                p_scr, q_scr, ts_scr, ws_scr):
    i = pl.program_id(0)
    g = pl.program_id(1)
    tm = x1_ref.shape[0]
    rows = p_scr.shape[0]
    offs, off = [], 0
    for gg in range(N_GROUPS):
        offs.append(off)
        off = off + (cnt_ref[i, gg] + (ROW_ALIGN - 1)) // ROW_ALIGN * ROW_ALIGN

    @pl.when(g == 0)
    def _():
        route = route_ref[...]
        lane = lax.broadcasted_iota(jnp.int32, (tm, LANES), 1)
        onehot = jnp.where(lane < N_GROUPS, route, 0.0)
        r_i = lax.broadcasted_iota(jnp.int32, (tm, tm), 0)
        c_i = lax.broadcasted_iota(jnp.int32, (tm, tm), 1)
        before = jnp.dot(jnp.where(c_i < r_i, 1.0, 0.0).astype(BF16), onehot.astype(BF16),
                         preferred_element_type=F32)
        goff = jnp.zeros((tm, LANES), F32)
        for gg in range(N_GROUPS):
            goff = jnp.where(lane == gg, jnp.asarray(offs[gg], jnp.int32).astype(F32), goff)
        dest_col = jnp.sum((before + goff) * onehot, axis=1, keepdims=True).astype(jnp.int32)
        q_scr[...] = jnp.where(dest_col == lax.broadcasted_iota(jnp.int32, (tm, rows), 1), 1.0, 0.0).astype(BF16)
        sub = lax.broadcasted_iota(jnp.int32, (16, tm), 0)
        onehot_t = jnp.where(sub < N_GROUPS, routet_ref[0:16, :], 0.0)
        before_t = jnp.dot(onehot_t.astype(BF16), jnp.where(r_i < c_i, 1.0, 0.0).astype(BF16),
                           preferred_element_type=F32)
        goff_t = jnp.zeros((16, tm), F32)
        for gg in range(N_GROUPS):
            goff_t = jnp.where(sub == gg, jnp.asarray(offs[gg], jnp.int32).astype(F32), goff_t)
        dest_row = jnp.sum((before_t + goff_t) * onehot_t, axis=0, keepdims=True).astype(jnp.int32)
        for r0 in range(0, rows, BIG_BLOCK):
            rid = lax.broadcasted_iota(jnp.int32, (BIG_BLOCK, tm), 0) + r0
            p_scr[r0:r0 + BIG_BLOCK, :] = jnp.where(dest_row == rid, 1.0, 0.0).astype(BF16)
        t = _rmsnorm(x1_ref[...], gf_ref[...]).astype(BF16)
        rh, rl = _hi_lo(route)
        for r0 in range(0, rows, GATHER_BLOCK):
            pb = p_scr[r0:r0 + GATHER_BLOCK, :]
            ts_scr[r0:r0 + GATHER_BLOCK, :] = jnp.dot(pb, t, preferred_element_type=F32).astype(BF16)
            ws_scr[r0:r0 + GATHER_BLOCK, :] = (jnp.dot(pb, rh, preferred_element_type=F32)
                                               + jnp.dot(pb, rl, preferred_element_type=F32))

    def ffn(r, m):
        tb = ts_scr[pl.ds(pl.multiple_of(r, ROW_ALIGN), m), :]
        wb = ws_scr[pl.ds(pl.multiple_of(r, ROW_ALIGN), m), :]
        y = jnp.zeros((m, o_ref.shape[1]), F32)
        for e in range(EXPERTS_PER_GROUP):
            a = jax.nn.silu(jnp.dot(tb, wg_ref[0, e], preferred_element_type=F32)) * jnp.dot(
                tb, wu_ref[0, e], preferred_element_type=F32)
            y = y + wb[:, N_GROUPS + e:N_GROUPS + e + 1] * jnp.dot(a.astype(BF16), wd_ref[0, e],
                                                                   preferred_element_type=F32)
        ts_scr[pl.ds(pl.multiple_of(r, ROW_ALIGN), m), :] = y.astype(BF16)

    start = offs[0]
    for gg in range(1, N_GROUPS):
        start = jnp.where(g == gg, offs[gg], start)
    n_small = (cnt_ref[i, g] + (ROW_ALIGN - 1)) // ROW_ALIGN

    for units in range(1, MAX_WHOLE + 1):
        @pl.when(n_small == units)
        def _(units=units):
            ffn(start, units * ROW_ALIGN)

    @pl.when(n_small > MAX_WHOLE)
    def _():
        n_big = n_small // 2

        def big(j, carry):
            ffn(start + j * BIG_BLOCK, BIG_BLOCK)
            return carry

        lax.fori_loop(0, n_big, big, 0)

        @pl.when(n_small % 2 == 1)
        def _():
            ffn(start + n_big * BIG_BLOCK, ROW_ALIGN)

    @pl.when(g == N_GROUPS - 1)
    def _():
        x2 = x1_ref[...] + jnp.dot(q_scr[...], ts_scr[...], preferred_element_type=F32)
        o_ref[...] = _rmsnorm(x2, gfin_ref[...])


def _moe(x1, route, routet, cnt, ffn_g, w_eg, w_eu, w_ed, final_g, tm):
    t, d = x1.shape
    de = w_eg.shape[-1]
    rows = tm + N_GROUPS * ROW_ALIGN
    grouped = lambda w: w.astype(BF16).reshape((N_GROUPS, EXPERTS_PER_GROUP) + w.shape[1:])
    row = lambda i, g, c: (i, 0)
    const = lambda i, g, c: (0, 0)
    wspec = lambda a, b: pl.BlockSpec((1, EXPERTS_PER_GROUP, a, b), lambda i, g, c: (g, 0, 0, 0))
    return pl.pallas_call(
        _moe_kernel,
        grid_spec=pltpu.PrefetchScalarGridSpec(
            num_scalar_prefetch=1,
            grid=(t // tm, N_GROUPS),
            in_specs=[pl.BlockSpec((tm, d), row), pl.BlockSpec((tm, LANES), row),
                      pl.BlockSpec((LANES, tm), lambda i, g, c: (0, i)), pl.BlockSpec((1, d), const),
                      wspec(d, de), wspec(d, de), wspec(de, d), pl.BlockSpec((1, d), const)],
            out_specs=pl.BlockSpec((tm, d), row),
            scratch_shapes=[pltpu.VMEM((rows, tm), BF16), pltpu.VMEM((tm, rows), BF16), pltpu.VMEM((rows, d), BF16),
                            pltpu.VMEM((rows, LANES), F32)]),
        out_shape=jax.ShapeDtypeStruct((t, d), F32),
        compiler_params=_params(("arbitrary", "arbitrary")),
        name="moe",
    )(cnt, x1, route, routet, ffn_g.reshape(1, d), grouped(w_eg), grouped(w_eu), grouped(w_ed),
      final_g.reshape(1, d))


def _tile(n, want):
    while n % want:
        want //= 2
    return want


def _trunk(x, attn_g, w_in, sink, gq, gk, w_oa, w_ob, w_out, ffn_g, w_rg, b_rg, w_re, b_re, w_eg, w_eu, w_ed,
           final_g):
    b, s, d = x.shape
    x2 = x.reshape(b * s, d)
    qa, ka, va, qb, kb, vb, ga, gb = _in_proj(x2, s, attn_g, w_in, gq, gk, _tile(s, 1024))
    ya = _win_attn(qa, ka, va, sink, s, _tile(s, 512))
    yb = _glob_attn(qb.reshape(b, s, Q_W), kb.reshape(b, s, KV_W), vb.reshape(b, s, KV_W),
                    256, 256).reshape(b * s, Q_W)
    tm_merge, tm_moe = _tile(b * s, 1024), _tile(b * s, 1024)
    x1, route, routet, cnt = _merge(x2, ya, yb, ga, gb, w_oa, w_ob, w_out, ffn_g, w_rg, b_rg, w_re, b_re, tm_merge)
    cnt = cnt[:, 0, :N_GROUPS].reshape(b * s // tm_moe, tm_moe // tm_merge, N_GROUPS).sum(axis=1)
    y = _moe(x1, route, routet, cnt, ffn_g, w_eg, w_eu, w_ed, final_g, tm_moe)
    return y.reshape(b, s, d)


def kernel(x_prompt, x_sample, attn_norm_g, w_in, a_sink, b_q_norm_g, b_k_norm_g, w_oa, w_ob, w_out, ffn_norm_g,
           w_router_group, b_router_group, w_router_expert, b_router_expert, w_expert_gate, w_expert_up,
           w_expert_down, final_norm_g):
    assert attn_norm_g.shape[0] == 1, "single-layer trunk"
    weights = (attn_norm_g[0], w_in[0], a_sink[0], b_q_norm_g[0], b_k_norm_g[0], w_oa[0], w_ob[0], w_out[0],
               ffn_norm_g[0], w_router_group[0], b_router_group[0], w_router_expert[0], b_router_expert[0],
               w_expert_gate[0], w_expert_up[0], w_expert_down[0], final_norm_g)
    return (_trunk(x_prompt, *weights), _trunk(x_sample, *weights))
```

```python
import functools

import jax
import jax.numpy as jnp
import numpy as np
from jax import lax
from jax.experimental import pallas as pl
from jax.experimental.pallas import tpu as pltpu

HEAD_DIM = 64
Q_HEADS = 8
KV_HEADS = 2
GROUP = Q_HEADS // KV_HEADS
Q_W = Q_HEADS * HEAD_DIM
KV_W = KV_HEADS * HEAD_DIM
WINDOW = 128
BLOCK = 128
GRID_W = 64
ROPE_THETA = 10000.0
N_GROUPS = 4
EXPERTS_PER_GROUP = 4
N_EXPERTS = N_GROUPS * EXPERTS_PER_GROUP
EPS = 1e-6
NEG_INF = -1e30
SCALE = HEAD_DIM ** -0.5
LOG2E = 1.4426950408889634
ONES_ROWS = 16
VT_ROWS = HEAD_DIM + ONES_ROWS
SCORE_DTYPE = jnp.bfloat16
LANES = 128

VMEM_LIMIT = 56 * 1024 * 1024

F32 = jnp.float32
BF16 = jnp.bfloat16


def _params(sem):
    return pltpu.CompilerParams(dimension_semantics=sem, vmem_limit_bytes=VMEM_LIMIT)


def _full(shape):
    return pl.BlockSpec(shape, lambda *_: (0,) * len(shape))


def _hi_lo(v):
    top = lax.bitcast_convert_type(lax.bitcast_convert_type(v, jnp.uint32) & jnp.uint32(0xFFFF0000), F32)
    return top.astype(BF16), (v - top).astype(BF16)


def _split_dot(y, ones_bd):
    hi, lo = _hi_lo(y)
    return jnp.dot(hi, ones_bd, preferred_element_type=F32) + jnp.dot(lo, ones_bd, preferred_element_type=F32)


def _pair_swap(y):
    n = y.shape[-1]
    lane = lax.broadcasted_iota(jnp.int32, y.shape, y.ndim - 1)
    nxt = pltpu.roll(y, n - 1, y.ndim - 1)
    prv = pltpu.roll(y, 1, y.ndim - 1)
    return jnp.where((lane & 1) == 0, nxt, prv)


def _norm_rope(y, gain, ones_bd, cos, sin):
    ms = _split_dot(y * y, ones_bd) * (1.0 / HEAD_DIM)
    yn = y * lax.rsqrt(ms + EPS) * gain
    return yn * cos + _pair_swap(yn) * sin


def _in_proj_kernel(x_ref, g_ref, w_ref, cos_ref, sin_ref, gq_ref, gk_ref, oq_ref, ok_ref,
                    qa_ref, ka_ref, va_ref, qb_ref, kb_ref, vb_ref, ga_ref, gb_ref):
    half = x_ref.shape[0] // 2
    d = x_ref.shape[1]
    for r in (0, half):
        rows = slice(r, r + half)
        x = x_ref[rows, :]
        ms = jnp.mean(x * x, axis=-1, keepdims=True)
        h = (x * lax.rsqrt(ms + EPS) * g_ref[...]).astype(BF16)

        def proj(lo, width, h=h):
            return jnp.dot(h, w_ref[:, lo:lo + width], preferred_element_type=F32)

        o = 0
        qa_ref[rows, :] = (proj(o, Q_W) * (SCALE * LOG2E)).astype(BF16); o += Q_W
        ka_ref[rows, :] = proj(o, KV_W).astype(BF16); o += KV_W
        va_ref[rows, :] = proj(o, KV_W).astype(BF16); o += KV_W
        cos = cos_ref[rows, :]
        sin = sin_ref[rows, :]
        qb = proj(o, Q_W); o += Q_W
        cos_q = jnp.concatenate([cos] * (Q_W // LANES), axis=1)
        sin_q = jnp.concatenate([sin] * (Q_W // LANES), axis=1)
        qb_ref[rows, :] = (_norm_rope(qb, gq_ref[...], oq_ref[...], cos_q, sin_q) * (SCALE * LOG2E)).astype(BF16)
        kb = proj(o, KV_W); o += KV_W
        kb_ref[rows, :] = _norm_rope(kb, gk_ref[...], ok_ref[...], cos, sin).astype(BF16)
        vb_ref[rows, :] = proj(o, KV_W).astype(BF16); o += KV_W
        ga_ref[rows, :] = jax.nn.sigmoid(proj(o, d)).astype(BF16); o += d
        gb_ref[rows, :] = jax.nn.sigmoid(proj(o, d)).astype(BF16)


def _rope_tables(seq_len):
    rows = seq_len // GRID_W
    row = jnp.repeat(jnp.arange(rows, dtype=F32), GRID_W)
    col = jnp.tile(jnp.arange(GRID_W, dtype=F32), rows)
    half = HEAD_DIM // 2
    inv = ROPE_THETA ** (-jnp.arange(0, half, 2, dtype=F32) / half)
    ang = jnp.concatenate([row[:, None] * inv, col[:, None] * inv], axis=-1)
    cos = jnp.repeat(jnp.cos(ang), 2, axis=-1)
    sin = jnp.repeat(jnp.sin(ang), 2, axis=-1) * jnp.tile(jnp.array([-1.0, 1.0], F32), half)
    reps = LANES // HEAD_DIM
    return jnp.tile(cos, (1, reps)), jnp.tile(sin, (1, reps))


def _block_ones(width):
    idx = np.arange(width) // HEAD_DIM
    return jnp.asarray(idx[:, None] == idx[None, :], dtype=BF16)


def _in_proj(x2, seq_len, attn_g, w_in, gq, gk, tm):
    t, d = x2.shape
    in_w = w_in.shape[1]
    cos, sin = _rope_tables(seq_len)
    nseq = seq_len // tm
    row = lambda i: (i, 0)
    pos = lambda i: (i % nseq, 0)
    outs = [(Q_W, BF16), (KV_W, BF16), (KV_W, BF16), (Q_W, BF16), (KV_W, BF16), (KV_W, BF16), (d, BF16), (d, BF16)]
    return pl.pallas_call(
        _in_proj_kernel,
        grid=(t // tm,),
        in_specs=[pl.BlockSpec((tm, d), row), _full((1, d)), _full((d, in_w)),
                  pl.BlockSpec((tm, LANES), pos), pl.BlockSpec((tm, LANES), pos),
                  _full((1, Q_W)), _full((1, KV_W)), _full((Q_W, Q_W)), _full((KV_W, KV_W))],
        out_specs=[pl.BlockSpec((tm, w), row) for w, _ in outs],
        out_shape=[jax.ShapeDtypeStruct((t, w), dt) for w, dt in outs],
        compiler_params=_params(("parallel",)),
        name="in_proj",
    )(x2, attn_g.reshape(1, d), w_in.astype(BF16), cos, sin,
      jnp.tile(gq.astype(F32), Q_HEADS).reshape(1, Q_W), jnp.tile(gk.astype(F32), KV_HEADS).reshape(1, KV_W),
      _block_ones(Q_W), _block_ones(KV_W))


PAIRS = Q_HEADS // 2
N_EDGE = 4


def _win_bias():
    slopes = np.exp2(-8.0 * np.arange(1, Q_HEADS + 1, dtype=np.float32) / Q_HEADS).astype(np.float32)
    key = np.arange(3 * BLOCK)[:, None]
    dist = np.abs(np.arange(BLOCK)[None, :] - (key - BLOCK))
    tabs = []
    for edge in range(N_EDGE):
        alive = dist <= WINDOW
        if edge & 1:
            alive = alive & (key >= BLOCK)
        if edge & 2:
            alive = alive & (key < 2 * BLOCK)
        heads = [np.where(alive, -slopes[h] * np.float32(LOG2E) * dist.astype(np.float32), np.float32(NEG_INF))
                 for h in range(Q_HEADS)]
        tabs.append(np.stack([np.concatenate([heads[2 * p], heads[2 * p + 1]], axis=1) for p in range(PAIRS)]))
    return jnp.asarray(np.stack(tabs), dtype=F32)


def _win_attn_kernel(sink_ref, q_ref, kc_ref, kp_ref, kn_ref, vc_ref, vp_ref, vn_ref, bias_ref, o_ref,
                     kh_scr, vt_scr, qp_scr, s_scr, p_scr, ot_scr, *, blocks_per_seq, sub):
    i = pl.program_id(0)
    tq = q_ref.shape[0]
    ones = jnp.ones((ONES_ROWS, BLOCK), BF16)
    for off, n, kr, vr in ((0, BLOCK, kp_ref, vp_ref), (BLOCK, tq, kc_ref, vc_ref), (BLOCK + tq, BLOCK, kn_ref, vn_ref)):
        k = kr[...]
        vt = vr[...].astype(F32).T.astype(BF16)
        for g in range(KV_HEADS):
            kh_scr[g, off:off + n, :] = k[:, g * HEAD_DIM:(g + 1) * HEAD_DIM]
            vt_scr[g * VT_ROWS:g * VT_ROWS + HEAD_DIM, off:off + n] = vt[g * HEAD_DIM:(g + 1) * HEAD_DIM, :]
            for o in range(off, off + n, BLOCK):
                vt_scr[g * VT_ROWS + HEAD_DIM:(g + 1) * VT_ROWS, o:o + BLOCK] = ones
    qt = q_ref[...].astype(F32).T.astype(BF16)
    for j in range(sub):
        for p in range(PAIRS):
            for u in range(2):
                h = 2 * p + u
                qp_scr[j, p, :, u * BLOCK:(u + 1) * BLOCK] = qt[h * HEAD_DIM:(h + 1) * HEAD_DIM,
                                                                j * BLOCK:(j + 1) * BLOCK]
    lane = lax.broadcasted_iota(jnp.int32, (1, 2 * BLOCK), 1)
    for j in range(sub):
        x = j % 2
        blk = (i * sub + j) % blocks_per_seq
        edge = (blk == 0).astype(jnp.int32) + 2 * (blk == blocks_per_seq - 1).astype(jnp.int32)
        for p in range(PAIRS):
            g = (2 * p) // GROUP
            s = jnp.dot(kh_scr[g, j * BLOCK:(j + 3) * BLOCK, :], qp_scr[j, p], preferred_element_type=F32)
            s_scr[x, p] = s + bias_ref[edge, p]
        sink_terms = []
        for p in range(PAIRS):
            logits = s_scr[x, p]
            sink = jnp.where(lane < BLOCK, sink_ref[2 * p], sink_ref[2 * p + 1]) * LOG2E
            m = jnp.maximum(jnp.max(logits, axis=0, keepdims=True), sink)
            p_scr[x, p] = jnp.exp2(logits - m).astype(BF16)
            sink_terms.append(jnp.exp2(sink - m))
        for p in range(PAIRS):
            g = (2 * p) // GROUP
            ot = jnp.dot(vt_scr[g * VT_ROWS:(g + 1) * VT_ROWS, j * BLOCK:(j + 3) * BLOCK], p_scr[x, p],
                         preferred_element_type=F32)
            out = ot[0:HEAD_DIM] / (ot[HEAD_DIM:HEAD_DIM + 1] + sink_terms[p])
            for u in range(2):
                h = 2 * p + u
                ot_scr[h * HEAD_DIM:(h + 1) * HEAD_DIM, j * BLOCK:(j + 1) * BLOCK] = out[:, u * BLOCK:(u + 1) * BLOCK]
    o_ref[...] = ot_scr[...].T.astype(o_ref.dtype)


def _win_attn(qa, ka, va, sink, seq_len, tq):
    t = qa.shape[0]
    sub = tq // BLOCK
    nblk = t // BLOCK
    win = tq + 2 * BLOCK
    row = lambda i: (i, 0)
    prev = lambda i: (jnp.maximum(i * sub - 1, 0), 0)
    nxt = lambda i: (jnp.minimum((i + 1) * sub, nblk - 1), 0)
    kv_specs = [pl.BlockSpec((tq, KV_W), row), pl.BlockSpec((BLOCK, KV_W), prev), pl.BlockSpec((BLOCK, KV_W), nxt)]
    bias_spec = pl.BlockSpec((N_EDGE, PAIRS, 3 * BLOCK, 2 * BLOCK), lambda i: (0, 0, 0, 0),
                             pipeline_mode=pl.Buffered(1))
    return pl.pallas_call(
        functools.partial(_win_attn_kernel, blocks_per_seq=seq_len // BLOCK, sub=sub),
        grid=(t // tq,),
        in_specs=[pl.BlockSpec(memory_space=pltpu.SMEM), pl.BlockSpec((tq, Q_W), row)] + kv_specs + kv_specs
                 + [bias_spec],
        out_specs=pl.BlockSpec((tq, Q_W), row),
        out_shape=jax.ShapeDtypeStruct((t, Q_W), BF16),
        scratch_shapes=[pltpu.VMEM((KV_HEADS, win, HEAD_DIM), BF16), pltpu.VMEM((KV_HEADS * VT_ROWS, win), BF16),
                        pltpu.VMEM((sub, PAIRS, HEAD_DIM, 2 * BLOCK), BF16),
                        pltpu.VMEM((2, PAIRS, 3 * BLOCK, 2 * BLOCK), F32),
                        pltpu.VMEM((2, PAIRS, 3 * BLOCK, 2 * BLOCK), BF16), pltpu.VMEM((Q_W, tq), F32)],
        compiler_params=_params(("parallel",)),
        name="win_attn",
    )(sink.astype(F32), qa, ka, ka, ka, va, va, va, _win_bias())


def _glob_attn_kernel(q_ref, k_ref, v_ref, o_ref, kh_ref, vt_ref, qt_scr, s_scr, p_scr, cm_scr, m_scr, a_scr, acc_scr,
                      ot_scr, *, tk):
    tq = q_ref.shape[1]
    nk = k_ref.shape[1] // tk

    @pl.when(pl.program_id(1) == 0)
    def _():
        ones = jnp.ones((ONES_ROWS, tk), BF16)

        def fill(c, carry):
            off = pl.multiple_of(c * tk, tk)
            vt = v_ref[0, pl.ds(off, tk), :].astype(F32).T.astype(BF16)
            kc = k_ref[0, pl.ds(off, tk), :]
            for g in range(KV_HEADS):
                kh_ref[g, pl.ds(off, tk), :] = kc[:, g * HEAD_DIM:(g + 1) * HEAD_DIM]
                vt_ref[g * VT_ROWS:g * VT_ROWS + HEAD_DIM, pl.ds(off, tk)] = vt[g * HEAD_DIM:(g + 1) * HEAD_DIM, :]
                vt_ref[g * VT_ROWS + HEAD_DIM:(g + 1) * VT_ROWS, pl.ds(off, tk)] = ones
            return carry

        lax.fori_loop(0, nk, fill, 0)

    qt_scr[...] = q_ref[0].astype(F32).T.astype(BF16)

    for g in range(KV_HEADS):
        heads = range(g * GROUP, (g + 1) * GROUP)

        def stage_a(c, x, heads=heads, g=g):
            kc = kh_ref[g, pl.ds(pl.multiple_of(c * tk, tk), tk), :]
            for i, h in enumerate(heads):
                s = jnp.dot(kc, qt_scr[h * HEAD_DIM:(h + 1) * HEAD_DIM, :], preferred_element_type=F32)
                s_scr[x, i] = s.astype(s_scr.dtype)
                cm_scr[x, i] = jnp.max(s, axis=0, keepdims=True)

        def stage_b(x):
            for i in range(GROUP):
                m = m_scr[i]
                m_new = jnp.maximum(m, cm_scr[x, i])
                a_scr[x, i] = jnp.exp2(m - m_new)
                m_scr[i] = m_new
                p_scr[x, i] = jnp.exp2(s_scr[x, i] - m_new.astype(s_scr.dtype)).astype(BF16)

        def stage_c(c, x, g=g):
            vt = vt_ref[g * VT_ROWS:(g + 1) * VT_ROWS, pl.ds(pl.multiple_of(c * tk, tk), tk)]
            for i in range(GROUP):
                acc_scr[i] = a_scr[x, i] * acc_scr[i] + jnp.dot(vt, p_scr[x, i], preferred_element_type=F32)

        def step(c, x, last=False):
            stage_c(c - 1, 1 - x)
            stage_b(x)
            if not last:
                stage_a(c + 1, 1 - x)

        for i in range(GROUP):
            m_scr[i] = jnp.full((1, tq), -jnp.inf, F32)
            acc_scr[i] = jnp.zeros((VT_ROWS, tq), F32)
        stage_a(0, 0)
        stage_b(0)
        stage_a(1, 1)

        def pair(j, carry):
            step(2 * j + 1, 1)
            step(2 * j + 2, 0)
            return carry

        lax.fori_loop(0, (nk - 2) // 2, pair, 0, unroll=4)
        step(nk - 1, 1, last=True)
        stage_c(nk - 1, 1)
        for i, h in enumerate(heads):
            acc = acc_scr[i]
            ot_scr[h * HEAD_DIM:(h + 1) * HEAD_DIM, :] = acc[0:HEAD_DIM] / acc[HEAD_DIM:HEAD_DIM + 1]
    o_ref[0] = ot_scr[...].T.astype(o_ref.dtype)


def _glob_attn(qb, kb, vb, tq, tk):
    b, s, _ = qb.shape
    assert s % (2 * tk) == 0, "the chunk pipeline walks key chunks in pairs"
    return pl.pallas_call(
        functools.partial(_glob_attn_kernel, tk=tk),
        grid=(b, s // tq),
        in_specs=[pl.BlockSpec((1, tq, Q_W), lambda bi, qi: (bi, qi, 0)),
                  pl.BlockSpec((1, s, KV_W), lambda bi, qi: (bi, 0, 0)),
                  pl.BlockSpec((1, s, KV_W), lambda bi, qi: (bi, 0, 0))],
        out_specs=pl.BlockSpec((1, tq, Q_W), lambda bi, qi: (bi, qi, 0)),
        out_shape=jax.ShapeDtypeStruct((b, s, Q_W), BF16),
        scratch_shapes=[pltpu.VMEM((KV_HEADS, s, HEAD_DIM), BF16), pltpu.VMEM((KV_HEADS * VT_ROWS, s), BF16),
                        pltpu.VMEM((Q_W, tq), BF16),
                        pltpu.VMEM((2, GROUP, tk, tq), SCORE_DTYPE), pltpu.VMEM((2, GROUP, tk, tq), BF16),
                        pltpu.VMEM((2, GROUP, 1, tq), F32), pltpu.VMEM((GROUP, 1, tq), F32),
                        pltpu.VMEM((2, GROUP, 1, tq), F32), pltpu.VMEM((GROUP, VT_ROWS, tq), F32),
                        pltpu.VMEM((Q_W, tq), F32)],
        compiler_params=_params(("arbitrary", "arbitrary")),
        name="glob_attn",
    )(qb, kb, vb)


def _first_argmax(vals, idx, big):
    m = jnp.max(vals, axis=0, keepdims=True)
    return m, jnp.min(jnp.where(vals == m, idx, big), axis=0, keepdims=True)


def _route(lt):
    n = lt.shape[1]
    gl = lt[0:N_GROUPS]
    idx = lax.broadcasted_iota(jnp.int32, (N_GROUPS, n), 0)
    gmax, g_sel = _first_argmax(gl, idx, N_GROUPS)
    g_w = 1.0 / jnp.sum(jnp.exp(gl - gmax), axis=0, keepdims=True)
    e_sel = jnp.zeros((EXPERTS_PER_GROUP, n), F32)
    for g in range(N_GROUPS):
        lo = N_GROUPS + g * EXPERTS_PER_GROUP
        e_sel = jnp.where(g_sel == g, lt[lo:lo + EXPERTS_PER_GROUP], e_sel)
    ex = jnp.exp(e_sel - jnp.max(e_sel, axis=0, keepdims=True))
    e_prob = ex / jnp.sum(ex, axis=0, keepdims=True)
    p1, i1 = _first_argmax(e_prob, idx, EXPERTS_PER_GROUP)
    rest = jnp.where(idx == i1, -1.0, e_prob)
    p2, i2 = _first_argmax(rest, idx, EXPERTS_PER_GROUP)
    tot = p1 + p2
    row = lax.broadcasted_iota(jnp.int32, (2 * N_GROUPS, n), 0)
    out = jnp.where(row == g_sel, 1.0, 0.0)
    out = out + jnp.where(row == N_GROUPS + i1, g_w * (p1 / tot), 0.0)
    return out + jnp.where(row == N_GROUPS + i2, g_w * (p2 / tot), 0.0)


def _merge_kernel(x_ref, ya_ref, yb_ref, ga_ref, gb_ref, woa_ref, wob_ref, wout_ref, gf_ref, wr_ref, br_ref,
                  x1_ref, route_ref, routet_ref, cnt_ref):
    tm = x_ref.shape[0]
    half = tm // 2
    cnt = jnp.zeros((2 * N_GROUPS, 1), F32)
    for r in (0, half):
        rows = slice(r, r + half)
        a = jnp.dot(ya_ref[rows, :], woa_ref[...], preferred_element_type=F32)
        b = jnp.dot(yb_ref[rows, :], wob_ref[...], preferred_element_type=F32)
        merged = ga_ref[rows, :].astype(F32) * a + gb_ref[rows, :].astype(F32) * b
        x1 = x_ref[rows, :] + jnp.dot(merged.astype(BF16), wout_ref[...], preferred_element_type=F32)
        x1_ref[rows, :] = x1
        t = _rmsnorm(x1, gf_ref[...])
        t_hi, t_lo = _hi_lo(t)
        big = jnp.dot(t_hi, wr_ref[...], preferred_element_type=F32)
        small = jnp.dot(t_lo, wr_ref[:, 0:LANES], preferred_element_type=F32)
        logits = big[:, 0:LANES] + big[:, LANES:2 * LANES] + small + br_ref[...]
        block = _route(logits.T)
        route_t = jnp.concatenate([block, jnp.zeros((LANES - block.shape[0], half), F32)], axis=0)
        routet_ref[:, rows] = route_t
        route_ref[rows, :] = route_t.T
        cnt = cnt + jnp.sum(block, axis=1, keepdims=True)
    cnt_ref[0] = jnp.broadcast_to(cnt, cnt_ref.shape[1:]).astype(jnp.int32)


def _merge(x2, ya, yb, ga, gb, w_oa, w_ob, w_out, ffn_g, w_rg, b_rg, w_re, b_re, tm):
    t, d = x2.shape
    wr = jnp.zeros((d, LANES), F32).at[:, :N_GROUPS].set(w_rg).at[:, N_GROUPS:N_GROUPS + N_EXPERTS].set(w_re)
    wr_hi, wr_lo = _hi_lo(wr)
    br = jnp.zeros((1, LANES), F32).at[0, :N_GROUPS].set(b_rg).at[0, N_GROUPS:N_GROUPS + N_EXPERTS].set(b_re)
    row = lambda i: (i, 0)
    return pl.pallas_call(
        _merge_kernel,
        grid=(t // tm,),
        in_specs=[pl.BlockSpec((tm, d), row), pl.BlockSpec((tm, Q_W), row), pl.BlockSpec((tm, Q_W), row),
                  pl.BlockSpec((tm, d), row), pl.BlockSpec((tm, d), row),
                  _full((Q_W, d)), _full((Q_W, d)), _full((d, d)), _full((1, d)), _full((d, 2 * LANES)),
                  _full((1, LANES))],
        out_specs=[pl.BlockSpec((tm, d), row), pl.BlockSpec((tm, LANES), row),
                   pl.BlockSpec((LANES, tm), lambda i: (0, i)), pl.BlockSpec((1, 8, LANES), lambda i: (i, 0, 0))],
        out_shape=[jax.ShapeDtypeStruct((t, d), F32), jax.ShapeDtypeStruct((t, LANES), F32),
                   jax.ShapeDtypeStruct((LANES, t), F32), jax.ShapeDtypeStruct((t // tm, 8, LANES), jnp.int32)],
        compiler_params=_params(("parallel",)),
        name="merge",
    )(x2, ya, yb, ga, gb, w_oa.astype(BF16), w_ob.astype(BF16), w_out.astype(BF16), ffn_g.reshape(1, d),
      jnp.concatenate([wr_hi, wr_lo], axis=1), br)


ROW_ALIGN = 128
BIG_BLOCK = 2 * ROW_ALIGN
GATHER_BLOCK = 4 * ROW_ALIGN
MAX_WHOLE = 4


def _rmsnorm(x, g):
    ms = jnp.mean(x * x, axis=-1, keepdims=True)
    return x * lax.rsqrt(ms + EPS) * g


def _moe_kernel(cnt_ref, x1_ref, route_ref, routet_ref, gf_ref, wg_ref, wu_ref, wd_ref, gfin_ref, o_ref,
                p_scr, q_scr, ts_scr, ws_scr):
    i = pl.program_id(0)
    g = pl.program_id(1)
    tm = x1_ref.shape[0]
    rows = p_scr.shape[0]
    offs, off = [], 0
    for gg in range(N_GROUPS):
        offs.append(off)
        off = off + (cnt_ref[i, gg] + (ROW_ALIGN - 1)) // ROW_ALIGN * ROW_ALIGN

    @pl.when(g == 0)
    def _():
        route = route_ref[...]
        lane = lax.broadcasted_iota(jnp.int32, (tm, LANES), 1)
        onehot = jnp.where(lane < N_GROUPS, route, 0.0)
        r_i = lax.broadcasted_iota(jnp.int32, (tm, tm), 0)
        c_i = lax.broadcasted_iota(jnp.int32, (tm, tm), 1)
        before = jnp.dot(jnp.where(c_i < r_i, 1.0, 0.0).astype(BF16), onehot.astype(BF16),
                         preferred_element_type=F32)
        goff = jnp.zeros((tm, LANES), F32)
        for gg in range(N_GROUPS):
            goff = jnp.where(lane == gg, jnp.asarray(offs[gg], jnp.int32).astype(F32), goff)
        dest_col = jnp.sum((before + goff) * onehot, axis=1, keepdims=True).astype(jnp.int32)
        q_scr[...] = jnp.where(dest_col == lax.broadcasted_iota(jnp.int32, (tm, rows), 1), 1.0, 0.0).astype(BF16)
        sub = lax.broadcasted_iota(jnp.int32, (16, tm), 0)
        onehot_t = jnp.where(sub < N_GROUPS, routet_ref[0:16, :], 0.0)
        before_t = jnp.dot(onehot_t.astype(BF16), jnp.where(r_i < c_i, 1.0, 0.0).astype(BF16),
                           preferred_element_type=F32)
        goff_t = jnp.zeros((16, tm), F32)
        for gg in range(N_GROUPS):
            goff_t = jnp.where(sub == gg, jnp.asarray(offs[gg], jnp.int32).astype(F32), goff_t)
        dest_row = jnp.sum((before_t + goff_t) * onehot_t, axis=0, keepdims=True).astype(jnp.int32)
        for r0 in range(0, rows, BIG_BLOCK):
            rid = lax.broadcasted_iota(jnp.int32, (BIG_BLOCK, tm), 0) + r0
            p_scr[r0:r0 + BIG_BLOCK, :] = jnp.where(dest_row == rid, 1.0, 0.0).astype(BF16)
        t = _rmsnorm(x1_ref[...], gf_ref[...]).astype(BF16)
        r_hi_lo = jnp.concatenate(_hi_lo(route), axis=1)
        for r0 in range(0, rows, GATHER_BLOCK):
            pb = p_scr[r0:r0 + GATHER_BLOCK, :]
            ts_scr[r0:r0 + GATHER_BLOCK, :] = jnp.dot(pb, t, preferred_element_type=F32).astype(BF16)
            w2 = jnp.dot(pb, r_hi_lo, preferred_element_type=F32)
            ws_scr[r0:r0 + GATHER_BLOCK, :] = w2[:, 0:LANES] + w2[:, LANES:2 * LANES]

    def ffn(r, m):
        tb = ts_scr[pl.ds(pl.multiple_of(r, ROW_ALIGN), m), :]
        wb = ws_scr[pl.ds(pl.multiple_of(r, ROW_ALIGN), m), :]
        y = jnp.zeros((m, o_ref.shape[1]), F32)
        for e in range(EXPERTS_PER_GROUP):
            a = jax.nn.silu(jnp.dot(tb, wg_ref[0, e], preferred_element_type=F32)) * jnp.dot(
                tb, wu_ref[0, e], preferred_element_type=F32)
            y = y + wb[:, N_GROUPS + e:N_GROUPS + e + 1] * jnp.dot(a.astype(BF16), wd_ref[0, e],
                                                                   preferred_element_type=F32)
        ts_scr[pl.ds(pl.multiple_of(r, ROW_ALIGN), m), :] = y.astype(BF16)

    start = offs[0]
    for gg in range(1, N_GROUPS):
        start = jnp.where(g == gg, offs[gg], start)
    n_small = (cnt_ref[i, g] + (ROW_ALIGN - 1)) // ROW_ALIGN

    for units in range(1, MAX_WHOLE + 1):
        @pl.when(n_small == units)
        def _(units=units):
            ffn(start, units * ROW_ALIGN)

    @pl.when(n_small > MAX_WHOLE)
    def _():
        n_big = n_small // 2

        def big(j, carry):
            ffn(start + j * BIG_BLOCK, BIG_BLOCK)
            return carry

        lax.fori_loop(0, n_big, big, 0)

        @pl.when(n_small % 2 == 1)
        def _():
            ffn(start + n_big * BIG_BLOCK, ROW_ALIGN)

    @pl.when(g == N_GROUPS - 1)
    def _():
        x2 = x1_ref[...] + jnp.dot(q_scr[...], ts_scr[...], preferred_element_type=F32)
        o_ref[...] = _rmsnorm(x2, gfin_ref[...])


def _moe(x1, route, routet, cnt, ffn_g, w_eg, w_eu, w_ed, final_g, tm):
    t, d = x1.shape
    de = w_eg.shape[-1]
    rows = tm + N_GROUPS * ROW_ALIGN
    grouped = lambda w: w.astype(BF16).reshape((N_GROUPS, EXPERTS_PER_GROUP) + w.shape[1:])
    row = lambda i, g, c: (i, 0)
    const = lambda i, g, c: (0, 0)
    wspec = lambda a, b: pl.BlockSpec((1, EXPERTS_PER_GROUP, a, b), lambda i, g, c: (g, 0, 0, 0))
    return pl.pallas_call(
        _moe_kernel,
        grid_spec=pltpu.PrefetchScalarGridSpec(
            num_scalar_prefetch=1,
            grid=(t // tm, N_GROUPS),
            in_specs=[pl.BlockSpec((tm, d), row), pl.BlockSpec((tm, LANES), row),
                      pl.BlockSpec((LANES, tm), lambda i, g, c: (0, i)), pl.BlockSpec((1, d), const),
                      wspec(d, de), wspec(d, de), wspec(de, d), pl.BlockSpec((1, d), const)],
            out_specs=pl.BlockSpec((tm, d), row),
            scratch_shapes=[pltpu.VMEM((rows, tm), BF16), pltpu.VMEM((tm, rows), BF16), pltpu.VMEM((rows, d), BF16),
                            pltpu.VMEM((rows, LANES), F32)]),
        out_shape=jax.ShapeDtypeStruct((t, d), F32),
        compiler_params=_params(("arbitrary", "arbitrary")),
        name="moe",
    )(cnt, x1, route, routet, ffn_g.reshape(1, d), grouped(w_eg), grouped(w_eu), grouped(w_ed),
      final_g.reshape(1, d))


def _tile(n, want):
    while n % want:
        want //= 2
    return want


def _trunk(x, attn_g, w_in, sink, gq, gk, w_oa, w_ob, w_out, ffn_g, w_rg, b_rg, w_re, b_re, w_eg, w_eu, w_ed,
           final_g):
    b, s, d = x.shape
    x2 = x.reshape(b * s, d)
    qa, ka, va, qb, kb, vb, ga, gb = _in_proj(x2, s, attn_g, w_in, gq, gk, _tile(s, 1024))
    ya = _win_attn(qa, ka, va, sink, s, _tile(s, 512))
    yb = _glob_attn(qb.reshape(b, s, Q_W), kb.reshape(b, s, KV_W), vb.reshape(b, s, KV_W),
                    256, 256).reshape(b * s, Q_W)
    tm_merge, tm_moe = _tile(b * s, 1024), _tile(b * s, 1024)
    x1, route, routet, cnt = _merge(x2, ya, yb, ga, gb, w_oa, w_ob, w_out, ffn_g, w_rg, b_rg, w_re, b_re, tm_merge)
    cnt = cnt[:, :N_GROUPS, 0].reshape(b * s // tm_moe, tm_moe // tm_merge, N_GROUPS).sum(axis=1)
    y = _moe(x1, route, routet, cnt, ffn_g, w_eg, w_eu, w_ed, final_g, tm_moe)
    return y.reshape(b, s, d)


def kernel(x_prompt, x_sample, attn_norm_g, w_in, a_sink, b_q_norm_g, b_k_norm_g, w_oa, w_ob, w_out, ffn_norm_g,
           w_router_group, b_router_group, w_router_expert, b_router_expert, w_expert_gate, w_expert_up,
           w_expert_down, final_norm_g):
    assert attn_norm_g.shape[0] == 1, "single-layer trunk"
    weights = (attn_norm_g[0], w_in[0], a_sink[0], b_q_norm_g[0], b_k_norm_g[0], w_oa[0], w_ob[0], w_out[0],
               ffn_norm_g[0], w_router_group[0], b_router_group[0], w_router_expert[0], b_router_expert[0],
               w_expert_gate[0], w_expert_up[0], w_expert_down[0], final_norm_g)
    return (_trunk(x_prompt, *weights), _trunk(x_sample, *weights))
```

```python
import functools

import jax
import jax.numpy as jnp
import numpy as np
from jax import lax
from jax.experimental import pallas as pl
from jax.experimental.pallas import tpu as pltpu

HEAD_DIM = 64
Q_HEADS = 8
KV_HEADS = 2
GROUP = Q_HEADS // KV_HEADS
Q_W = Q_HEADS * HEAD_DIM
KV_W = KV_HEADS * HEAD_DIM
WINDOW = 128
BLOCK = 128
GRID_W = 64
ROPE_THETA = 10000.0
N_GROUPS = 4
EXPERTS_PER_GROUP = 4
N_EXPERTS = N_GROUPS * EXPERTS_PER_GROUP
EPS = 1e-6
NEG_INF = -1e30
SCALE = HEAD_DIM ** -0.5
LOG2E = 1.4426950408889634
ONES_ROWS = 16
VT_ROWS = HEAD_DIM + ONES_ROWS
SCORE_DTYPE = jnp.bfloat16
P_PAD_ROWS = 16
LANES = 128

VMEM_LIMIT = 56 * 1024 * 1024

F32 = jnp.float32
BF16 = jnp.bfloat16


def _params(sem):
    return pltpu.CompilerParams(dimension_semantics=sem, vmem_limit_bytes=VMEM_LIMIT)


def _full(shape):
    return pl.BlockSpec(shape, lambda *_: (0,) * len(shape))


def _hi_lo(v):
    top = lax.bitcast_convert_type(lax.bitcast_convert_type(v, jnp.uint32) & jnp.uint32(0xFFFF0000), F32)
    return top.astype(BF16), (v - top).astype(BF16)


def _split_dot(y, ones_bd):
    hi, lo = _hi_lo(y)
    return jnp.dot(hi, ones_bd, preferred_element_type=F32) + jnp.dot(lo, ones_bd, preferred_element_type=F32)


def _pair_swap(y):
    n = y.shape[-1]
    lane = lax.broadcasted_iota(jnp.int32, y.shape, y.ndim - 1)
    nxt = pltpu.roll(y, n - 1, y.ndim - 1)
    prv = pltpu.roll(y, 1, y.ndim - 1)
    return jnp.where((lane & 1) == 0, nxt, prv)


def _norm_rope(y, gain, ones_bd, cos, sin):
    ms = _split_dot(y * y, ones_bd) * (1.0 / HEAD_DIM)
    yn = y * lax.rsqrt(ms + EPS) * gain
    return yn * cos + _pair_swap(yn) * sin


def _in_proj_kernel(x_ref, g_ref, w_ref, cos_ref, sin_ref, gq_ref, gk_ref, oq_ref, ok_ref,
                    qa_ref, ka_ref, va_ref, qb_ref, kb_ref, vb_ref, ga_ref, gb_ref):
    half = x_ref.shape[0] // 2
    d = x_ref.shape[1]
    for r in (0, half):
        rows = slice(r, r + half)
        x = x_ref[rows, :]
        ms = jnp.mean(x * x, axis=-1, keepdims=True)
        h = (x * lax.rsqrt(ms + EPS) * g_ref[...]).astype(BF16)

        def proj(lo, width, h=h):
            return jnp.dot(h, w_ref[:, lo:lo + width], preferred_element_type=F32)

        o = 0
        qa_ref[rows, :] = (proj(o, Q_W) * (SCALE * LOG2E)).astype(BF16); o += Q_W
        ka_ref[rows, :] = proj(o, KV_W).astype(BF16); o += KV_W
        va_ref[rows, :] = proj(o, KV_W).astype(BF16); o += KV_W
        cos = cos_ref[rows, :]
        sin = sin_ref[rows, :]
        qb = proj(o, Q_W); o += Q_W
        cos_q = jnp.concatenate([cos] * (Q_W // LANES), axis=1)
        sin_q = jnp.concatenate([sin] * (Q_W // LANES), axis=1)
        qb_ref[rows, :] = (_norm_rope(qb, gq_ref[...], oq_ref[...], cos_q, sin_q) * (SCALE * LOG2E)).astype(BF16)
        kb = proj(o, KV_W); o += KV_W
        kb_ref[rows, :] = _norm_rope(kb, gk_ref[...], ok_ref[...], cos, sin).astype(BF16)
        vb_ref[rows, :] = proj(o, KV_W).astype(BF16); o += KV_W
        ga_ref[rows, :] = jax.nn.sigmoid(proj(o, d)).astype(BF16); o += d
        gb_ref[rows, :] = jax.nn.sigmoid(proj(o, d)).astype(BF16)


def _rope_tables(seq_len):
    rows = seq_len // GRID_W
    row = jnp.repeat(jnp.arange(rows, dtype=F32), GRID_W)
    col = jnp.tile(jnp.arange(GRID_W, dtype=F32), rows)
    half = HEAD_DIM // 2
    inv = ROPE_THETA ** (-jnp.arange(0, half, 2, dtype=F32) / half)
    ang = jnp.concatenate([row[:, None] * inv, col[:, None] * inv], axis=-1)
    cos = jnp.repeat(jnp.cos(ang), 2, axis=-1)
    sin = jnp.repeat(jnp.sin(ang), 2, axis=-1) * jnp.tile(jnp.array([-1.0, 1.0], F32), half)
    reps = LANES // HEAD_DIM
    return jnp.tile(cos, (1, reps)), jnp.tile(sin, (1, reps))


def _block_ones(width):
    idx = np.arange(width) // HEAD_DIM
    return jnp.asarray(idx[:, None] == idx[None, :], dtype=BF16)


def _in_proj(x2, seq_len, attn_g, w_in, gq, gk, tm):
    t, d = x2.shape
    in_w = w_in.shape[1]
    cos, sin = _rope_tables(seq_len)
    nseq = seq_len // tm
    row = lambda i: (i, 0)
    pos = lambda i: (i % nseq, 0)
    outs = [(Q_W, BF16), (KV_W, BF16), (KV_W, BF16), (Q_W, BF16), (KV_W, BF16), (KV_W, BF16), (d, BF16), (d, BF16)]
    return pl.pallas_call(
        _in_proj_kernel,
        grid=(t // tm,),
        in_specs=[pl.BlockSpec((tm, d), row), _full((1, d)), _full((d, in_w)),
                  pl.BlockSpec((tm, LANES), pos), pl.BlockSpec((tm, LANES), pos),
                  _full((1, Q_W)), _full((1, KV_W)), _full((Q_W, Q_W)), _full((KV_W, KV_W))],
        out_specs=[pl.BlockSpec((tm, w), row) for w, _ in outs],
        out_shape=[jax.ShapeDtypeStruct((t, w), dt) for w, dt in outs],
        compiler_params=_params(("parallel",)),
        name="in_proj",
    )(x2, attn_g.reshape(1, d), w_in.astype(BF16), cos, sin,
      jnp.tile(gq.astype(F32), Q_HEADS).reshape(1, Q_W), jnp.tile(gk.astype(F32), KV_HEADS).reshape(1, KV_W),
      _block_ones(Q_W), _block_ones(KV_W))


PAIRS = Q_HEADS // 2
N_EDGE = 4


def _win_bias():
    slopes = np.exp2(-8.0 * np.arange(1, Q_HEADS + 1, dtype=np.float32) / Q_HEADS).astype(np.float32)
    key = np.arange(3 * BLOCK)[:, None]
    dist = np.abs(np.arange(BLOCK)[None, :] - (key - BLOCK))
    tabs = []
    for edge in range(N_EDGE):
        alive = dist <= WINDOW
        if edge & 1:
            alive = alive & (key >= BLOCK)
        if edge & 2:
            alive = alive & (key < 2 * BLOCK)
        heads = [np.where(alive, -slopes[h] * np.float32(LOG2E) * dist.astype(np.float32), np.float32(NEG_INF))
                 for h in range(Q_HEADS)]
        tabs.append(np.stack([np.concatenate([heads[2 * p], heads[2 * p + 1]], axis=1) for p in range(PAIRS)]))
    return jnp.asarray(np.stack(tabs), dtype=F32)


def _win_attn_kernel(sink_ref, q_ref, kc_ref, kp_ref, kn_ref, vc_ref, vp_ref, vn_ref, bias_ref, o_ref,
                     kh_scr, vt_scr, qp_scr, s_scr, p_scr, ot_scr, *, blocks_per_seq, sub):
    i = pl.program_id(0)
    tq = q_ref.shape[0]
    ones = jnp.ones((ONES_ROWS, BLOCK), BF16)
    for off, n, kr, vr in ((0, BLOCK, kp_ref, vp_ref), (BLOCK, tq, kc_ref, vc_ref), (BLOCK + tq, BLOCK, kn_ref, vn_ref)):
        k = kr[...]
        vt = vr[...].astype(F32).T.astype(BF16)
        for g in range(KV_HEADS):
            kh_scr[g, off:off + n, :] = k[:, g * HEAD_DIM:(g + 1) * HEAD_DIM]
            vt_scr[g * VT_ROWS:g * VT_ROWS + HEAD_DIM, off:off + n] = vt[g * HEAD_DIM:(g + 1) * HEAD_DIM, :]
            for o in range(off, off + n, BLOCK):
                vt_scr[g * VT_ROWS + HEAD_DIM:(g + 1) * VT_ROWS, o:o + BLOCK] = ones
    qt = q_ref[...].astype(F32).T.astype(BF16)
    for j in range(sub):
        for p in range(PAIRS):
            for u in range(2):
                h = 2 * p + u
                qp_scr[j, p, :, u * BLOCK:(u + 1) * BLOCK] = qt[h * HEAD_DIM:(h + 1) * HEAD_DIM,
                                                                j * BLOCK:(j + 1) * BLOCK]
    lane = lax.broadcasted_iota(jnp.int32, (1, 2 * BLOCK), 1)
    for j in range(sub):
        x = j % 2
        blk = (i * sub + j) % blocks_per_seq
        edge = (blk == 0).astype(jnp.int32) + 2 * (blk == blocks_per_seq - 1).astype(jnp.int32)
        for p in range(PAIRS):
            g = (2 * p) // GROUP
            s = jnp.dot(kh_scr[g, j * BLOCK:(j + 3) * BLOCK, :], qp_scr[j, p], preferred_element_type=F32)
            s_scr[x, p] = s + bias_ref[edge, p]
        sink_terms = []
        for p in range(PAIRS):
            logits = s_scr[x, p]
            sink = jnp.where(lane < BLOCK, sink_ref[2 * p], sink_ref[2 * p + 1]) * LOG2E
            m = jnp.maximum(jnp.max(logits, axis=0, keepdims=True), sink)
            p_scr[x, p] = jnp.exp2(logits - m).astype(BF16)
            sink_terms.append(jnp.exp2(sink - m))
        for p in range(PAIRS):
            g = (2 * p) // GROUP
            ot = jnp.dot(vt_scr[g * VT_ROWS:(g + 1) * VT_ROWS, j * BLOCK:(j + 3) * BLOCK], p_scr[x, p],
                         preferred_element_type=F32)
            out = ot[0:HEAD_DIM] / (ot[HEAD_DIM:HEAD_DIM + 1] + sink_terms[p])
            for u in range(2):
                h = 2 * p + u
                ot_scr[h * HEAD_DIM:(h + 1) * HEAD_DIM, j * BLOCK:(j + 1) * BLOCK] = out[:, u * BLOCK:(u + 1) * BLOCK]
    o_ref[...] = ot_scr[...].T.astype(o_ref.dtype)


def _win_attn(qa, ka, va, sink, seq_len, tq):
    t = qa.shape[0]
    sub = tq // BLOCK
    nblk = t // BLOCK
    win = tq + 2 * BLOCK
    row = lambda i: (i, 0)
    prev = lambda i: (jnp.maximum(i * sub - 1, 0), 0)
    nxt = lambda i: (jnp.minimum((i + 1) * sub, nblk - 1), 0)
    kv_specs = [pl.BlockSpec((tq, KV_W), row), pl.BlockSpec((BLOCK, KV_W), prev), pl.BlockSpec((BLOCK, KV_W), nxt)]
    bias_spec = pl.BlockSpec((N_EDGE, PAIRS, 3 * BLOCK, 2 * BLOCK), lambda i: (0, 0, 0, 0),
                             pipeline_mode=pl.Buffered(1))
    return pl.pallas_call(
        functools.partial(_win_attn_kernel, blocks_per_seq=seq_len // BLOCK, sub=sub),
        grid=(t // tq,),
        in_specs=[pl.BlockSpec(memory_space=pltpu.SMEM), pl.BlockSpec((tq, Q_W), row)] + kv_specs + kv_specs
                 + [bias_spec],
        out_specs=pl.BlockSpec((tq, Q_W), row),
        out_shape=jax.ShapeDtypeStruct((t, Q_W), BF16),
        scratch_shapes=[pltpu.VMEM((KV_HEADS, win, HEAD_DIM), BF16), pltpu.VMEM((KV_HEADS * VT_ROWS, win), BF16),
                        pltpu.VMEM((sub, PAIRS, HEAD_DIM, 2 * BLOCK), BF16),
                        pltpu.VMEM((2, PAIRS, 3 * BLOCK, 2 * BLOCK), F32),
                        pltpu.VMEM((2, PAIRS, 3 * BLOCK, 2 * BLOCK), BF16), pltpu.VMEM((Q_W, tq), F32)],
        compiler_params=_params(("parallel",)),
        name="win_attn",
    )(sink.astype(F32), qa, ka, ka, ka, va, va, va, _win_bias())


def _glob_attn_kernel(q_ref, k_ref, v_ref, o_ref, kh_ref, vt_ref, qt_scr, s_scr, p_scr, cm_scr, m_scr, a_scr, acc_scr,
                      ot_scr, *, tk):
    tq = q_ref.shape[1]
    nk = k_ref.shape[1] // tk

    @pl.when(pl.program_id(1) == 0)
    def _():
        ones = jnp.ones((ONES_ROWS, tk), BF16)

        def fill(c, carry):
            off = pl.multiple_of(c * tk, tk)
            vt = v_ref[0, pl.ds(off, tk), :].astype(F32).T.astype(BF16)
            kc = k_ref[0, pl.ds(off, tk), :]
            for g in range(KV_HEADS):
                kh_ref[g, pl.ds(off, tk), :] = kc[:, g * HEAD_DIM:(g + 1) * HEAD_DIM]
                vt_ref[g * VT_ROWS:g * VT_ROWS + HEAD_DIM, pl.ds(off, tk)] = vt[g * HEAD_DIM:(g + 1) * HEAD_DIM, :]
                vt_ref[g * VT_ROWS + HEAD_DIM:(g + 1) * VT_ROWS, pl.ds(off, tk)] = ones
            return carry

        lax.fori_loop(0, nk, fill, 0)

    qt_scr[...] = q_ref[0].astype(F32).T.astype(BF16)

    for g in range(KV_HEADS):
        heads = range(g * GROUP, (g + 1) * GROUP)

        def stage_a(c, x, heads=heads, g=g):
            kc = kh_ref[g, pl.ds(pl.multiple_of(c * tk, tk), tk), :]
            for i, h in enumerate(heads):
                s = jnp.dot(kc, qt_scr[h * HEAD_DIM:(h + 1) * HEAD_DIM, :], preferred_element_type=F32)
                s_scr[x, i] = s.astype(s_scr.dtype)
                cm_scr[x, i] = jnp.max(s, axis=0, keepdims=True)

        def stage_b(x):
            for i in range(GROUP):
                m = m_scr[i]
                m_new = jnp.maximum(m, cm_scr[x, i])
                a_scr[x, i] = jnp.exp2(m - m_new)
                m_scr[i] = m_new
                p_scr[x, i, 0:tk, :] = jnp.exp2(s_scr[x, i] - m_new.astype(s_scr.dtype)).astype(BF16)

        def stage_c(c, x, g=g):
            vt = vt_ref[g * VT_ROWS:(g + 1) * VT_ROWS, pl.ds(pl.multiple_of(c * tk, tk), tk)]
            for i in range(GROUP):
                acc_scr[i] = a_scr[x, i] * acc_scr[i] + jnp.dot(vt, p_scr[x, i, 0:tk, :], preferred_element_type=F32)

        def step(c, x, last=False):
            stage_c(c - 1, 1 - x)
            stage_b(x)
            if not last:
                stage_a(c + 1, 1 - x)

        for i in range(GROUP):
            m_scr[i] = jnp.full((1, tq), -jnp.inf, F32)
            acc_scr[i] = jnp.zeros((VT_ROWS, tq), F32)
        stage_a(0, 0)
        stage_b(0)
        stage_a(1, 1)

        def pair(j, carry):
            step(2 * j + 1, 1)
            step(2 * j + 2, 0)
            return carry

        lax.fori_loop(0, (nk - 2) // 2, pair, 0, unroll=4)
        step(nk - 1, 1, last=True)
        stage_c(nk - 1, 1)
        for i, h in enumerate(heads):
            acc = acc_scr[i]
            ot_scr[h * HEAD_DIM:(h + 1) * HEAD_DIM, :] = acc[0:HEAD_DIM] / acc[HEAD_DIM:HEAD_DIM + 1]
    o_ref[0] = ot_scr[...].T.astype(o_ref.dtype)


def _glob_attn(qb, kb, vb, tq, tk):
    b, s, _ = qb.shape
    assert s % (2 * tk) == 0, "the chunk pipeline walks key chunks in pairs"
    return pl.pallas_call(
        functools.partial(_glob_attn_kernel, tk=tk),
        grid=(b, s // tq),
        in_specs=[pl.BlockSpec((1, tq, Q_W), lambda bi, qi: (bi, qi, 0)),
                  pl.BlockSpec((1, s, KV_W), lambda bi, qi: (bi, 0, 0)),
                  pl.BlockSpec((1, s, KV_W), lambda bi, qi: (bi, 0, 0))],
        out_specs=pl.BlockSpec((1, tq, Q_W), lambda bi, qi: (bi, qi, 0)),
        out_shape=jax.ShapeDtypeStruct((b, s, Q_W), BF16),
        scratch_shapes=[pltpu.VMEM((KV_HEADS, s, HEAD_DIM), BF16), pltpu.VMEM((KV_HEADS * VT_ROWS, s), BF16),
                        pltpu.VMEM((Q_W, tq), BF16),
                        pltpu.VMEM((2, GROUP, tk, tq), SCORE_DTYPE), pltpu.VMEM((2, GROUP, tk + P_PAD_ROWS, tq), BF16),
                        pltpu.VMEM((2, GROUP, 1, tq), F32), pltpu.VMEM((GROUP, 1, tq), F32),
                        pltpu.VMEM((2, GROUP, 1, tq), F32), pltpu.VMEM((GROUP, VT_ROWS, tq), F32),
                        pltpu.VMEM((Q_W, tq), F32)],
        compiler_params=_params(("arbitrary", "arbitrary")),
        name="glob_attn",
    )(qb, kb, vb)


def _first_argmax(vals, idx, big):
    m = jnp.max(vals, axis=0, keepdims=True)
    return m, jnp.min(jnp.where(vals == m, idx, big), axis=0, keepdims=True)


def _route(lt):
    n = lt.shape[1]
    gl = lt[0:N_GROUPS]
    idx = lax.broadcasted_iota(jnp.int32, (N_GROUPS, n), 0)
    gmax, g_sel = _first_argmax(gl, idx, N_GROUPS)
    g_w = 1.0 / jnp.sum(jnp.exp(gl - gmax), axis=0, keepdims=True)
    e_sel = jnp.zeros((EXPERTS_PER_GROUP, n), F32)
    for g in range(N_GROUPS):
        lo = N_GROUPS + g * EXPERTS_PER_GROUP
        e_sel = jnp.where(g_sel == g, lt[lo:lo + EXPERTS_PER_GROUP], e_sel)
    ex = jnp.exp(e_sel - jnp.max(e_sel, axis=0, keepdims=True))
    e_prob = ex / jnp.sum(ex, axis=0, keepdims=True)
    p1, i1 = _first_argmax(e_prob, idx, EXPERTS_PER_GROUP)
    rest = jnp.where(idx == i1, -1.0, e_prob)
    p2, i2 = _first_argmax(rest, idx, EXPERTS_PER_GROUP)
    tot = p1 + p2
    row = lax.broadcasted_iota(jnp.int32, (2 * N_GROUPS, n), 0)
    out = jnp.where(row == g_sel, 1.0, 0.0)
    out = out + jnp.where(row == N_GROUPS + i1, g_w * (p1 / tot), 0.0)
    return out + jnp.where(row == N_GROUPS + i2, g_w * (p2 / tot), 0.0)


def _merge_kernel(x_ref, ya_ref, yb_ref, ga_ref, gb_ref, woa_ref, wob_ref, wout_ref, gf_ref, wr_ref, br_ref,
                  x1_ref, route_ref, routet_ref, cnt_ref):
    tm = x_ref.shape[0]
    half = tm // 2
    cnt = jnp.zeros((2 * N_GROUPS, 1), F32)
    for r in (0, half):
        rows = slice(r, r + half)
        a = jnp.dot(ya_ref[rows, :], woa_ref[...], preferred_element_type=F32)
        b = jnp.dot(yb_ref[rows, :], wob_ref[...], preferred_element_type=F32)
        merged = ga_ref[rows, :].astype(F32) * a + gb_ref[rows, :].astype(F32) * b
        x1 = x_ref[rows, :] + jnp.dot(merged.astype(BF16), wout_ref[...], preferred_element_type=F32)
        x1_ref[rows, :] = x1
        t = _rmsnorm(x1, gf_ref[...])
        t_hi, t_lo = _hi_lo(t)
        big = jnp.dot(t_hi, wr_ref[...], preferred_element_type=F32)
        small = jnp.dot(t_lo, wr_ref[:, 0:LANES], preferred_element_type=F32)
        logits = big[:, 0:LANES] + big[:, LANES:2 * LANES] + small + br_ref[...]
        block = _route(logits.T)
        route_t = jnp.concatenate([block, jnp.zeros((LANES - block.shape[0], half), F32)], axis=0)
        routet_ref[:, rows] = route_t
        route_ref[rows, :] = route_t.T
        cnt = cnt + jnp.sum(block, axis=1, keepdims=True)
    cnt_ref[0] = jnp.broadcast_to(cnt, cnt_ref.shape[1:]).astype(jnp.int32)


def _merge(x2, ya, yb, ga, gb, w_oa, w_ob, w_out, ffn_g, w_rg, b_rg, w_re, b_re, tm):
    t, d = x2.shape
    wr = jnp.zeros((d, LANES), F32).at[:, :N_GROUPS].set(w_rg).at[:, N_GROUPS:N_GROUPS + N_EXPERTS].set(w_re)
    wr_hi, wr_lo = _hi_lo(wr)
    br = jnp.zeros((1, LANES), F32).at[0, :N_GROUPS].set(b_rg).at[0, N_GROUPS:N_GROUPS + N_EXPERTS].set(b_re)
    row = lambda i: (i, 0)
    return pl.pallas_call(
        _merge_kernel,
        grid=(t // tm,),
        in_specs=[pl.BlockSpec((tm, d), row), pl.BlockSpec((tm, Q_W), row), pl.BlockSpec((tm, Q_W), row),
                  pl.BlockSpec((tm, d), row), pl.BlockSpec((tm, d), row),
                  _full((Q_W, d)), _full((Q_W, d)), _full((d, d)), _full((1, d)), _full((d, 2 * LANES)),
                  _full((1, LANES))],
        out_specs=[pl.BlockSpec((tm, d), row), pl.BlockSpec((tm, LANES), row),
                   pl.BlockSpec((LANES, tm), lambda i: (0, i)), pl.BlockSpec((1, 8, LANES), lambda i: (i, 0, 0))],
        out_shape=[jax.ShapeDtypeStruct((t, d), F32), jax.ShapeDtypeStruct((t, LANES), F32),
                   jax.ShapeDtypeStruct((LANES, t), F32), jax.ShapeDtypeStruct((t // tm, 8, LANES), jnp.int32)],
        compiler_params=_params(("parallel",)),
        name="merge",
    )(x2, ya, yb, ga, gb, w_oa.astype(BF16), w_ob.astype(BF16), w_out.astype(BF16), ffn_g.reshape(1, d),
      jnp.concatenate([wr_hi, wr_lo], axis=1), br)


ROW_ALIGN = 128
BIG_BLOCK = 2 * ROW_ALIGN
GATHER_BLOCK = 4 * ROW_ALIGN
MAX_WHOLE = 4


def _rmsnorm(x, g):
    ms = jnp.mean(x * x, axis=-1, keepdims=True)
    return x * lax.rsqrt(ms + EPS) * g


def _moe_kernel(cnt_ref, x1_ref, route_ref, routet_ref, gf_ref, wg_ref, wu_ref, wd_ref, gfin_ref, o_ref,
                p_scr, q_scr, ts_scr, ws_scr):
    i = pl.program_id(0)
    g = pl.program_id(1)
    tm = x1_ref.shape[0]
    rows = p_scr.shape[0]
    offs, off = [], 0
    for gg in range(N_GROUPS):
        offs.append(off)
        off = off + (cnt_ref[i, gg] + (ROW_ALIGN - 1)) // ROW_ALIGN * ROW_ALIGN

    @pl.when(g == 0)
    def _():
        route = route_ref[...]
        lane = lax.broadcasted_iota(jnp.int32, (tm, LANES), 1)
        onehot = jnp.where(lane < N_GROUPS, route, 0.0)
        r_i = lax.broadcasted_iota(jnp.int32, (tm, tm), 0)
        c_i = lax.broadcasted_iota(jnp.int32, (tm, tm), 1)
        before = jnp.dot(jnp.where(c_i < r_i, 1.0, 0.0).astype(BF16), onehot.astype(BF16),
                         preferred_element_type=F32)
        goff = jnp.zeros((tm, LANES), F32)
        for gg in range(N_GROUPS):
            goff = jnp.where(lane == gg, jnp.asarray(offs[gg], jnp.int32).astype(F32), goff)
        dest_col = jnp.sum((before + goff) * onehot, axis=1, keepdims=True).astype(jnp.int32)
        q_scr[...] = jnp.where(dest_col == lax.broadcasted_iota(jnp.int32, (tm, rows), 1), 1.0, 0.0).astype(BF16)
        sub = lax.broadcasted_iota(jnp.int32, (16, tm), 0)
        onehot_t = jnp.where(sub < N_GROUPS, routet_ref[0:16, :], 0.0)
        before_t = jnp.dot(onehot_t.astype(BF16), jnp.where(r_i < c_i, 1.0, 0.0).astype(BF16),
                           preferred_element_type=F32)
        goff_t = jnp.zeros((16, tm), F32)
        for gg in range(N_GROUPS):
            goff_t = jnp.where(sub == gg, jnp.asarray(offs[gg], jnp.int32).astype(F32), goff_t)
        dest_row = jnp.sum((before_t + goff_t) * onehot_t, axis=0, keepdims=True).astype(jnp.int32)
        for r0 in range(0, rows, BIG_BLOCK):
            rid = lax.broadcasted_iota(jnp.int32, (BIG_BLOCK, tm), 0) + r0
            p_scr[r0:r0 + BIG_BLOCK, :] = jnp.where(dest_row == rid, 1.0, 0.0).astype(BF16)
        t = _rmsnorm(x1_ref[...], gf_ref[...]).astype(BF16)
        r_hi_lo = jnp.concatenate(_hi_lo(route), axis=1)
        for r0 in range(0, rows, GATHER_BLOCK):
            pb = p_scr[r0:r0 + GATHER_BLOCK, :]
            ts_scr[r0:r0 + GATHER_BLOCK, :] = jnp.dot(pb, t, preferred_element_type=F32).astype(BF16)
            w2 = jnp.dot(pb, r_hi_lo, preferred_element_type=F32)
            ws_scr[r0:r0 + GATHER_BLOCK, :] = w2[:, 0:LANES] + w2[:, LANES:2 * LANES]

    def ffn(r, m):
        tb = ts_scr[pl.ds(pl.multiple_of(r, ROW_ALIGN), m), :]
        wb = ws_scr[pl.ds(pl.multiple_of(r, ROW_ALIGN), m), :]
        y = jnp.zeros((m, o_ref.shape[1]), F32)
        for e in range(EXPERTS_PER_GROUP):
            a = jax.nn.silu(jnp.dot(tb, wg_ref[0, e], preferred_element_type=F32)) * jnp.dot(
                tb, wu_ref[0, e], preferred_element_type=F32)
            y = y + wb[:, N_GROUPS + e:N_GROUPS + e + 1] * jnp.dot(a.astype(BF16), wd_ref[0, e],
                                                                   preferred_element_type=F32)
        ts_scr[pl.ds(pl.multiple_of(r, ROW_ALIGN), m), :] = y.astype(BF16)

    start = offs[0]
    for gg in range(1, N_GROUPS):
        start = jnp.where(g == gg, offs[gg], start)
    n_small = (cnt_ref[i, g] + (ROW_ALIGN - 1)) // ROW_ALIGN

    for units in range(1, MAX_WHOLE + 1):
        @pl.when(n_small == units)
        def _(units=units):
            ffn(start, units * ROW_ALIGN)

    @pl.when(n_small > MAX_WHOLE)
    def _():
        n_big = n_small // 2

        def big(j, carry):
            ffn(start + j * BIG_BLOCK, BIG_BLOCK)
            return carry

        lax.fori_loop(0, n_big, big, 0)

        @pl.when(n_small % 2 == 1)
        def _():
            ffn(start + n_big * BIG_BLOCK, ROW_ALIGN)

    @pl.when(g == N_GROUPS - 1)
    def _():
        x2 = x1_ref[...] + jnp.dot(q_scr[...], ts_scr[...], preferred_element_type=F32)
        o_ref[...] = _rmsnorm(x2, gfin_ref[...])


def _moe(x1, route, routet, cnt, ffn_g, w_eg, w_eu, w_ed, final_g, tm):
    t, d = x1.shape
    de = w_eg.shape[-1]
    rows = tm + N_GROUPS * ROW_ALIGN
    grouped = lambda w: w.astype(BF16).reshape((N_GROUPS, EXPERTS_PER_GROUP) + w.shape[1:])
    row = lambda i, g, c: (i, 0)
    const = lambda i, g, c: (0, 0)
    wspec = lambda a, b: pl.BlockSpec((1, EXPERTS_PER_GROUP, a, b), lambda i, g, c: (g, 0, 0, 0))
    return pl.pallas_call(
        _moe_kernel,
        grid_spec=pltpu.PrefetchScalarGridSpec(
            num_scalar_prefetch=1,
            grid=(t // tm, N_GROUPS),
            in_specs=[pl.BlockSpec((tm, d), row), pl.BlockSpec((tm, LANES), row),
                      pl.BlockSpec((LANES, tm), lambda i, g, c: (0, i)), pl.BlockSpec((1, d), const),
                      wspec(d, de), wspec(d, de), wspec(de, d), pl.BlockSpec((1, d), const)],
            out_specs=pl.BlockSpec((tm, d), row),
            scratch_shapes=[pltpu.VMEM((rows, tm), BF16), pltpu.VMEM((tm, rows), BF16), pltpu.VMEM((rows, d), BF16),
                            pltpu.VMEM((rows, LANES), F32)]),
        out_shape=jax.ShapeDtypeStruct((t, d), F32),
        compiler_params=_params(("arbitrary", "arbitrary")),
        name="moe",
    )(cnt, x1, route, routet, ffn_g.reshape(1, d), grouped(w_eg), grouped(w_eu), grouped(w_ed),
      final_g.reshape(1, d))


def _tile(n, want):
    while n % want:
        want //= 2
    return want


def _trunk(x, attn_g, w_in, sink, gq, gk, w_oa, w_ob, w_out, ffn_g, w_rg, b_rg, w_re, b_re, w_eg, w_eu, w_ed,
           final_g):
    b, s, d = x.shape
    x2 = x.reshape(b * s, d)
    qa, ka, va, qb, kb, vb, ga, gb = _in_proj(x2, s, attn_g, w_in, gq, gk, _tile(s, 1024))
    ya = _win_attn(qa, ka, va, sink, s, _tile(s, 512))
    yb = _glob_attn(qb.reshape(b, s, Q_W), kb.reshape(b, s, KV_W), vb.reshape(b, s, KV_W),
                    256, 256).reshape(b * s, Q_W)
    tm_merge, tm_moe = _tile(b * s, 1024), _tile(b * s, 1024)
    x1, route, routet, cnt = _merge(x2, ya, yb, ga, gb, w_oa, w_ob, w_out, ffn_g, w_rg, b_rg, w_re, b_re, tm_merge)
    cnt = cnt[:, :N_GROUPS, 0].reshape(b * s // tm_moe, tm_moe // tm_merge, N_GROUPS).sum(axis=1)
    y = _moe(x1, route, routet, cnt, ffn_g, w_eg, w_eu, w_ed, final_g, tm_moe)
    return y.reshape(b, s, d)


def kernel(x_prompt, x_sample, attn_norm_g, w_in, a_sink, b_q_norm_g, b_k_norm_g, w_oa, w_ob, w_out, ffn_norm_g,
           w_router_group, b_router_group, w_router_expert, b_router_expert, w_expert_gate, w_expert_up,
           w_expert_down, final_norm_g):
    assert attn_norm_g.shape[0] == 1, "single-layer trunk"
    weights = (attn_norm_g[0], w_in[0], a_sink[0], b_q_norm_g[0], b_k_norm_g[0], w_oa[0], w_ob[0], w_out[0],
               ffn_norm_g[0], w_router_group[0], b_router_group[0], w_router_expert[0], b_router_expert[0],
               w_expert_gate[0], w_expert_up[0], w_expert_down[0], final_norm_g)
    return (_trunk(x_prompt, *weights), _trunk(x_sample, *weights))
```

```python
import functools

import jax
import jax.numpy as jnp
import numpy as np
from jax import lax
from jax.experimental import pallas as pl
from jax.experimental.pallas import tpu as pltpu

HEAD_DIM = 64
Q_HEADS = 8
KV_HEADS = 2
GROUP = Q_HEADS // KV_HEADS
Q_W = Q_HEADS * HEAD_DIM
KV_W = KV_HEADS * HEAD_DIM
WINDOW = 128
BLOCK = 128
GRID_W = 64
ROPE_THETA = 10000.0
N_GROUPS = 4
EXPERTS_PER_GROUP = 4
N_EXPERTS = N_GROUPS * EXPERTS_PER_GROUP
EPS = 1e-6
NEG_INF = -1e30
SCALE = HEAD_DIM ** -0.5
LOG2E = 1.4426950408889634
ONES_ROWS = 16
VT_ROWS = HEAD_DIM + ONES_ROWS
SCORE_DTYPE = jnp.bfloat16
LANES = 128

VMEM_LIMIT = 56 * 1024 * 1024

F32 = jnp.float32
BF16 = jnp.bfloat16


def _params(sem):
    return pltpu.CompilerParams(dimension_semantics=sem, vmem_limit_bytes=VMEM_LIMIT)


def _full(shape):
    return pl.BlockSpec(shape, lambda *_: (0,) * len(shape))


def _hi_lo(v):
    top = lax.bitcast_convert_type(lax.bitcast_convert_type(v, jnp.uint32) & jnp.uint32(0xFFFF0000), F32)
    return top.astype(BF16), (v - top).astype(BF16)


def _split_dot(y, ones_bd):
    hi, lo = _hi_lo(y)
    return jnp.dot(hi, ones_bd, preferred_element_type=F32) + jnp.dot(lo, ones_bd, preferred_element_type=F32)


def _pair_swap(y):
    n = y.shape[-1]
    lane = lax.broadcasted_iota(jnp.int32, y.shape, y.ndim - 1)
    nxt = pltpu.roll(y, n - 1, y.ndim - 1)
    prv = pltpu.roll(y, 1, y.ndim - 1)
    return jnp.where((lane & 1) == 0, nxt, prv)


def _norm_rope(y, gain, ones_bd, cos, sin):
    ms = _split_dot(y * y, ones_bd) * (1.0 / HEAD_DIM)
    yn = y * lax.rsqrt(ms + EPS) * gain
    return yn * cos + _pair_swap(yn) * sin


def _in_proj_kernel(x_ref, g_ref, w_ref, cos_ref, sin_ref, gq_ref, gk_ref, oq_ref, ok_ref,
                    qa_ref, ka_ref, va_ref, qb_ref, kb_ref, vb_ref, ga_ref, gb_ref):
    half = x_ref.shape[0] // 2
    d = x_ref.shape[1]
    for r in (0, half):
        rows = slice(r, r + half)
        x = x_ref[rows, :]
        ms = jnp.mean(x * x, axis=-1, keepdims=True)
        h = (x * lax.rsqrt(ms + EPS) * g_ref[...]).astype(BF16)

        def proj(lo, width, h=h):
            return jnp.dot(h, w_ref[:, lo:lo + width], preferred_element_type=F32)

        o = 0
        qa_ref[rows, :] = (proj(o, Q_W) * (SCALE * LOG2E)).astype(BF16); o += Q_W
        ka_ref[rows, :] = proj(o, KV_W).astype(BF16); o += KV_W
        va_ref[rows, :] = proj(o, KV_W).astype(BF16); o += KV_W
        cos = cos_ref[rows, :]
        sin = sin_ref[rows, :]
        qb = proj(o, Q_W); o += Q_W
        cos_q = jnp.concatenate([cos] * (Q_W // LANES), axis=1)
        sin_q = jnp.concatenate([sin] * (Q_W // LANES), axis=1)
        qb_ref[rows, :] = (_norm_rope(qb, gq_ref[...], oq_ref[...], cos_q, sin_q) * (SCALE * LOG2E)).astype(BF16)
        kb = proj(o, KV_W); o += KV_W
        kb_ref[rows, :] = _norm_rope(kb, gk_ref[...], ok_ref[...], cos, sin).astype(BF16)
        vb_ref[rows, :] = proj(o, KV_W).astype(BF16); o += KV_W
        ga_ref[rows, :] = jax.nn.sigmoid(proj(o, d)).astype(BF16); o += d
        gb_ref[rows, :] = jax.nn.sigmoid(proj(o, d)).astype(BF16)


def _rope_tables(seq_len):
    rows = seq_len // GRID_W
    row = jnp.repeat(jnp.arange(rows, dtype=F32), GRID_W)
    col = jnp.tile(jnp.arange(GRID_W, dtype=F32), rows)
    half = HEAD_DIM // 2
    inv = ROPE_THETA ** (-jnp.arange(0, half, 2, dtype=F32) / half)
    ang = jnp.concatenate([row[:, None] * inv, col[:, None] * inv], axis=-1)
    cos = jnp.repeat(jnp.cos(ang), 2, axis=-1)
    sin = jnp.repeat(jnp.sin(ang), 2, axis=-1) * jnp.tile(jnp.array([-1.0, 1.0], F32), half)
    reps = LANES // HEAD_DIM
    return jnp.tile(cos, (1, reps)), jnp.tile(sin, (1, reps))


def _block_ones(width):
    idx = np.arange(width) // HEAD_DIM
    return jnp.asarray(idx[:, None] == idx[None, :], dtype=BF16)


def _in_proj(x2, seq_len, w, tm):
    t, d = x2.shape
    in_w = w["w_in"].shape[1]
    cos, sin = _rope_tables(seq_len)
    nseq = seq_len // tm
    row = lambda i: (i, 0)
    pos = lambda i: (i % nseq, 0)
    outs = [(Q_W, BF16), (KV_W, BF16), (KV_W, BF16), (Q_W, BF16), (KV_W, BF16), (KV_W, BF16), (d, BF16), (d, BF16)]
    return pl.pallas_call(
        _in_proj_kernel,
        grid=(t // tm,),
        in_specs=[pl.BlockSpec((tm, d), row), _full((1, d)), _full((d, in_w)),
                  pl.BlockSpec((tm, LANES), pos), pl.BlockSpec((tm, LANES), pos),
                  _full((1, Q_W)), _full((1, KV_W)), _full((Q_W, Q_W)), _full((KV_W, KV_W))],
        out_specs=[pl.BlockSpec((tm, w), row) for w, _ in outs],
        out_shape=[jax.ShapeDtypeStruct((t, w), dt) for w, dt in outs],
        compiler_params=_params(("parallel",)),
        name="in_proj",
    )(x2, w["attn_g"], w["w_in"], cos, sin, w["gq"], w["gk"], _block_ones(Q_W), _block_ones(KV_W))


PAIRS = Q_HEADS // 2
N_EDGE = 4


def _win_bias():
    slopes = np.exp2(-8.0 * np.arange(1, Q_HEADS + 1, dtype=np.float32) / Q_HEADS).astype(np.float32)
    key = np.arange(3 * BLOCK)[:, None]
    dist = np.abs(np.arange(BLOCK)[None, :] - (key - BLOCK))
    tabs = []
    for edge in range(N_EDGE):
        alive = dist <= WINDOW
        if edge & 1:
            alive = alive & (key >= BLOCK)
        if edge & 2:
            alive = alive & (key < 2 * BLOCK)
        heads = [np.where(alive, -slopes[h] * np.float32(LOG2E) * dist.astype(np.float32), np.float32(NEG_INF))
                 for h in range(Q_HEADS)]
        tabs.append(np.stack([np.concatenate([heads[2 * p], heads[2 * p + 1]], axis=1) for p in range(PAIRS)]))
    return jnp.asarray(np.stack(tabs), dtype=F32)


def _win_attn_kernel(sink_ref, q_ref, kc_ref, kp_ref, kn_ref, vc_ref, vp_ref, vn_ref, bias_ref, o_ref,
                     kh_scr, vt_scr, qp_scr, s_scr, p_scr, ot_scr, *, blocks_per_seq, sub):
    i = pl.program_id(0)
    tq = q_ref.shape[0]
    ones = jnp.ones((ONES_ROWS, BLOCK), BF16)
    for off, n, kr, vr in ((0, BLOCK, kp_ref, vp_ref), (BLOCK, tq, kc_ref, vc_ref), (BLOCK + tq, BLOCK, kn_ref, vn_ref)):
        k = kr[...]
        vt = vr[...].astype(F32).T.astype(BF16)
        for g in range(KV_HEADS):
            kh_scr[g, off:off + n, :] = k[:, g * HEAD_DIM:(g + 1) * HEAD_DIM]
            vt_scr[g * VT_ROWS:g * VT_ROWS + HEAD_DIM, off:off + n] = vt[g * HEAD_DIM:(g + 1) * HEAD_DIM, :]
            for o in range(off, off + n, BLOCK):
                vt_scr[g * VT_ROWS + HEAD_DIM:(g + 1) * VT_ROWS, o:o + BLOCK] = ones
    qt = q_ref[...].astype(F32).T.astype(BF16)
    for j in range(sub):
        for p in range(PAIRS):
            for u in range(2):
                h = 2 * p + u
                qp_scr[j, p, :, u * BLOCK:(u + 1) * BLOCK] = qt[h * HEAD_DIM:(h + 1) * HEAD_DIM,
                                                                j * BLOCK:(j + 1) * BLOCK]
    lane = lax.broadcasted_iota(jnp.int32, (1, 2 * BLOCK), 1)
    for j in range(sub):
        x = j % 2
        blk = (i * sub + j) % blocks_per_seq
        edge = (blk == 0).astype(jnp.int32) + 2 * (blk == blocks_per_seq - 1).astype(jnp.int32)
        for p in range(PAIRS):
            g = (2 * p) // GROUP
            s = jnp.dot(kh_scr[g, j * BLOCK:(j + 3) * BLOCK, :], qp_scr[j, p], preferred_element_type=F32)
            s_scr[x, p] = s + bias_ref[edge, p]
        sink_terms = []
        for p in range(PAIRS):
            logits = s_scr[x, p]
            sink = jnp.where(lane < BLOCK, sink_ref[2 * p], sink_ref[2 * p + 1]) * LOG2E
            m = jnp.maximum(jnp.max(logits, axis=0, keepdims=True), sink)
            p_scr[x, p] = jnp.exp2(logits - m).astype(BF16)
            sink_terms.append(jnp.exp2(sink - m))
        for p in range(PAIRS):
            g = (2 * p) // GROUP
            ot = jnp.dot(vt_scr[g * VT_ROWS:(g + 1) * VT_ROWS, j * BLOCK:(j + 3) * BLOCK], p_scr[x, p],
                         preferred_element_type=F32)
            out = ot[0:HEAD_DIM] / (ot[HEAD_DIM:HEAD_DIM + 1] + sink_terms[p])
            for u in range(2):
                h = 2 * p + u
                ot_scr[h * HEAD_DIM:(h + 1) * HEAD_DIM, j * BLOCK:(j + 1) * BLOCK] = out[:, u * BLOCK:(u + 1) * BLOCK]
    o_ref[...] = ot_scr[...].T.astype(o_ref.dtype)


def _win_attn(qa, ka, va, sink, seq_len, tq):
    t = qa.shape[0]
    sub = tq // BLOCK
    nblk = t // BLOCK
    win = tq + 2 * BLOCK
    row = lambda i: (i, 0)
    prev = lambda i: (jnp.maximum(i * sub - 1, 0), 0)
    nxt = lambda i: (jnp.minimum((i + 1) * sub, nblk - 1), 0)
    kv_specs = [pl.BlockSpec((tq, KV_W), row), pl.BlockSpec((BLOCK, KV_W), prev), pl.BlockSpec((BLOCK, KV_W), nxt)]
    bias_spec = pl.BlockSpec((N_EDGE, PAIRS, 3 * BLOCK, 2 * BLOCK), lambda i: (0, 0, 0, 0),
                             pipeline_mode=pl.Buffered(1))
    return pl.pallas_call(
        functools.partial(_win_attn_kernel, blocks_per_seq=seq_len // BLOCK, sub=sub),
        grid=(t // tq,),
        in_specs=[pl.BlockSpec(memory_space=pltpu.SMEM), pl.BlockSpec((tq, Q_W), row)] + kv_specs + kv_specs
                 + [bias_spec],
        out_specs=pl.BlockSpec((tq, Q_W), row),
        out_shape=jax.ShapeDtypeStruct((t, Q_W), BF16),
        scratch_shapes=[pltpu.VMEM((KV_HEADS, win, HEAD_DIM), BF16), pltpu.VMEM((KV_HEADS * VT_ROWS, win), BF16),
                        pltpu.VMEM((sub, PAIRS, HEAD_DIM, 2 * BLOCK), BF16),
                        pltpu.VMEM((2, PAIRS, 3 * BLOCK, 2 * BLOCK), F32),
                        pltpu.VMEM((2, PAIRS, 3 * BLOCK, 2 * BLOCK), BF16), pltpu.VMEM((Q_W, tq), F32)],
        compiler_params=_params(("parallel",)),
        name="win_attn",
    )(sink, qa, ka, ka, ka, va, va, va, _win_bias())


def _glob_attn_kernel(q_ref, k_ref, v_ref, o_ref, kh_ref, vt_ref, qt_scr, s_scr, p_scr, cm_scr, m_scr, a_scr, acc_scr,
                      ot_scr, *, tk):
    tq = q_ref.shape[1]
    nk = k_ref.shape[1] // tk

    @pl.when(pl.program_id(1) == 0)
    def _():
        ones = jnp.ones((ONES_ROWS, tk), BF16)

        def fill(c, carry):
            off = pl.multiple_of(c * tk, tk)
            vt = v_ref[0, pl.ds(off, tk), :].astype(F32).T.astype(BF16)
            kc = k_ref[0, pl.ds(off, tk), :]
            for g in range(KV_HEADS):
                kh_ref[g, pl.ds(off, tk), :] = kc[:, g * HEAD_DIM:(g + 1) * HEAD_DIM]
                vt_ref[g * VT_ROWS:g * VT_ROWS + HEAD_DIM, pl.ds(off, tk)] = vt[g * HEAD_DIM:(g + 1) * HEAD_DIM, :]
                vt_ref[g * VT_ROWS + HEAD_DIM:(g + 1) * VT_ROWS, pl.ds(off, tk)] = ones
            return carry

        lax.fori_loop(0, nk, fill, 0)

    qt_scr[...] = q_ref[0].astype(F32).T.astype(BF16)

    for g in range(KV_HEADS):
        heads = range(g * GROUP, (g + 1) * GROUP)

        def stage_a(c, x, heads=heads, g=g):
            kc = kh_ref[g, pl.ds(pl.multiple_of(c * tk, tk), tk), :]
            for i, h in enumerate(heads):
                s = jnp.dot(kc, qt_scr[h * HEAD_DIM:(h + 1) * HEAD_DIM, :], preferred_element_type=F32)
                s_scr[x, i] = s.astype(s_scr.dtype)
                cm_scr[x, i] = jnp.max(s, axis=0, keepdims=True)

        def stage_b(x):
            for i in range(GROUP):
                m = m_scr[i]
                m_new = jnp.maximum(m, cm_scr[x, i])
                a_scr[x, i] = jnp.exp2(m - m_new)
                m_scr[i] = m_new
                p_scr[x, i] = jnp.exp2(s_scr[x, i] - m_new.astype(s_scr.dtype)).astype(BF16)

        def stage_c(c, x, g=g):
            vt = vt_ref[g * VT_ROWS:(g + 1) * VT_ROWS, pl.ds(pl.multiple_of(c * tk, tk), tk)]
            for i in range(GROUP):
                acc_scr[i] = a_scr[x, i] * acc_scr[i] + jnp.dot(vt, p_scr[x, i], preferred_element_type=F32)

        def step(c, x, last=False):
            stage_c(c - 1, 1 - x)
            stage_b(x)
            if not last:
                stage_a(c + 1, 1 - x)

        for i in range(GROUP):
            m_scr[i] = jnp.full((1, tq), -jnp.inf, F32)
            acc_scr[i] = jnp.zeros((VT_ROWS, tq), F32)
        stage_a(0, 0)
        stage_b(0)
        stage_a(1, 1)

        def pair(j, carry):
            step(2 * j + 1, 1)
            step(2 * j + 2, 0)
            return carry

        lax.fori_loop(0, (nk - 2) // 2, pair, 0, unroll=4)
        step(nk - 1, 1, last=True)
        stage_c(nk - 1, 1)
        for i, h in enumerate(heads):
            acc = acc_scr[i]
            ot_scr[h * HEAD_DIM:(h + 1) * HEAD_DIM, :] = acc[0:HEAD_DIM] / acc[HEAD_DIM:HEAD_DIM + 1]
    o_ref[0] = ot_scr[...].T.astype(o_ref.dtype)


def _glob_attn(qb, kb, vb, tq, tk):
    b, s, _ = qb.shape
    assert s % (2 * tk) == 0, "the chunk pipeline walks key chunks in pairs"
    return pl.pallas_call(
        functools.partial(_glob_attn_kernel, tk=tk),
        grid=(b, s // tq),
        in_specs=[pl.BlockSpec((1, tq, Q_W), lambda bi, qi: (bi, qi, 0)),
                  pl.BlockSpec((1, s, KV_W), lambda bi, qi: (bi, 0, 0)),
                  pl.BlockSpec((1, s, KV_W), lambda bi, qi: (bi, 0, 0))],
        out_specs=pl.BlockSpec((1, tq, Q_W), lambda bi, qi: (bi, qi, 0)),
        out_shape=jax.ShapeDtypeStruct((b, s, Q_W), BF16),
        scratch_shapes=[pltpu.VMEM((KV_HEADS, s, HEAD_DIM), BF16), pltpu.VMEM((KV_HEADS * VT_ROWS, s), BF16),
                        pltpu.VMEM((Q_W, tq), BF16),
                        pltpu.VMEM((2, GROUP, tk, tq), SCORE_DTYPE), pltpu.VMEM((2, GROUP, tk, tq), BF16),
                        pltpu.VMEM((2, GROUP, 1, tq), F32), pltpu.VMEM((GROUP, 1, tq), F32),
                        pltpu.VMEM((2, GROUP, 1, tq), F32), pltpu.VMEM((GROUP, VT_ROWS, tq), F32),
                        pltpu.VMEM((Q_W, tq), F32)],
        compiler_params=_params(("arbitrary", "arbitrary")),
        name="glob_attn",
    )(qb, kb, vb)


def _first_argmax(vals, idx, big):
    m = jnp.max(vals, axis=0, keepdims=True)
    return m, jnp.min(jnp.where(vals == m, idx, big), axis=0, keepdims=True)


def _route(lt):
    n = lt.shape[1]
    gl = lt[0:N_GROUPS]
    idx = lax.broadcasted_iota(jnp.int32, (N_GROUPS, n), 0)
    gmax, g_sel = _first_argmax(gl, idx, N_GROUPS)
    g_w = 1.0 / jnp.sum(jnp.exp(gl - gmax), axis=0, keepdims=True)
    e_sel = jnp.zeros((EXPERTS_PER_GROUP, n), F32)
    for g in range(N_GROUPS):
        lo = N_GROUPS + g * EXPERTS_PER_GROUP
        e_sel = jnp.where(g_sel == g, lt[lo:lo + EXPERTS_PER_GROUP], e_sel)
    ex = jnp.exp(e_sel - jnp.max(e_sel, axis=0, keepdims=True))
    e_prob = ex / jnp.sum(ex, axis=0, keepdims=True)
    p1, i1 = _first_argmax(e_prob, idx, EXPERTS_PER_GROUP)
    rest = jnp.where(idx == i1, -1.0, e_prob)
    p2, i2 = _first_argmax(rest, idx, EXPERTS_PER_GROUP)
    tot = p1 + p2
    row = lax.broadcasted_iota(jnp.int32, (2 * N_GROUPS, n), 0)
    out = jnp.where(row == g_sel, 1.0, 0.0)
    out = out + jnp.where(row == N_GROUPS + i1, g_w * (p1 / tot), 0.0)
    return out + jnp.where(row == N_GROUPS + i2, g_w * (p2 / tot), 0.0)


def _merge_kernel(x_ref, ya_ref, yb_ref, ga_ref, gb_ref, woa_ref, wob_ref, wout_ref, gf_ref, wr_ref, br_ref,
                  x1_ref, route_ref, routet_ref, cnt_ref):
    tm = x_ref.shape[0]
    half = tm // 2
    cnt = jnp.zeros((2 * N_GROUPS, 1), F32)
    for r in (0, half):
        rows = slice(r, r + half)
        a = jnp.dot(ya_ref[rows, :], woa_ref[...], preferred_element_type=F32)
        b = jnp.dot(yb_ref[rows, :], wob_ref[...], preferred_element_type=F32)
        merged = ga_ref[rows, :].astype(F32) * a + gb_ref[rows, :].astype(F32) * b
        x1 = x_ref[rows, :] + jnp.dot(merged.astype(BF16), wout_ref[...], preferred_element_type=F32)
        x1_ref[rows, :] = x1
        t = _rmsnorm(x1, gf_ref[...])
        t_hi, t_lo = _hi_lo(t)
        big = jnp.dot(t_hi, wr_ref[...], preferred_element_type=F32)
        small = jnp.dot(t_lo, wr_ref[:, 0:LANES], preferred_element_type=F32)
        logits = big[:, 0:LANES] + big[:, LANES:2 * LANES] + small + br_ref[...]
        block = _route(logits.T)
        route_t = jnp.concatenate([block, jnp.zeros((LANES - block.shape[0], half), F32)], axis=0)
        routet_ref[:, rows] = route_t
        route_ref[rows, :] = route_t.T
        cnt = cnt + jnp.sum(block, axis=1, keepdims=True)
    cnt_ref[0] = jnp.broadcast_to(cnt, cnt_ref.shape[1:]).astype(jnp.int32)


def _router_weights(w_rg, b_rg, w_re, b_re):
    d = w_rg.shape[0]
    wr = jnp.zeros((d, LANES), F32).at[:, :N_GROUPS].set(w_rg).at[:, N_GROUPS:N_GROUPS + N_EXPERTS].set(w_re)
    br = jnp.zeros((1, LANES), F32).at[0, :N_GROUPS].set(b_rg).at[0, N_GROUPS:N_GROUPS + N_EXPERTS].set(b_re)
    return jnp.concatenate(_hi_lo(wr), axis=1), br


def _merge(x2, ya, yb, ga, gb, w, tm):
    t, d = x2.shape
    row = lambda i: (i, 0)
    return pl.pallas_call(
        _merge_kernel,
        grid=(t // tm,),
        in_specs=[pl.BlockSpec((tm, d), row), pl.BlockSpec((tm, Q_W), row), pl.BlockSpec((tm, Q_W), row),
                  pl.BlockSpec((tm, d), row), pl.BlockSpec((tm, d), row),
                  _full((Q_W, d)), _full((Q_W, d)), _full((d, d)), _full((1, d)), _full((d, 2 * LANES)),
                  _full((1, LANES))],
        out_specs=[pl.BlockSpec((tm, d), row), pl.BlockSpec((tm, LANES), row),
                   pl.BlockSpec((LANES, tm), lambda i: (0, i)), pl.BlockSpec((1, 8, LANES), lambda i: (i, 0, 0))],
        out_shape=[jax.ShapeDtypeStruct((t, d), F32), jax.ShapeDtypeStruct((t, LANES), F32),
                   jax.ShapeDtypeStruct((LANES, t), F32), jax.ShapeDtypeStruct((t // tm, 8, LANES), jnp.int32)],
        compiler_params=_params(("parallel",)),
        name="merge",
    )(x2, ya, yb, ga, gb, w["w_oa"], w["w_ob"], w["w_out"], w["ffn_g"], w["w_router"], w["b_router"])


ROW_ALIGN = 128
BIG_BLOCK = 2 * ROW_ALIGN
GATHER_BLOCK = 4 * ROW_ALIGN
MAX_WHOLE = 4


def _rmsnorm(x, g):
    ms = jnp.mean(x * x, axis=-1, keepdims=True)
    return x * lax.rsqrt(ms + EPS) * g


def _moe_kernel(cnt_ref, x1_ref, route_ref, routet_ref, gf_ref, wg_ref, wu_ref, wd_ref, gfin_ref, o_ref,
                p_scr, q_scr, ts_scr, ws_scr):
    i = pl.program_id(0)
    g = pl.program_id(1)
    tm = x1_ref.shape[0]
    rows = p_scr.shape[0]
    offs, off = [], 0
    for gg in range(N_GROUPS):
        offs.append(off)
        off = off + (cnt_ref[i, gg] + (ROW_ALIGN - 1)) // ROW_ALIGN * ROW_ALIGN

    @pl.when(g == 0)
    def _():
        route = route_ref[...]
        lane = lax.broadcasted_iota(jnp.int32, (tm, LANES), 1)
        onehot = jnp.where(lane < N_GROUPS, route, 0.0)
        r_i = lax.broadcasted_iota(jnp.int32, (tm, tm), 0)
        c_i = lax.broadcasted_iota(jnp.int32, (tm, tm), 1)
        before = jnp.dot(jnp.where(c_i < r_i, 1.0, 0.0).astype(BF16), onehot.astype(BF16),
                         preferred_element_type=F32)
        goff = jnp.zeros((tm, LANES), F32)
        for gg in range(N_GROUPS):
            goff = jnp.where(lane == gg, jnp.asarray(offs[gg], jnp.int32).astype(F32), goff)
        dest_col = jnp.sum((before + goff) * onehot, axis=1, keepdims=True).astype(jnp.int32)
        q_scr[...] = jnp.where(dest_col == lax.broadcasted_iota(jnp.int32, (tm, rows), 1), 1.0, 0.0).astype(BF16)
        sub = lax.broadcasted_iota(jnp.int32, (16, tm), 0)
        onehot_t = jnp.where(sub < N_GROUPS, routet_ref[0:16, :], 0.0)
        before_t = jnp.dot(onehot_t.astype(BF16), jnp.where(r_i < c_i, 1.0, 0.0).astype(BF16),
                           preferred_element_type=F32)
        goff_t = jnp.zeros((16, tm), F32)
        for gg in range(N_GROUPS):
            goff_t = jnp.where(sub == gg, jnp.asarray(offs[gg], jnp.int32).astype(F32), goff_t)
        dest_row = jnp.sum((before_t + goff_t) * onehot_t, axis=0, keepdims=True).astype(jnp.int32)
        for r0 in range(0, rows, BIG_BLOCK):
            rid = lax.broadcasted_iota(jnp.int32, (BIG_BLOCK, tm), 0) + r0
            p_scr[r0:r0 + BIG_BLOCK, :] = jnp.where(dest_row == rid, 1.0, 0.0).astype(BF16)
        t = _rmsnorm(x1_ref[...], gf_ref[...]).astype(BF16)
        r_hi_lo = jnp.concatenate(_hi_lo(route), axis=1)
        for r0 in range(0, rows, GATHER_BLOCK):
            pb = p_scr[r0:r0 + GATHER_BLOCK, :]
            ts_scr[r0:r0 + GATHER_BLOCK, :] = jnp.dot(pb, t, preferred_element_type=F32).astype(BF16)
            w2 = jnp.dot(pb, r_hi_lo, preferred_element_type=F32)
            ws_scr[r0:r0 + GATHER_BLOCK, :] = w2[:, 0:LANES] + w2[:, LANES:2 * LANES]

    def ffn(r, m):
        tb = ts_scr[pl.ds(pl.multiple_of(r, ROW_ALIGN), m), :]
        wb = ws_scr[pl.ds(pl.multiple_of(r, ROW_ALIGN), m), :]
        y = jnp.zeros((m, o_ref.shape[1]), F32)
        for e in range(EXPERTS_PER_GROUP):
            a = jax.nn.silu(jnp.dot(tb, wg_ref[0, e], preferred_element_type=F32)) * jnp.dot(
                tb, wu_ref[0, e], preferred_element_type=F32)
            y = y + wb[:, N_GROUPS + e:N_GROUPS + e + 1] * jnp.dot(a.astype(BF16), wd_ref[0, e],
                                                                   preferred_element_type=F32)
        ts_scr[pl.ds(pl.multiple_of(r, ROW_ALIGN), m), :] = y.astype(BF16)

    start = offs[0]
    for gg in range(1, N_GROUPS):
        start = jnp.where(g == gg, offs[gg], start)
    n_small = (cnt_ref[i, g] + (ROW_ALIGN - 1)) // ROW_ALIGN

    for units in range(1, MAX_WHOLE + 1):
        @pl.when(n_small == units)
        def _(units=units):
            ffn(start, units * ROW_ALIGN)

    @pl.when(n_small > MAX_WHOLE)
    def _():
        n_big = n_small // 2

        def big(j, carry):
            ffn(start + j * BIG_BLOCK, BIG_BLOCK)
            return carry

        lax.fori_loop(0, n_big, big, 0)

        @pl.when(n_small % 2 == 1)
        def _():
            ffn(start + n_big * BIG_BLOCK, ROW_ALIGN)

    @pl.when(g == N_GROUPS - 1)
    def _():
        x2 = x1_ref[...] + jnp.dot(q_scr[...], ts_scr[...], preferred_element_type=F32)
        o_ref[...] = _rmsnorm(x2, gfin_ref[...])


def _moe(x1, route, routet, cnt, w, tm):
    t, d = x1.shape
    de = w["w_eg"].shape[-1]
    rows = tm + N_GROUPS * ROW_ALIGN
    row = lambda i, g, c: (i, 0)
    const = lambda i, g, c: (0, 0)
    wspec = lambda a, b: pl.BlockSpec((1, EXPERTS_PER_GROUP, a, b), lambda i, g, c: (g, 0, 0, 0))
    return pl.pallas_call(
        _moe_kernel,
        grid_spec=pltpu.PrefetchScalarGridSpec(
            num_scalar_prefetch=1,
            grid=(t // tm, N_GROUPS),
            in_specs=[pl.BlockSpec((tm, d), row), pl.BlockSpec((tm, LANES), row),
                      pl.BlockSpec((LANES, tm), lambda i, g, c: (0, i)), pl.BlockSpec((1, d), const),
                      wspec(d, de), wspec(d, de), wspec(de, d), pl.BlockSpec((1, d), const)],
            out_specs=pl.BlockSpec((tm, d), row),
            scratch_shapes=[pltpu.VMEM((rows, tm), BF16), pltpu.VMEM((tm, rows), BF16), pltpu.VMEM((rows, d), BF16),
                            pltpu.VMEM((rows, LANES), F32)]),
        out_shape=jax.ShapeDtypeStruct((t, d), F32),
        compiler_params=_params(("arbitrary", "arbitrary")),
        name="moe",
    )(cnt, x1, route, routet, w["ffn_g"], w["w_eg"], w["w_eu"], w["w_ed"], w["final_g"])


ROW_TILE = 1024
WIN_TILE = 512
GLOB_TQ = 256
GLOB_TK = 256


def _prepare(attn_g, w_in, sink, gq, gk, w_oa, w_ob, w_out, ffn_g, w_rg, b_rg, w_re, b_re, w_eg, w_eu, w_ed,
             final_g):
    d = w_in.shape[0]
    w_router, b_router = _router_weights(w_rg, b_rg, w_re, b_re)
    grouped = lambda a: a.astype(BF16).reshape((N_GROUPS, EXPERTS_PER_GROUP) + a.shape[1:])
    return dict(
        attn_g=attn_g.reshape(1, d), w_in=w_in.astype(BF16), sink=sink.astype(F32),
        gq=jnp.tile(gq.astype(F32), Q_HEADS).reshape(1, Q_W), gk=jnp.tile(gk.astype(F32), KV_HEADS).reshape(1, KV_W),
        w_oa=w_oa.astype(BF16), w_ob=w_ob.astype(BF16), w_out=w_out.astype(BF16), ffn_g=ffn_g.reshape(1, d),
        w_router=w_router, b_router=b_router, w_eg=grouped(w_eg), w_eu=grouped(w_eu), w_ed=grouped(w_ed),
        final_g=final_g.reshape(1, d))


def _trunk(x, w):
    b, s, d = x.shape
    t = b * s
    assert s % ROW_TILE == 0 and s % (2 * GLOB_TK) == 0, "sequence length must be a multiple of the row tile"
    x2 = x.reshape(t, d)
    qa, ka, va, qb, kb, vb, ga, gb = _in_proj(x2, s, w, ROW_TILE)
    ya = _win_attn(qa, ka, va, w["sink"], s, WIN_TILE)
    yb = _glob_attn(qb.reshape(b, s, Q_W), kb.reshape(b, s, KV_W), vb.reshape(b, s, KV_W),
                    GLOB_TQ, GLOB_TK).reshape(t, Q_W)
    x1, route, routet, cnt = _merge(x2, ya, yb, ga, gb, w, ROW_TILE)
    y = _moe(x1, route, routet, cnt[:, :N_GROUPS, 0], w, ROW_TILE)
    return y.reshape(b, s, d)


def kernel(x_prompt, x_sample, attn_norm_g, w_in, a_sink, b_q_norm_g, b_k_norm_g, w_oa, w_ob, w_out, ffn_norm_g,
           w_router_group, b_router_group, w_router_expert, b_router_expert, w_expert_gate, w_expert_up,
           w_expert_down, final_norm_g):
    assert attn_norm_g.shape[0] == 1, "single-layer trunk"
    w = _prepare(attn_norm_g[0], w_in[0], a_sink[0], b_q_norm_g[0], b_k_norm_g[0], w_oa[0], w_ob[0], w_out[0],
                 ffn_norm_g[0], w_router_group[0], b_router_group[0], w_router_expert[0], b_router_expert[0],
                 w_expert_gate[0], w_expert_up[0], w_expert_down[0], final_norm_g)
    return (_trunk(x_prompt, w), _trunk(x_sample, w))
```

```python
import functools

import jax
import jax.numpy as jnp
import numpy as np
from jax import lax
from jax.experimental import pallas as pl
from jax.experimental.pallas import tpu as pltpu

HEAD_DIM = 64
Q_HEADS = 8
KV_HEADS = 2
GROUP = Q_HEADS // KV_HEADS
Q_W = Q_HEADS * HEAD_DIM
KV_W = KV_HEADS * HEAD_DIM
WINDOW = 128
BLOCK = 128
GRID_W = 64
ROPE_THETA = 10000.0
N_GROUPS = 4
EXPERTS_PER_GROUP = 4
N_EXPERTS = N_GROUPS * EXPERTS_PER_GROUP
EPS = 1e-6
NEG_INF = -1e30
SCALE = HEAD_DIM ** -0.5
LOG2E = 1.4426950408889634
ONES_ROWS = 16
VT_ROWS = HEAD_DIM + ONES_ROWS
SCORE_DTYPE = jnp.bfloat16
LANES = 128

VMEM_LIMIT = 56 * 1024 * 1024

F32 = jnp.float32
BF16 = jnp.bfloat16


def _params(sem):
    return pltpu.CompilerParams(dimension_semantics=sem, vmem_limit_bytes=VMEM_LIMIT)


def _full(shape):
    return pl.BlockSpec(shape, lambda *_: (0,) * len(shape))


def _hi_lo(v):
    top = lax.bitcast_convert_type(lax.bitcast_convert_type(v, jnp.uint32) & jnp.uint32(0xFFFF0000), F32)
    return top.astype(BF16), (v - top).astype(BF16)


def _split_dot(y, ones_bd):
    hi, lo = _hi_lo(y)
    return jnp.dot(hi, ones_bd, preferred_element_type=F32) + jnp.dot(lo, ones_bd, preferred_element_type=F32)


def _pair_swap(y):
    n = y.shape[-1]
    lane = lax.broadcasted_iota(jnp.int32, y.shape, y.ndim - 1)
    nxt = pltpu.roll(y, n - 1, y.ndim - 1)
    prv = pltpu.roll(y, 1, y.ndim - 1)
    return jnp.where((lane & 1) == 0, nxt, prv)


def _norm_rope(y, gain, ones_bd, cos, sin):
    ms = _split_dot(y * y, ones_bd) * (1.0 / HEAD_DIM)
    yn = y * lax.rsqrt(ms + EPS) * gain
    return yn * cos + _pair_swap(yn) * sin


def _in_proj_kernel(x_ref, g_ref, w_ref, cos_ref, sin_ref, gq_ref, gk_ref, oq_ref, ok_ref,
                    qa_ref, ka_ref, va_ref, qb_ref, kb_ref, vb_ref, ga_ref, gb_ref):
    half = x_ref.shape[0] // 2
    d = x_ref.shape[1]
    for r in (0, half):
        rows = slice(r, r + half)
        x = x_ref[rows, :]
        ms = jnp.mean(x * x, axis=-1, keepdims=True)
        h = (x * lax.rsqrt(ms + EPS) * g_ref[...]).astype(BF16)

        def proj(lo, width, h=h):
            return jnp.dot(h, w_ref[:, lo:lo + width], preferred_element_type=F32)

        o = 0
        qa_ref[rows, :] = (proj(o, Q_W) * (SCALE * LOG2E)).astype(BF16); o += Q_W
        ka_ref[rows, :] = proj(o, KV_W).astype(BF16); o += KV_W
        va_ref[rows, :] = proj(o, KV_W).astype(BF16); o += KV_W
        cos = cos_ref[rows, :]
        sin = sin_ref[rows, :]
        qb = proj(o, Q_W); o += Q_W
        cos_q = jnp.concatenate([cos] * (Q_W // LANES), axis=1)
        sin_q = jnp.concatenate([sin] * (Q_W // LANES), axis=1)
        qb_ref[rows, :] = (_norm_rope(qb, gq_ref[...], oq_ref[...], cos_q, sin_q) * (SCALE * LOG2E)).astype(BF16)
        kb = proj(o, KV_W); o += KV_W
        kb_ref[rows, :] = _norm_rope(kb, gk_ref[...], ok_ref[...], cos, sin).astype(BF16)
        vb_ref[rows, :] = proj(o, KV_W).astype(BF16); o += KV_W
        ga_ref[rows, :] = jax.nn.sigmoid(proj(o, d)).astype(BF16); o += d
        gb_ref[rows, :] = jax.nn.sigmoid(proj(o, d)).astype(BF16)


def _rope_tables(seq_len):
    rows = seq_len // GRID_W
    row = jnp.repeat(jnp.arange(rows, dtype=F32), GRID_W)
    col = jnp.tile(jnp.arange(GRID_W, dtype=F32), rows)
    half = HEAD_DIM // 2
    inv = ROPE_THETA ** (-jnp.arange(0, half, 2, dtype=F32) / half)
    ang = jnp.concatenate([row[:, None] * inv, col[:, None] * inv], axis=-1)
    cos = jnp.repeat(jnp.cos(ang), 2, axis=-1)
    sin = jnp.repeat(jnp.sin(ang), 2, axis=-1) * jnp.tile(jnp.array([-1.0, 1.0], F32), half)
    reps = LANES // HEAD_DIM
    return jnp.tile(cos, (1, reps)), jnp.tile(sin, (1, reps))


def _block_ones(width):
    idx = np.arange(width) // HEAD_DIM
    return jnp.asarray(idx[:, None] == idx[None, :], dtype=BF16)


def _in_proj(x2, seq_len, w, tm):
    t, d = x2.shape
    in_w = w["w_in"].shape[1]
    cos, sin = _rope_tables(seq_len)
    nseq = seq_len // tm
    row = lambda i: (i, 0)
    pos = lambda i: (i % nseq, 0)
    outs = [(Q_W, BF16), (KV_W, BF16), (KV_W, BF16), (Q_W, BF16), (KV_W, BF16), (KV_W, BF16), (d, BF16), (d, BF16)]
    return pl.pallas_call(
        _in_proj_kernel,
        grid=(t // tm,),
        in_specs=[pl.BlockSpec((tm, d), row), _full((1, d)), _full((d, in_w)),
                  pl.BlockSpec((tm, LANES), pos), pl.BlockSpec((tm, LANES), pos),
                  _full((1, Q_W)), _full((1, KV_W)), _full((Q_W, Q_W)), _full((KV_W, KV_W))],
        out_specs=[pl.BlockSpec((tm, w), row) for w, _ in outs],
        out_shape=[jax.ShapeDtypeStruct((t, w), dt) for w, dt in outs],
        compiler_params=_params(("parallel",)),
        name="in_proj",
    )(x2, w["attn_g"], w["w_in"], cos, sin, w["gq"], w["gk"], _block_ones(Q_W), _block_ones(KV_W))


PAIRS = Q_HEADS // 2
N_EDGE = 4


def _win_bias():
    slopes = np.exp2(-8.0 * np.arange(1, Q_HEADS + 1, dtype=np.float32) / Q_HEADS).astype(np.float32)
    key = np.arange(3 * BLOCK)[:, None]
    dist = np.abs(np.arange(BLOCK)[None, :] - (key - BLOCK))
    tabs = []
    for edge in range(N_EDGE):
        alive = dist <= WINDOW
        if edge & 1:
            alive = alive & (key >= BLOCK)
        if edge & 2:
            alive = alive & (key < 2 * BLOCK)
        heads = [np.where(alive, -slopes[h] * np.float32(LOG2E) * dist.astype(np.float32), np.float32(NEG_INF))
                 for h in range(Q_HEADS)]
        tabs.append(np.stack([np.concatenate([heads[2 * p], heads[2 * p + 1]], axis=1) for p in range(PAIRS)]))
    return jnp.asarray(np.stack(tabs), dtype=F32)


def _win_attn_kernel(sink_ref, q_ref, kc_ref, kp_ref, kn_ref, vc_ref, vp_ref, vn_ref, bias_ref, o_ref,
                     kh_scr, vt_scr, qp_scr, s_scr, p_scr, ot_scr, *, blocks_per_seq, sub):
    i = pl.program_id(0)
    tq = q_ref.shape[0]
    ones = jnp.ones((ONES_ROWS, BLOCK), BF16)
    for off, n, kr, vr in ((0, BLOCK, kp_ref, vp_ref), (BLOCK, tq, kc_ref, vc_ref), (BLOCK + tq, BLOCK, kn_ref, vn_ref)):
        k = kr[...]
        vt = vr[...].astype(F32).T.astype(BF16)
        for g in range(KV_HEADS):
            kh_scr[g, off:off + n, :] = k[:, g * HEAD_DIM:(g + 1) * HEAD_DIM]
            vt_scr[g * VT_ROWS:g * VT_ROWS + HEAD_DIM, off:off + n] = vt[g * HEAD_DIM:(g + 1) * HEAD_DIM, :]
            for o in range(off, off + n, BLOCK):
                vt_scr[g * VT_ROWS + HEAD_DIM:(g + 1) * VT_ROWS, o:o + BLOCK] = ones
    qt = q_ref[...].astype(F32).T.astype(BF16)
    for j in range(sub):
        for p in range(PAIRS):
            for u in range(2):
                h = 2 * p + u
                qp_scr[j, p, :, u * BLOCK:(u + 1) * BLOCK] = qt[h * HEAD_DIM:(h + 1) * HEAD_DIM,
                                                                j * BLOCK:(j + 1) * BLOCK]
    lane = lax.broadcasted_iota(jnp.int32, (1, 2 * BLOCK), 1)
    for j in range(sub):
        x = j % 2
        blk = (i * sub + j) % blocks_per_seq
        edge = (blk == 0).astype(jnp.int32) + 2 * (blk == blocks_per_seq - 1).astype(jnp.int32)
        for p in range(PAIRS):
            g = (2 * p) // GROUP
            s = jnp.dot(kh_scr[g, j * BLOCK:(j + 3) * BLOCK, :], qp_scr[j, p], preferred_element_type=F32)
            s_scr[x, p] = s + bias_ref[edge, p]
        sink_terms = []
        for p in range(PAIRS):
            logits = s_scr[x, p]
            sink = jnp.where(lane < BLOCK, sink_ref[2 * p], sink_ref[2 * p + 1]) * LOG2E
            m = jnp.maximum(jnp.max(logits, axis=0, keepdims=True), sink)
            p_scr[x, p] = jnp.exp2(logits - m).astype(BF16)
            sink_terms.append(jnp.exp2(sink - m))
        for p in range(PAIRS):
            g = (2 * p) // GROUP
            ot = jnp.dot(vt_scr[g * VT_ROWS:(g + 1) * VT_ROWS, j * BLOCK:(j + 3) * BLOCK], p_scr[x, p],
                         preferred_element_type=F32)
            out = ot[0:HEAD_DIM] / (ot[HEAD_DIM:HEAD_DIM + 1] + sink_terms[p])
            for u in range(2):
                h = 2 * p + u
                ot_scr[h * HEAD_DIM:(h + 1) * HEAD_DIM, j * BLOCK:(j + 1) * BLOCK] = out[:, u * BLOCK:(u + 1) * BLOCK]
    o_ref[...] = ot_scr[...].T.astype(o_ref.dtype)


def _win_attn(qa, ka, va, sink, seq_len, tq):
    t = qa.shape[0]
    sub = tq // BLOCK
    nblk = t // BLOCK
    win = tq + 2 * BLOCK
    row = lambda i: (i, 0)
    prev = lambda i: (jnp.maximum(i * sub - 1, 0), 0)
    nxt = lambda i: (jnp.minimum((i + 1) * sub, nblk - 1), 0)
    kv_specs = [pl.BlockSpec((tq, KV_W), row), pl.BlockSpec((BLOCK, KV_W), prev), pl.BlockSpec((BLOCK, KV_W), nxt)]
    bias_spec = pl.BlockSpec((N_EDGE, PAIRS, 3 * BLOCK, 2 * BLOCK), lambda i: (0, 0, 0, 0),
                             pipeline_mode=pl.Buffered(1))
    return pl.pallas_call(
        functools.partial(_win_attn_kernel, blocks_per_seq=seq_len // BLOCK, sub=sub),
        grid=(t // tq,),
        in_specs=[pl.BlockSpec(memory_space=pltpu.SMEM), pl.BlockSpec((tq, Q_W), row)] + kv_specs + kv_specs
                 + [bias_spec],
        out_specs=pl.BlockSpec((tq, Q_W), row),
        out_shape=jax.ShapeDtypeStruct((t, Q_W), BF16),
        scratch_shapes=[pltpu.VMEM((KV_HEADS, win, HEAD_DIM), BF16), pltpu.VMEM((KV_HEADS * VT_ROWS, win), BF16),
                        pltpu.VMEM((sub, PAIRS, HEAD_DIM, 2 * BLOCK), BF16),
                        pltpu.VMEM((2, PAIRS, 3 * BLOCK, 2 * BLOCK), F32),
                        pltpu.VMEM((2, PAIRS, 3 * BLOCK, 2 * BLOCK), BF16), pltpu.VMEM((Q_W, tq), F32)],
        compiler_params=_params(("parallel",)),
        name="win_attn",
    )(sink, qa, ka, ka, ka, va, va, va, _win_bias())


def _glob_attn_kernel(q_ref, k_ref, v_ref, o_ref, kh_ref, vt_ref, qt_scr, s_scr, p_scr, cm_scr, m_scr, a_scr, acc_scr,
                      ot_scr, *, tk):
    tq = q_ref.shape[1]
    nk = k_ref.shape[1] // tk

    @pl.when(pl.program_id(1) == 0)
    def _():
        ones = jnp.ones((ONES_ROWS, tk), BF16)

        def fill(c, carry):
            off = pl.multiple_of(c * tk, tk)
            vt = v_ref[0, pl.ds(off, tk), :].astype(F32).T.astype(BF16)
            kc = k_ref[0, pl.ds(off, tk), :]
            for g in range(KV_HEADS):
                kh_ref[g, pl.ds(off, tk), :] = kc[:, g * HEAD_DIM:(g + 1) * HEAD_DIM]
                vt_ref[g * VT_ROWS:g * VT_ROWS + HEAD_DIM, pl.ds(off, tk)] = vt[g * HEAD_DIM:(g + 1) * HEAD_DIM, :]
                vt_ref[g * VT_ROWS + HEAD_DIM:(g + 1) * VT_ROWS, pl.ds(off, tk)] = ones
            return carry

        lax.fori_loop(0, nk, fill, 0)

    qt_scr[...] = q_ref[0].astype(F32).T.astype(BF16)

    for g in range(KV_HEADS):
        heads = range(g * GROUP, (g + 1) * GROUP)

        def stage_a(c, x, heads=heads, g=g):
            kc = kh_ref[g, pl.ds(pl.multiple_of(c * tk, tk), tk), :]
            for i, h in enumerate(heads):
                s = jnp.dot(kc, qt_scr[h * HEAD_DIM:(h + 1) * HEAD_DIM, :], preferred_element_type=F32)
                s_scr[x, i] = s.astype(s_scr.dtype)
                cm_scr[x, i] = jnp.max(s, axis=0, keepdims=True)

        def stage_b(x):
            for i in range(GROUP):
                m = m_scr[i]
                m_new = jnp.maximum(m, cm_scr[x, i])
                a_scr[x, i] = jnp.exp2(m - m_new)
                m_scr[i] = m_new
                p_scr[x, i] = jnp.exp2(s_scr[x, i] - m_new.astype(s_scr.dtype)).astype(BF16)

        def stage_c(c, x, g=g):
            vt = vt_ref[g * VT_ROWS:(g + 1) * VT_ROWS, pl.ds(pl.multiple_of(c * tk, tk), tk)]
            for i in range(GROUP):
                acc_scr[i] = a_scr[x, i] * acc_scr[i] + jnp.dot(vt, p_scr[x, i], preferred_element_type=F32)

        def step(c, x, last=False):
            stage_c(c - 1, 1 - x)
            stage_b(x)
            if not last:
                stage_a(c + 1, 1 - x)

        for i in range(GROUP):
            m_scr[i] = jnp.full((1, tq), -jnp.inf, F32)
            acc_scr[i] = jnp.zeros((VT_ROWS, tq), F32)
        stage_a(0, 0)
        stage_b(0)
        stage_a(1, 1)

        def pair(j, carry):
            step(2 * j + 1, 1)
            step(2 * j + 2, 0)
            return carry

        lax.fori_loop(0, (nk - 2) // 2, pair, 0, unroll=8)
        step(nk - 1, 1, last=True)
        stage_c(nk - 1, 1)
        for i, h in enumerate(heads):
            acc = acc_scr[i]
            ot_scr[h * HEAD_DIM:(h + 1) * HEAD_DIM, :] = acc[0:HEAD_DIM] / acc[HEAD_DIM:HEAD_DIM + 1]
    o_ref[0] = ot_scr[...].T.astype(o_ref.dtype)


def _glob_attn(qb, kb, vb, tq, tk):
    b, s, _ = qb.shape
    assert s % (2 * tk) == 0, "the chunk pipeline walks key chunks in pairs"
    return pl.pallas_call(
        functools.partial(_glob_attn_kernel, tk=tk),
        grid=(b, s // tq),
        in_specs=[pl.BlockSpec((1, tq, Q_W), lambda bi, qi: (bi, qi, 0)),
                  pl.BlockSpec((1, s, KV_W), lambda bi, qi: (bi, 0, 0)),
                  pl.BlockSpec((1, s, KV_W), lambda bi, qi: (bi, 0, 0))],
        out_specs=pl.BlockSpec((1, tq, Q_W), lambda bi, qi: (bi, qi, 0)),
        out_shape=jax.ShapeDtypeStruct((b, s, Q_W), BF16),
        scratch_shapes=[pltpu.VMEM((KV_HEADS, s, HEAD_DIM), BF16), pltpu.VMEM((KV_HEADS * VT_ROWS, s), BF16),
                        pltpu.VMEM((Q_W, tq), BF16),
                        pltpu.VMEM((2, GROUP, tk, tq), SCORE_DTYPE), pltpu.VMEM((2, GROUP, tk, tq), BF16),
                        pltpu.VMEM((2, GROUP, 1, tq), F32), pltpu.VMEM((GROUP, 1, tq), F32),
                        pltpu.VMEM((2, GROUP, 1, tq), F32), pltpu.VMEM((GROUP, VT_ROWS, tq), F32),
                        pltpu.VMEM((Q_W, tq), F32)],
        compiler_params=_params(("arbitrary", "arbitrary")),
        name="glob_attn",
    )(qb, kb, vb)


def _first_argmax(vals, idx, big):
    m = jnp.max(vals, axis=0, keepdims=True)
    return m, jnp.min(jnp.where(vals == m, idx, big), axis=0, keepdims=True)


def _route(lt):
    n = lt.shape[1]
    gl = lt[0:N_GROUPS]
    idx = lax.broadcasted_iota(jnp.int32, (N_GROUPS, n), 0)
    gmax, g_sel = _first_argmax(gl, idx, N_GROUPS)
    g_w = 1.0 / jnp.sum(jnp.exp(gl - gmax), axis=0, keepdims=True)
    e_sel = jnp.zeros((EXPERTS_PER_GROUP, n), F32)
    for g in range(N_GROUPS):
        lo = N_GROUPS + g * EXPERTS_PER_GROUP
        e_sel = jnp.where(g_sel == g, lt[lo:lo + EXPERTS_PER_GROUP], e_sel)
    ex = jnp.exp(e_sel - jnp.max(e_sel, axis=0, keepdims=True))
    e_prob = ex / jnp.sum(ex, axis=0, keepdims=True)
    p1, i1 = _first_argmax(e_prob, idx, EXPERTS_PER_GROUP)
    rest = jnp.where(idx == i1, -1.0, e_prob)
    p2, i2 = _first_argmax(rest, idx, EXPERTS_PER_GROUP)
    tot = p1 + p2
    row = lax.broadcasted_iota(jnp.int32, (2 * N_GROUPS, n), 0)
    out = jnp.where(row == g_sel, 1.0, 0.0)
    out = out + jnp.where(row == N_GROUPS + i1, g_w * (p1 / tot), 0.0)
    return out + jnp.where(row == N_GROUPS + i2, g_w * (p2 / tot), 0.0)


def _merge_kernel(x_ref, ya_ref, yb_ref, ga_ref, gb_ref, woa_ref, wob_ref, wout_ref, gf_ref, wr_ref, br_ref,
                  x1_ref, route_ref, routet_ref, cnt_ref):
    tm = x_ref.shape[0]
    half = tm // 2
    cnt = jnp.zeros((2 * N_GROUPS, 1), F32)
    for r in (0, half):
        rows = slice(r, r + half)
        a = jnp.dot(ya_ref[rows, :], woa_ref[...], preferred_element_type=F32)
        b = jnp.dot(yb_ref[rows, :], wob_ref[...], preferred_element_type=F32)
        merged = ga_ref[rows, :].astype(F32) * a + gb_ref[rows, :].astype(F32) * b
        x1 = x_ref[rows, :] + jnp.dot(merged.astype(BF16), wout_ref[...], preferred_element_type=F32)
        x1_ref[rows, :] = x1
        t = _rmsnorm(x1, gf_ref[...])
        t_hi, t_lo = _hi_lo(t)
        big = jnp.dot(t_hi, wr_ref[...], preferred_element_type=F32)
        small = jnp.dot(t_lo, wr_ref[:, 0:LANES], preferred_element_type=F32)
        logits = big[:, 0:LANES] + big[:, LANES:2 * LANES] + small + br_ref[...]
        block = _route(logits.T)
        route_t = jnp.concatenate([block, jnp.zeros((LANES - block.shape[0], half), F32)], axis=0)
        routet_ref[:, rows] = route_t
        route_ref[rows, :] = route_t.T
        cnt = cnt + jnp.sum(block, axis=1, keepdims=True)
    cnt_ref[0] = jnp.broadcast_to(cnt, cnt_ref.shape[1:]).astype(jnp.int32)


def _router_weights(w_rg, b_rg, w_re, b_re):
    d = w_rg.shape[0]
    wr = jnp.zeros((d, LANES), F32).at[:, :N_GROUPS].set(w_rg).at[:, N_GROUPS:N_GROUPS + N_EXPERTS].set(w_re)
    br = jnp.zeros((1, LANES), F32).at[0, :N_GROUPS].set(b_rg).at[0, N_GROUPS:N_GROUPS + N_EXPERTS].set(b_re)
    return jnp.concatenate(_hi_lo(wr), axis=1), br


def _merge(x2, ya, yb, ga, gb, w, tm):
    t, d = x2.shape
    row = lambda i: (i, 0)
    return pl.pallas_call(
        _merge_kernel,
        grid=(t // tm,),
        in_specs=[pl.BlockSpec((tm, d), row), pl.BlockSpec((tm, Q_W), row), pl.BlockSpec((tm, Q_W), row),
                  pl.BlockSpec((tm, d), row), pl.BlockSpec((tm, d), row),
                  _full((Q_W, d)), _full((Q_W, d)), _full((d, d)), _full((1, d)), _full((d, 2 * LANES)),
                  _full((1, LANES))],
        out_specs=[pl.BlockSpec((tm, d), row), pl.BlockSpec((tm, LANES), row),
                   pl.BlockSpec((LANES, tm), lambda i: (0, i)), pl.BlockSpec((1, 8, LANES), lambda i: (i, 0, 0))],
        out_shape=[jax.ShapeDtypeStruct((t, d), F32), jax.ShapeDtypeStruct((t, LANES), F32),
                   jax.ShapeDtypeStruct((LANES, t), F32), jax.ShapeDtypeStruct((t // tm, 8, LANES), jnp.int32)],
        compiler_params=_params(("parallel",)),
        name="merge",
    )(x2, ya, yb, ga, gb, w["w_oa"], w["w_ob"], w["w_out"], w["ffn_g"], w["w_router"], w["b_router"])


ROW_ALIGN = 128
BIG_BLOCK = 2 * ROW_ALIGN
GATHER_BLOCK = 4 * ROW_ALIGN
MAX_WHOLE = 4


def _rmsnorm(x, g):
    ms = jnp.mean(x * x, axis=-1, keepdims=True)
    return x * lax.rsqrt(ms + EPS) * g


def _moe_kernel(cnt_ref, x1_ref, route_ref, routet_ref, gf_ref, wg_ref, wu_ref, wd_ref, gfin_ref, o_ref,
                p_scr, q_scr, ts_scr, ws_scr):
    i = pl.program_id(0)
    g = pl.program_id(1)
    tm = x1_ref.shape[0]
    rows = p_scr.shape[0]
    offs, off = [], 0
    for gg in range(N_GROUPS):
        offs.append(off)
        off = off + (cnt_ref[i, gg] + (ROW_ALIGN - 1)) // ROW_ALIGN * ROW_ALIGN

    @pl.when(g == 0)
    def _():
        route = route_ref[...]
        lane = lax.broadcasted_iota(jnp.int32, (tm, LANES), 1)
        onehot = jnp.where(lane < N_GROUPS, route, 0.0)
        r_i = lax.broadcasted_iota(jnp.int32, (tm, tm), 0)
        c_i = lax.broadcasted_iota(jnp.int32, (tm, tm), 1)
        before = jnp.dot(jnp.where(c_i < r_i, 1.0, 0.0).astype(BF16), onehot.astype(BF16),
                         preferred_element_type=F32)
        goff = jnp.zeros((tm, LANES), F32)
        for gg in range(N_GROUPS):
            goff = jnp.where(lane == gg, jnp.asarray(offs[gg], jnp.int32).astype(F32), goff)
        dest_col = jnp.sum((before + goff) * onehot, axis=1, keepdims=True).astype(jnp.int32)
        q_scr[...] = jnp.where(dest_col == lax.broadcasted_iota(jnp.int32, (tm, rows), 1), 1.0, 0.0).astype(BF16)
        sub = lax.broadcasted_iota(jnp.int32, (16, tm), 0)
        onehot_t = jnp.where(sub < N_GROUPS, routet_ref[0:16, :], 0.0)
        before_t = jnp.dot(onehot_t.astype(BF16), jnp.where(r_i < c_i, 1.0, 0.0).astype(BF16),
                           preferred_element_type=F32)
        goff_t = jnp.zeros((16, tm), F32)
        for gg in range(N_GROUPS):
            goff_t = jnp.where(sub == gg, jnp.asarray(offs[gg], jnp.int32).astype(F32), goff_t)
        dest_row = jnp.sum((before_t + goff_t) * onehot_t, axis=0, keepdims=True).astype(jnp.int32)
        for r0 in range(0, rows, BIG_BLOCK):
            rid = lax.broadcasted_iota(jnp.int32, (BIG_BLOCK, tm), 0) + r0
            p_scr[r0:r0 + BIG_BLOCK, :] = jnp.where(dest_row == rid, 1.0, 0.0).astype(BF16)
        t = _rmsnorm(x1_ref[...], gf_ref[...]).astype(BF16)
        r_hi_lo = jnp.concatenate(_hi_lo(route), axis=1)
        for r0 in range(0, rows, GATHER_BLOCK):
            pb = p_scr[r0:r0 + GATHER_BLOCK, :]
            ts_scr[r0:r0 + GATHER_BLOCK, :] = jnp.dot(pb, t, preferred_element_type=F32).astype(BF16)
            w2 = jnp.dot(pb, r_hi_lo, preferred_element_type=F32)
            ws_scr[r0:r0 + GATHER_BLOCK, :] = w2[:, 0:LANES] + w2[:, LANES:2 * LANES]

    def ffn(r, m):
        tb = ts_scr[pl.ds(pl.multiple_of(r, ROW_ALIGN), m), :]
        wb = ws_scr[pl.ds(pl.multiple_of(r, ROW_ALIGN), m), :]
        y = jnp.zeros((m, o_ref.shape[1]), F32)
        for e in range(EXPERTS_PER_GROUP):
            a = jax.nn.silu(jnp.dot(tb, wg_ref[0, e], preferred_element_type=F32)) * jnp.dot(
                tb, wu_ref[0, e], preferred_element_type=F32)
            y = y + wb[:, N_GROUPS + e:N_GROUPS + e + 1] * jnp.dot(a.astype(BF16), wd_ref[0, e],
                                                                   preferred_element_type=F32)
        ts_scr[pl.ds(pl.multiple_of(r, ROW_ALIGN), m), :] = y.astype(BF16)

    start = offs[0]
    for gg in range(1, N_GROUPS):
        start = jnp.where(g == gg, offs[gg], start)
    n_small = (cnt_ref[i, g] + (ROW_ALIGN - 1)) // ROW_ALIGN

    for units in range(1, MAX_WHOLE + 1):
        @pl.when(n_small == units)
        def _(units=units):
            ffn(start, units * ROW_ALIGN)

    @pl.when(n_small > MAX_WHOLE)
    def _():
        n_big = n_small // 2

        def big(j, carry):
            ffn(start + j * BIG_BLOCK, BIG_BLOCK)
            return carry

        lax.fori_loop(0, n_big, big, 0)

        @pl.when(n_small % 2 == 1)
        def _():
            ffn(start + n_big * BIG_BLOCK, ROW_ALIGN)

    @pl.when(g == N_GROUPS - 1)
    def _():
        x2 = x1_ref[...] + jnp.dot(q_scr[...], ts_scr[...], preferred_element_type=F32)
        o_ref[...] = _rmsnorm(x2, gfin_ref[...])


def _moe(x1, route, routet, cnt, w, tm):
    t, d = x1.shape
    de = w["w_eg"].shape[-1]
    rows = tm + N_GROUPS * ROW_ALIGN
    row = lambda i, g, c: (i, 0)
    const = lambda i, g, c: (0, 0)
    wspec = lambda a, b: pl.BlockSpec((1, EXPERTS_PER_GROUP, a, b), lambda i, g, c: (g, 0, 0, 0))
    return pl.pallas_call(
        _moe_kernel,
        grid_spec=pltpu.PrefetchScalarGridSpec(
            num_scalar_prefetch=1,
            grid=(t // tm, N_GROUPS),
            in_specs=[pl.BlockSpec((tm, d), row), pl.BlockSpec((tm, LANES), row),
                      pl.BlockSpec((LANES, tm), lambda i, g, c: (0, i)), pl.BlockSpec((1, d), const),
                      wspec(d, de), wspec(d, de), wspec(de, d), pl.BlockSpec((1, d), const)],
            out_specs=pl.BlockSpec((tm, d), row),
            scratch_shapes=[pltpu.VMEM((rows, tm), BF16), pltpu.VMEM((tm, rows), BF16), pltpu.VMEM((rows, d), BF16),
                            pltpu.VMEM((rows, LANES), F32)]),
        out_shape=jax.ShapeDtypeStruct((t, d), F32),
        compiler_params=_params(("arbitrary", "arbitrary")),
        name="moe",
    )(cnt, x1, route, routet, w["ffn_g"], w["w_eg"], w["w_eu"], w["w_ed"], w["final_g"])


ROW_TILE = 1024
WIN_TILE = 512
GLOB_TQ = 256
GLOB_TK = 256


def _prepare(attn_g, w_in, sink, gq, gk, w_oa, w_ob, w_out, ffn_g, w_rg, b_rg, w_re, b_re, w_eg, w_eu, w_ed,
             final_g):
    d = w_in.shape[0]
    w_router, b_router = _router_weights(w_rg, b_rg, w_re, b_re)
    grouped = lambda a: a.astype(BF16).reshape((N_GROUPS, EXPERTS_PER_GROUP) + a.shape[1:])
    return dict(
        attn_g=attn_g.reshape(1, d), w_in=w_in.astype(BF16), sink=sink.astype(F32),
        gq=jnp.tile(gq.astype(F32), Q_HEADS).reshape(1, Q_W), gk=jnp.tile(gk.astype(F32), KV_HEADS).reshape(1, KV_W),
        w_oa=w_oa.astype(BF16), w_ob=w_ob.astype(BF16), w_out=w_out.astype(BF16), ffn_g=ffn_g.reshape(1, d),
        w_router=w_router, b_router=b_router, w_eg=grouped(w_eg), w_eu=grouped(w_eu), w_ed=grouped(w_ed),
        final_g=final_g.reshape(1, d))


def _trunk(x, w):
    b, s, d = x.shape
    t = b * s
    assert s % ROW_TILE == 0 and s % (2 * GLOB_TK) == 0, "sequence length must be a multiple of the row tile"
    x2 = x.reshape(t, d)
    qa, ka, va, qb, kb, vb, ga, gb = _in_proj(x2, s, w, ROW_TILE)
    ya = _win_attn(qa, ka, va, w["sink"], s, WIN_TILE)
    yb = _glob_attn(qb.reshape(b, s, Q_W), kb.reshape(b, s, KV_W), vb.reshape(b, s, KV_W),
                    GLOB_TQ, GLOB_TK).reshape(t, Q_W)
    x1, route, routet, cnt = _merge(x2, ya, yb, ga, gb, w, ROW_TILE)
    y = _moe(x1, route, routet, cnt[:, :N_GROUPS, 0], w, ROW_TILE)
    return y.reshape(b, s, d)


def kernel(x_prompt, x_sample, attn_norm_g, w_in, a_sink, b_q_norm_g, b_k_norm_g, w_oa, w_ob, w_out, ffn_norm_g,
           w_router_group, b_router_group, w_router_expert, b_router_expert, w_expert_gate, w_expert_up,
           w_expert_down, final_norm_g):
    assert attn_norm_g.shape[0] == 1, "single-layer trunk"
    w = _prepare(attn_norm_g[0], w_in[0], a_sink[0], b_q_norm_g[0], b_k_norm_g[0], w_oa[0], w_ob[0], w_out[0],
                 ffn_norm_g[0], w_router_group[0], b_router_group[0], w_router_expert[0], b_router_expert[0],
                 w_expert_gate[0], w_expert_up[0], w_expert_down[0], final_norm_g)
    return (_trunk(x_prompt, w), _trunk(x_sample, w))
```

```python
import functools

import jax
import jax.numpy as jnp
import numpy as np
from jax import lax
from jax.experimental import pallas as pl
from jax.experimental.pallas import tpu as pltpu

HEAD_DIM = 64
Q_HEADS = 8
KV_HEADS = 2
GROUP = Q_HEADS // KV_HEADS
Q_W = Q_HEADS * HEAD_DIM
KV_W = KV_HEADS * HEAD_DIM
WINDOW = 128
BLOCK = 128
GRID_W = 64
ROPE_THETA = 10000.0
N_GROUPS = 4
EXPERTS_PER_GROUP = 4
N_EXPERTS = N_GROUPS * EXPERTS_PER_GROUP
EPS = 1e-6
NEG_INF = -1e30
SCALE = HEAD_DIM ** -0.5
LOG2E = 1.4426950408889634
ONES_ROWS = 16
VT_ROWS = HEAD_DIM + ONES_ROWS
SCORE_DTYPE = jnp.bfloat16
LANES = 128
IN_PROJ_SUBTILES = 1
MERGE_SUBTILES = 1

VMEM_LIMIT = 56 * 1024 * 1024

F32 = jnp.float32
BF16 = jnp.bfloat16


def _params(sem):
    return pltpu.CompilerParams(dimension_semantics=sem, vmem_limit_bytes=VMEM_LIMIT)


def _full(shape):
    return pl.BlockSpec(shape, lambda *_: (0,) * len(shape))


def _hi_lo(v):
    top = lax.bitcast_convert_type(lax.bitcast_convert_type(v, jnp.uint32) & jnp.uint32(0xFFFF0000), F32)
    return top.astype(BF16), (v - top).astype(BF16)


def _split_dot(y, ones_bd):
    hi, lo = _hi_lo(y)
    return jnp.dot(hi, ones_bd, preferred_element_type=F32) + jnp.dot(lo, ones_bd, preferred_element_type=F32)


def _pair_swap(y):
    n = y.shape[-1]
    lane = lax.broadcasted_iota(jnp.int32, y.shape, y.ndim - 1)
    nxt = pltpu.roll(y, n - 1, y.ndim - 1)
    prv = pltpu.roll(y, 1, y.ndim - 1)
    return jnp.where((lane & 1) == 0, nxt, prv)


def _norm_rope(y, gain, ones_bd, cos, sin):
    ms = _split_dot(y * y, ones_bd) * (1.0 / HEAD_DIM)
    yn = y * lax.rsqrt(ms + EPS) * gain
    return yn * cos + _pair_swap(yn) * sin


def _in_proj_kernel(x_ref, g_ref, w_ref, cos_ref, sin_ref, gq_ref, gk_ref, oq_ref, ok_ref,
                    qa_ref, ka_ref, va_ref, qb_ref, kb_ref, vb_ref, ga_ref, gb_ref):
    half = x_ref.shape[0] // IN_PROJ_SUBTILES
    d = x_ref.shape[1]
    for r in range(0, x_ref.shape[0], half):
        rows = slice(r, r + half)
        x = x_ref[rows, :]
        ms = jnp.mean(x * x, axis=-1, keepdims=True)
        h = (x * lax.rsqrt(ms + EPS) * g_ref[...]).astype(BF16)

        def proj(lo, width, h=h):
            return jnp.dot(h, w_ref[:, lo:lo + width], preferred_element_type=F32)

        o = 0
        qa_ref[rows, :] = (proj(o, Q_W) * (SCALE * LOG2E)).astype(BF16); o += Q_W
        ka_ref[rows, :] = proj(o, KV_W).astype(BF16); o += KV_W
        va_ref[rows, :] = proj(o, KV_W).astype(BF16); o += KV_W
        cos = cos_ref[rows, :]
        sin = sin_ref[rows, :]
        qb = proj(o, Q_W); o += Q_W
        cos_q = jnp.concatenate([cos] * (Q_W // LANES), axis=1)
        sin_q = jnp.concatenate([sin] * (Q_W // LANES), axis=1)
        qb_ref[rows, :] = (_norm_rope(qb, gq_ref[...], oq_ref[...], cos_q, sin_q) * (SCALE * LOG2E)).astype(BF16)
        kb = proj(o, KV_W); o += KV_W
        kb_ref[rows, :] = _norm_rope(kb, gk_ref[...], ok_ref[...], cos, sin).astype(BF16)
        vb_ref[rows, :] = proj(o, KV_W).astype(BF16); o += KV_W
        ga_ref[rows, :] = jax.nn.sigmoid(proj(o, d)).astype(BF16); o += d
        gb_ref[rows, :] = jax.nn.sigmoid(proj(o, d)).astype(BF16)


def _rope_tables(seq_len):
    rows = seq_len // GRID_W
    row = jnp.repeat(jnp.arange(rows, dtype=F32), GRID_W)
    col = jnp.tile(jnp.arange(GRID_W, dtype=F32), rows)
    half = HEAD_DIM // 2
    inv = ROPE_THETA ** (-jnp.arange(0, half, 2, dtype=F32) / half)
    ang = jnp.concatenate([row[:, None] * inv, col[:, None] * inv], axis=-1)
    cos = jnp.repeat(jnp.cos(ang), 2, axis=-1)
    sin = jnp.repeat(jnp.sin(ang), 2, axis=-1) * jnp.tile(jnp.array([-1.0, 1.0], F32), half)
    reps = LANES // HEAD_DIM
    return jnp.tile(cos, (1, reps)), jnp.tile(sin, (1, reps))


def _block_ones(width):
    idx = np.arange(width) // HEAD_DIM
    return jnp.asarray(idx[:, None] == idx[None, :], dtype=BF16)


def _in_proj(x2, seq_len, w, tm):
    t, d = x2.shape
    in_w = w["w_in"].shape[1]
    cos, sin = _rope_tables(seq_len)
    nseq = seq_len // tm
    row = lambda i: (i, 0)
    pos = lambda i: (i % nseq, 0)
    outs = [(Q_W, BF16), (KV_W, BF16), (KV_W, BF16), (Q_W, BF16), (KV_W, BF16), (KV_W, BF16), (d, BF16), (d, BF16)]
    return pl.pallas_call(
        _in_proj_kernel,
        grid=(t // tm,),
        in_specs=[pl.BlockSpec((tm, d), row), _full((1, d)), _full((d, in_w)),
                  pl.BlockSpec((tm, LANES), pos), pl.BlockSpec((tm, LANES), pos),
                  _full((1, Q_W)), _full((1, KV_W)), _full((Q_W, Q_W)), _full((KV_W, KV_W))],
        out_specs=[pl.BlockSpec((tm, w), row) for w, _ in outs],
        out_shape=[jax.ShapeDtypeStruct((t, w), dt) for w, dt in outs],
        compiler_params=_params(("parallel",)),
        name="in_proj",
    )(x2, w["attn_g"], w["w_in"], cos, sin, w["gq"], w["gk"], _block_ones(Q_W), _block_ones(KV_W))


PAIRS = Q_HEADS // 2
N_EDGE = 4


def _win_bias():
    slopes = np.exp2(-8.0 * np.arange(1, Q_HEADS + 1, dtype=np.float32) / Q_HEADS).astype(np.float32)
    key = np.arange(3 * BLOCK)[:, None]
    dist = np.abs(np.arange(BLOCK)[None, :] - (key - BLOCK))
    tabs = []
    for edge in range(N_EDGE):
        alive = dist <= WINDOW
        if edge & 1:
            alive = alive & (key >= BLOCK)
        if edge & 2:
            alive = alive & (key < 2 * BLOCK)
        heads = [np.where(alive, -slopes[h] * np.float32(LOG2E) * dist.astype(np.float32), np.float32(NEG_INF))
                 for h in range(Q_HEADS)]
        tabs.append(np.stack([np.concatenate([heads[2 * p], heads[2 * p + 1]], axis=1) for p in range(PAIRS)]))
    return jnp.asarray(np.stack(tabs), dtype=F32)


def _win_attn_kernel(sink_ref, q_ref, kc_ref, kp_ref, kn_ref, vc_ref, vp_ref, vn_ref, bias_ref, o_ref,
                     kh_scr, vt_scr, qp_scr, s_scr, p_scr, ot_scr, *, blocks_per_seq, sub):
    i = pl.program_id(0)
    tq = q_ref.shape[0]
    ones = jnp.ones((ONES_ROWS, BLOCK), BF16)
    for off, n, kr, vr in ((0, BLOCK, kp_ref, vp_ref), (BLOCK, tq, kc_ref, vc_ref), (BLOCK + tq, BLOCK, kn_ref, vn_ref)):
        k = kr[...]
        vt = vr[...].astype(F32).T.astype(BF16)
        for g in range(KV_HEADS):
            kh_scr[g, off:off + n, :] = k[:, g * HEAD_DIM:(g + 1) * HEAD_DIM]
            vt_scr[g * VT_ROWS:g * VT_ROWS + HEAD_DIM, off:off + n] = vt[g * HEAD_DIM:(g + 1) * HEAD_DIM, :]
            for o in range(off, off + n, BLOCK):
                vt_scr[g * VT_ROWS + HEAD_DIM:(g + 1) * VT_ROWS, o:o + BLOCK] = ones
    qt = q_ref[...].astype(F32).T.astype(BF16)
    for j in range(sub):
        for p in range(PAIRS):
            for u in range(2):
                h = 2 * p + u
                qp_scr[j, p, :, u * BLOCK:(u + 1) * BLOCK] = qt[h * HEAD_DIM:(h + 1) * HEAD_DIM,
                                                                j * BLOCK:(j + 1) * BLOCK]
    lane = lax.broadcasted_iota(jnp.int32, (1, 2 * BLOCK), 1)
    for j in range(sub):
        x = j % 2
        blk = (i * sub + j) % blocks_per_seq
        edge = (blk == 0).astype(jnp.int32) + 2 * (blk == blocks_per_seq - 1).astype(jnp.int32)
        for p in range(PAIRS):
            g = (2 * p) // GROUP
            s = jnp.dot(kh_scr[g, j * BLOCK:(j + 3) * BLOCK, :], qp_scr[j, p], preferred_element_type=F32)
            s_scr[x, p] = s + bias_ref[edge, p]
        sink_terms = []
        for p in range(PAIRS):
            logits = s_scr[x, p]
            sink = jnp.where(lane < BLOCK, sink_ref[2 * p], sink_ref[2 * p + 1]) * LOG2E
            m = jnp.maximum(jnp.max(logits, axis=0, keepdims=True), sink)
            p_scr[x, p] = jnp.exp2(logits - m).astype(BF16)
            sink_terms.append(jnp.exp2(sink - m))
        for p in range(PAIRS):
            g = (2 * p) // GROUP
            ot = jnp.dot(vt_scr[g * VT_ROWS:(g + 1) * VT_ROWS, j * BLOCK:(j + 3) * BLOCK], p_scr[x, p],
                         preferred_element_type=F32)
            out = ot[0:HEAD_DIM] / (ot[HEAD_DIM:HEAD_DIM + 1] + sink_terms[p])
            for u in range(2):
                h = 2 * p + u
                ot_scr[h * HEAD_DIM:(h + 1) * HEAD_DIM, j * BLOCK:(j + 1) * BLOCK] = out[:, u * BLOCK:(u + 1) * BLOCK]
    o_ref[...] = ot_scr[...].T.astype(o_ref.dtype)


def _win_attn(qa, ka, va, sink, seq_len, tq):
    t = qa.shape[0]
    sub = tq // BLOCK
    nblk = t // BLOCK
    win = tq + 2 * BLOCK
    row = lambda i: (i, 0)
    prev = lambda i: (jnp.maximum(i * sub - 1, 0), 0)
    nxt = lambda i: (jnp.minimum((i + 1) * sub, nblk - 1), 0)
    kv_specs = [pl.BlockSpec((tq, KV_W), row), pl.BlockSpec((BLOCK, KV_W), prev), pl.BlockSpec((BLOCK, KV_W), nxt)]
    bias_spec = pl.BlockSpec((N_EDGE, PAIRS, 3 * BLOCK, 2 * BLOCK), lambda i: (0, 0, 0, 0),
                             pipeline_mode=pl.Buffered(1))
    return pl.pallas_call(
        functools.partial(_win_attn_kernel, blocks_per_seq=seq_len // BLOCK, sub=sub),
        grid=(t // tq,),
        in_specs=[pl.BlockSpec(memory_space=pltpu.SMEM), pl.BlockSpec((tq, Q_W), row)] + kv_specs + kv_specs
                 + [bias_spec],
        out_specs=pl.BlockSpec((tq, Q_W), row),
        out_shape=jax.ShapeDtypeStruct((t, Q_W), BF16),
        scratch_shapes=[pltpu.VMEM((KV_HEADS, win, HEAD_DIM), BF16), pltpu.VMEM((KV_HEADS * VT_ROWS, win), BF16),
                        pltpu.VMEM((sub, PAIRS, HEAD_DIM, 2 * BLOCK), BF16),
                        pltpu.VMEM((2, PAIRS, 3 * BLOCK, 2 * BLOCK), F32),
                        pltpu.VMEM((2, PAIRS, 3 * BLOCK, 2 * BLOCK), BF16), pltpu.VMEM((Q_W, tq), F32)],
        compiler_params=_params(("parallel",)),
        name="win_attn",
    )(sink, qa, ka, ka, ka, va, va, va, _win_bias())


def _glob_attn_kernel(q_ref, k_ref, v_ref, o_ref, kh_ref, vt_ref, qt_scr, s_scr, p_scr, cm_scr, m_scr, a_scr, acc_scr,
                      ot_scr, *, tk):
    tq = q_ref.shape[1]
    nk = k_ref.shape[1] // tk

    @pl.when(pl.program_id(1) == 0)
    def _():
        ones = jnp.ones((ONES_ROWS, tk), BF16)

        def fill(c, carry):
            off = pl.multiple_of(c * tk, tk)
            vt = v_ref[0, pl.ds(off, tk), :].astype(F32).T.astype(BF16)
            kc = k_ref[0, pl.ds(off, tk), :]
            for g in range(KV_HEADS):
                kh_ref[g, pl.ds(off, tk), :] = kc[:, g * HEAD_DIM:(g + 1) * HEAD_DIM]
                vt_ref[g * VT_ROWS:g * VT_ROWS + HEAD_DIM, pl.ds(off, tk)] = vt[g * HEAD_DIM:(g + 1) * HEAD_DIM, :]
                vt_ref[g * VT_ROWS + HEAD_DIM:(g + 1) * VT_ROWS, pl.ds(off, tk)] = ones
            return carry

        lax.fori_loop(0, nk, fill, 0)

    qt_scr[...] = q_ref[0].astype(F32).T.astype(BF16)

    for g in range(KV_HEADS):
        heads = range(g * GROUP, (g + 1) * GROUP)

        def stage_a(c, x, heads=heads, g=g):
            kc = kh_ref[g, pl.ds(pl.multiple_of(c * tk, tk), tk), :]
            for i, h in enumerate(heads):
                s = jnp.dot(kc, qt_scr[h * HEAD_DIM:(h + 1) * HEAD_DIM, :], preferred_element_type=F32)
                s_scr[x, i] = s.astype(s_scr.dtype)
                cm_scr[x, i] = jnp.max(s, axis=0, keepdims=True)

        def stage_b(x):
            for i in range(GROUP):
                m = m_scr[i]
                m_new = jnp.maximum(m, cm_scr[x, i])
                a_scr[x, i] = jnp.exp2(m - m_new)
                m_scr[i] = m_new
                p_scr[x, i] = jnp.exp2(s_scr[x, i] - m_new.astype(s_scr.dtype)).astype(BF16)

        def stage_c(c, x, g=g):
            vt = vt_ref[g * VT_ROWS:(g + 1) * VT_ROWS, pl.ds(pl.multiple_of(c * tk, tk), tk)]
            for i in range(GROUP):
                acc_scr[i] = a_scr[x, i] * acc_scr[i] + jnp.dot(vt, p_scr[x, i], preferred_element_type=F32)

        def step(c, x, last=False):
            stage_c(c - 1, 1 - x)
            stage_b(x)
            if not last:
                stage_a(c + 1, 1 - x)

        for i in range(GROUP):
            m_scr[i] = jnp.full((1, tq), -jnp.inf, F32)
            acc_scr[i] = jnp.zeros((VT_ROWS, tq), F32)
        stage_a(0, 0)
        stage_b(0)
        stage_a(1, 1)

        def pair(j, carry):
            step(2 * j + 1, 1)
            step(2 * j + 2, 0)
            return carry

        lax.fori_loop(0, (nk - 2) // 2, pair, 0, unroll=8)
        step(nk - 1, 1, last=True)
        stage_c(nk - 1, 1)
        for i, h in enumerate(heads):
            acc = acc_scr[i]
            ot_scr[h * HEAD_DIM:(h + 1) * HEAD_DIM, :] = acc[0:HEAD_DIM] / acc[HEAD_DIM:HEAD_DIM + 1]
    o_ref[0] = ot_scr[...].T.astype(o_ref.dtype)


def _glob_attn(qb, kb, vb, tq, tk):
    b, s, _ = qb.shape
    assert s % (2 * tk) == 0, "the chunk pipeline walks key chunks in pairs"
    return pl.pallas_call(
        functools.partial(_glob_attn_kernel, tk=tk),
        grid=(b, s // tq),
        in_specs=[pl.BlockSpec((1, tq, Q_W), lambda bi, qi: (bi, qi, 0)),
                  pl.BlockSpec((1, s, KV_W), lambda bi, qi: (bi, 0, 0)),
                  pl.BlockSpec((1, s, KV_W), lambda bi, qi: (bi, 0, 0))],
        out_specs=pl.BlockSpec((1, tq, Q_W), lambda bi, qi: (bi, qi, 0)),
        out_shape=jax.ShapeDtypeStruct((b, s, Q_W), BF16),
        scratch_shapes=[pltpu.VMEM((KV_HEADS, s, HEAD_DIM), BF16), pltpu.VMEM((KV_HEADS * VT_ROWS, s), BF16),
                        pltpu.VMEM((Q_W, tq), BF16),
                        pltpu.VMEM((2, GROUP, tk, tq), SCORE_DTYPE), pltpu.VMEM((2, GROUP, tk, tq), BF16),
                        pltpu.VMEM((2, GROUP, 1, tq), F32), pltpu.VMEM((GROUP, 1, tq), F32),
                        pltpu.VMEM((2, GROUP, 1, tq), F32), pltpu.VMEM((GROUP, VT_ROWS, tq), F32),
                        pltpu.VMEM((Q_W, tq), F32)],
        compiler_params=_params(("arbitrary", "arbitrary")),
        name="glob_attn",
    )(qb, kb, vb)


def _first_argmax(vals, idx, big):
    m = jnp.max(vals, axis=0, keepdims=True)
    return m, jnp.min(jnp.where(vals == m, idx, big), axis=0, keepdims=True)


def _route(lt):
    n = lt.shape[1]
    gl = lt[0:N_GROUPS]
    idx = lax.broadcasted_iota(jnp.int32, (N_GROUPS, n), 0)
    gmax, g_sel = _first_argmax(gl, idx, N_GROUPS)
    g_w = 1.0 / jnp.sum(jnp.exp(gl - gmax), axis=0, keepdims=True)
    e_sel = jnp.zeros((EXPERTS_PER_GROUP, n), F32)
    for g in range(N_GROUPS):
        lo = N_GROUPS + g * EXPERTS_PER_GROUP
        e_sel = jnp.where(g_sel == g, lt[lo:lo + EXPERTS_PER_GROUP], e_sel)
    ex = jnp.exp(e_sel - jnp.max(e_sel, axis=0, keepdims=True))
    e_prob = ex / jnp.sum(ex, axis=0, keepdims=True)
    p1, i1 = _first_argmax(e_prob, idx, EXPERTS_PER_GROUP)
    rest = jnp.where(idx == i1, -1.0, e_prob)
    p2, i2 = _first_argmax(rest, idx, EXPERTS_PER_GROUP)
    tot = p1 + p2
    row = lax.broadcasted_iota(jnp.int32, (2 * N_GROUPS, n), 0)
    out = jnp.where(row == g_sel, 1.0, 0.0)
    out = out + jnp.where(row == N_GROUPS + i1, g_w * (p1 / tot), 0.0)
    return out + jnp.where(row == N_GROUPS + i2, g_w * (p2 / tot), 0.0)


def _merge_kernel(x_ref, ya_ref, yb_ref, ga_ref, gb_ref, woa_ref, wob_ref, wout_ref, gf_ref, wr_ref, br_ref,
                  x1_ref, route_ref, routet_ref, cnt_ref):
    tm = x_ref.shape[0]
    half = tm // MERGE_SUBTILES
    cnt = jnp.zeros((2 * N_GROUPS, 1), F32)
    for r in range(0, tm, half):
        rows = slice(r, r + half)
        a = jnp.dot(ya_ref[rows, :], woa_ref[...], preferred_element_type=F32)
        b = jnp.dot(yb_ref[rows, :], wob_ref[...], preferred_element_type=F32)
        merged = ga_ref[rows, :].astype(F32) * a + gb_ref[rows, :].astype(F32) * b
        x1 = x_ref[rows, :] + jnp.dot(merged.astype(BF16), wout_ref[...], preferred_element_type=F32)
        x1_ref[rows, :] = x1
        t = _rmsnorm(x1, gf_ref[...])
        t_hi, t_lo = _hi_lo(t)
        big = jnp.dot(t_hi, wr_ref[...], preferred_element_type=F32)
        small = jnp.dot(t_lo, wr_ref[:, 0:LANES], preferred_element_type=F32)
        logits = big[:, 0:LANES] + big[:, LANES:2 * LANES] + small + br_ref[...]
        block = _route(logits.T)
        route_t = jnp.concatenate([block, jnp.zeros((LANES - block.shape[0], half), F32)], axis=0)
        routet_ref[:, rows] = route_t
        route_ref[rows, :] = route_t.T
        cnt = cnt + jnp.sum(block, axis=1, keepdims=True)
    cnt_ref[0] = jnp.broadcast_to(cnt, cnt_ref.shape[1:]).astype(jnp.int32)


def _router_weights(w_rg, b_rg, w_re, b_re):
    d = w_rg.shape[0]
    wr = jnp.zeros((d, LANES), F32).at[:, :N_GROUPS].set(w_rg).at[:, N_GROUPS:N_GROUPS + N_EXPERTS].set(w_re)
    br = jnp.zeros((1, LANES), F32).at[0, :N_GROUPS].set(b_rg).at[0, N_GROUPS:N_GROUPS + N_EXPERTS].set(b_re)
    return jnp.concatenate(_hi_lo(wr), axis=1), br


def _merge(x2, ya, yb, ga, gb, w, tm):
    t, d = x2.shape
    row = lambda i: (i, 0)
    return pl.pallas_call(
        _merge_kernel,
        grid=(t // tm,),
        in_specs=[pl.BlockSpec((tm, d), row), pl.BlockSpec((tm, Q_W), row), pl.BlockSpec((tm, Q_W), row),
                  pl.BlockSpec((tm, d), row), pl.BlockSpec((tm, d), row),
                  _full((Q_W, d)), _full((Q_W, d)), _full((d, d)), _full((1, d)), _full((d, 2 * LANES)),
                  _full((1, LANES))],
        out_specs=[pl.BlockSpec((tm, d), row), pl.BlockSpec((tm, LANES), row),
                   pl.BlockSpec((LANES, tm), lambda i: (0, i)), pl.BlockSpec((1, 8, LANES), lambda i: (i, 0, 0))],
        out_shape=[jax.ShapeDtypeStruct((t, d), F32), jax.ShapeDtypeStruct((t, LANES), F32),
                   jax.ShapeDtypeStruct((LANES, t), F32), jax.ShapeDtypeStruct((t // tm, 8, LANES), jnp.int32)],
        compiler_params=_params(("parallel",)),
        name="merge",
    )(x2, ya, yb, ga, gb, w["w_oa"], w["w_ob"], w["w_out"], w["ffn_g"], w["w_router"], w["b_router"])


ROW_ALIGN = 128
BIG_BLOCK = 2 * ROW_ALIGN
GATHER_BLOCK = 4 * ROW_ALIGN
MAX_WHOLE = 4


def _rmsnorm(x, g):
    ms = jnp.mean(x * x, axis=-1, keepdims=True)
    return x * lax.rsqrt(ms + EPS) * g


def _moe_kernel(cnt_ref, x1_ref, route_ref, routet_ref, gf_ref, wg_ref, wu_ref, wd_ref, gfin_ref, o_ref,
                p_scr, q_scr, ts_scr, ws_scr):
    i = pl.program_id(0)
    g = pl.program_id(1)
    tm = x1_ref.shape[0]
    rows = p_scr.shape[0]
    offs, off = [], 0
    for gg in range(N_GROUPS):
        offs.append(off)
        off = off + (cnt_ref[i, gg] + (ROW_ALIGN - 1)) // ROW_ALIGN * ROW_ALIGN

    @pl.when(g == 0)
    def _():
        route = route_ref[...]
        lane = lax.broadcasted_iota(jnp.int32, (tm, LANES), 1)
        onehot = jnp.where(lane < N_GROUPS, route, 0.0)
        r_i = lax.broadcasted_iota(jnp.int32, (tm, tm), 0)
        c_i = lax.broadcasted_iota(jnp.int32, (tm, tm), 1)
        before = jnp.dot(jnp.where(c_i < r_i, 1.0, 0.0).astype(BF16), onehot.astype(BF16),
                         preferred_element_type=F32)
        goff = jnp.zeros((tm, LANES), F32)
        for gg in range(N_GROUPS):
            goff = jnp.where(lane == gg, jnp.asarray(offs[gg], jnp.int32).astype(F32), goff)
        dest_col = jnp.sum((before + goff) * onehot, axis=1, keepdims=True).astype(jnp.int32)
        q_scr[...] = jnp.where(dest_col == lax.broadcasted_iota(jnp.int32, (tm, rows), 1), 1.0, 0.0).astype(BF16)
        sub = lax.broadcasted_iota(jnp.int32, (16, tm), 0)
        onehot_t = jnp.where(sub < N_GROUPS, routet_ref[0:16, :], 0.0)
        before_t = jnp.dot(onehot_t.astype(BF16), jnp.where(r_i < c_i, 1.0, 0.0).astype(BF16),
                           preferred_element_type=F32)
        goff_t = jnp.zeros((16, tm), F32)
        for gg in range(N_GROUPS):
            goff_t = jnp.where(sub == gg, jnp.asarray(offs[gg], jnp.int32).astype(F32), goff_t)
        dest_row = jnp.sum((before_t + goff_t) * onehot_t, axis=0, keepdims=True).astype(jnp.int32)
        for r0 in range(0, rows, BIG_BLOCK):
            rid = lax.broadcasted_iota(jnp.int32, (BIG_BLOCK, tm), 0) + r0
            p_scr[r0:r0 + BIG_BLOCK, :] = jnp.where(dest_row == rid, 1.0, 0.0).astype(BF16)
        t = _rmsnorm(x1_ref[...], gf_ref[...]).astype(BF16)
        r_hi_lo = jnp.concatenate(_hi_lo(route), axis=1)
        for r0 in range(0, rows, GATHER_BLOCK):
            pb = p_scr[r0:r0 + GATHER_BLOCK, :]
            ts_scr[r0:r0 + GATHER_BLOCK, :] = jnp.dot(pb, t, preferred_element_type=F32).astype(BF16)
            w2 = jnp.dot(pb, r_hi_lo, preferred_element_type=F32)
            ws_scr[r0:r0 + GATHER_BLOCK, :] = w2[:, 0:LANES] + w2[:, LANES:2 * LANES]

    def ffn(r, m):
        tb = ts_scr[pl.ds(pl.multiple_of(r, ROW_ALIGN), m), :]
        wb = ws_scr[pl.ds(pl.multiple_of(r, ROW_ALIGN), m), :]
        y = jnp.zeros((m, o_ref.shape[1]), F32)
        for e in range(EXPERTS_PER_GROUP):
            a = jax.nn.silu(jnp.dot(tb, wg_ref[0, e], preferred_element_type=F32)) * jnp.dot(
                tb, wu_ref[0, e], preferred_element_type=F32)
            y = y + wb[:, N_GROUPS + e:N_GROUPS + e + 1] * jnp.dot(a.astype(BF16), wd_ref[0, e],
                                                                   preferred_element_type=F32)
        ts_scr[pl.ds(pl.multiple_of(r, ROW_ALIGN), m), :] = y.astype(BF16)

    start = offs[0]
    for gg in range(1, N_GROUPS):
        start = jnp.where(g == gg, offs[gg], start)
    n_small = (cnt_ref[i, g] + (ROW_ALIGN - 1)) // ROW_ALIGN

    for units in range(1, MAX_WHOLE + 1):
        @pl.when(n_small == units)
        def _(units=units):
            ffn(start, units * ROW_ALIGN)

    @pl.when(n_small > MAX_WHOLE)
    def _():
        n_big = n_small // 2

        def big(j, carry):
            ffn(start + j * BIG_BLOCK, BIG_BLOCK)
            return carry

        lax.fori_loop(0, n_big, big, 0)

        @pl.when(n_small % 2 == 1)
        def _():
            ffn(start + n_big * BIG_BLOCK, ROW_ALIGN)

    @pl.when(g == N_GROUPS - 1)
    def _():
        x2 = x1_ref[...] + jnp.dot(q_scr[...], ts_scr[...], preferred_element_type=F32)
        o_ref[...] = _rmsnorm(x2, gfin_ref[...])


def _moe(x1, route, routet, cnt, w, tm):
    t, d = x1.shape
    de = w["w_eg"].shape[-1]
    rows = tm + N_GROUPS * ROW_ALIGN
    row = lambda i, g, c: (i, 0)
    const = lambda i, g, c: (0, 0)
    wspec = lambda a, b: pl.BlockSpec((1, EXPERTS_PER_GROUP, a, b), lambda i, g, c: (g, 0, 0, 0))
    return pl.pallas_call(
        _moe_kernel,
        grid_spec=pltpu.PrefetchScalarGridSpec(
            num_scalar_prefetch=1,
            grid=(t // tm, N_GROUPS),
            in_specs=[pl.BlockSpec((tm, d), row), pl.BlockSpec((tm, LANES), row),
                      pl.BlockSpec((LANES, tm), lambda i, g, c: (0, i)), pl.BlockSpec((1, d), const),
                      wspec(d, de), wspec(d, de), wspec(de, d), pl.BlockSpec((1, d), const)],
            out_specs=pl.BlockSpec((tm, d), row),
            scratch_shapes=[pltpu.VMEM((rows, tm), BF16), pltpu.VMEM((tm, rows), BF16), pltpu.VMEM((rows, d), BF16),
                            pltpu.VMEM((rows, LANES), F32)]),
        out_shape=jax.ShapeDtypeStruct((t, d), F32),
        compiler_params=_params(("arbitrary", "arbitrary")),
        name="moe",
    )(cnt, x1, route, routet, w["ffn_g"], w["w_eg"], w["w_eu"], w["w_ed"], w["final_g"])


ROW_TILE = 1024
WIN_TILE = 512
GLOB_TQ = 256
GLOB_TK = 256


def _prepare(attn_g, w_in, sink, gq, gk, w_oa, w_ob, w_out, ffn_g, w_rg, b_rg, w_re, b_re, w_eg, w_eu, w_ed,
             final_g):
    d = w_in.shape[0]
    w_router, b_router = _router_weights(w_rg, b_rg, w_re, b_re)
    grouped = lambda a: a.astype(BF16).reshape((N_GROUPS, EXPERTS_PER_GROUP) + a.shape[1:])
    return dict(
        attn_g=attn_g.reshape(1, d), w_in=w_in.astype(BF16), sink=sink.astype(F32),
        gq=jnp.tile(gq.astype(F32), Q_HEADS).reshape(1, Q_W), gk=jnp.tile(gk.astype(F32), KV_HEADS).reshape(1, KV_W),
        w_oa=w_oa.astype(BF16), w_ob=w_ob.astype(BF16), w_out=w_out.astype(BF16), ffn_g=ffn_g.reshape(1, d),
        w_router=w_router, b_router=b_router, w_eg=grouped(w_eg), w_eu=grouped(w_eu), w_ed=grouped(w_ed),
        final_g=final_g.reshape(1, d))


def _trunk(x, w):
    b, s, d = x.shape
    t = b * s
    assert s % ROW_TILE == 0 and s % (2 * GLOB_TK) == 0, "sequence length must be a multiple of the row tile"
    x2 = x.reshape(t, d)
    qa, ka, va, qb, kb, vb, ga, gb = _in_proj(x2, s, w, ROW_TILE)
    ya = _win_attn(qa, ka, va, w["sink"], s, WIN_TILE)
    yb = _glob_attn(qb.reshape(b, s, Q_W), kb.reshape(b, s, KV_W), vb.reshape(b, s, KV_W),
                    GLOB_TQ, GLOB_TK).reshape(t, Q_W)
    x1, route, routet, cnt = _merge(x2, ya, yb, ga, gb, w, ROW_TILE)
    y = _moe(x1, route, routet, cnt[:, :N_GROUPS, 0], w, ROW_TILE)
    return y.reshape(b, s, d)


def kernel(x_prompt, x_sample, attn_norm_g, w_in, a_sink, b_q_norm_g, b_k_norm_g, w_oa, w_ob, w_out, ffn_norm_g,
           w_router_group, b_router_group, w_router_expert, b_router_expert, w_expert_gate, w_expert_up,
           w_expert_down, final_norm_g):
    assert attn_norm_g.shape[0] == 1, "single-layer trunk"
    w = _prepare(attn_norm_g[0], w_in[0], a_sink[0], b_q_norm_g[0], b_k_norm_g[0], w_oa[0], w_ob[0], w_out[0],
                 ffn_norm_g[0], w_router_group[0], b_router_group[0], w_router_expert[0], b_router_expert[0],
                 w_expert_gate[0], w_expert_up[0], w_expert_down[0], final_norm_g)
    return (_trunk(x_prompt, w), _trunk(x_sample, w))
```

```python
import functools

import jax
import jax.numpy as jnp
import numpy as np
from jax import lax
from jax.experimental import pallas as pl
from jax.experimental.pallas import tpu as pltpu

HEAD_DIM = 64
Q_HEADS = 8
KV_HEADS = 2
GROUP = Q_HEADS // KV_HEADS
Q_W = Q_HEADS * HEAD_DIM
KV_W = KV_HEADS * HEAD_DIM
WINDOW = 128
BLOCK = 128
GRID_W = 64
ROPE_THETA = 10000.0
N_GROUPS = 4
EXPERTS_PER_GROUP = 4
N_EXPERTS = N_GROUPS * EXPERTS_PER_GROUP
EPS = 1e-6
NEG_INF = -1e30
SCALE = HEAD_DIM ** -0.5
LOG2E = 1.4426950408889634
ONES_ROWS = 16
VT_ROWS = HEAD_DIM + ONES_ROWS
SCORE_DTYPE = jnp.bfloat16
LANES = 128

VMEM_LIMIT = 56 * 1024 * 1024

F32 = jnp.float32
BF16 = jnp.bfloat16


def _params(sem):
    return pltpu.CompilerParams(dimension_semantics=sem, vmem_limit_bytes=VMEM_LIMIT)


def _full(shape):
    return pl.BlockSpec(shape, lambda *_: (0,) * len(shape))


def _hi_lo(v):
    top = lax.bitcast_convert_type(lax.bitcast_convert_type(v, jnp.uint32) & jnp.uint32(0xFFFF0000), F32)
    return top.astype(BF16), (v - top).astype(BF16)


def _split_dot(y, ones_bd):
    hi, lo = _hi_lo(y)
    return jnp.dot(hi, ones_bd, preferred_element_type=F32) + jnp.dot(lo, ones_bd, preferred_element_type=F32)


def _pair_swap(y):
    n = y.shape[-1]
    lane = lax.broadcasted_iota(jnp.int32, y.shape, y.ndim - 1)
    nxt = pltpu.roll(y, n - 1, y.ndim - 1)
    prv = pltpu.roll(y, 1, y.ndim - 1)
    return jnp.where((lane & 1) == 0, nxt, prv)


def _norm_rope(y, gain, ones_bd, cos, sin):
    ms = _split_dot(y * y, ones_bd) * (1.0 / HEAD_DIM)
    yn = y * lax.rsqrt(ms + EPS) * gain
    return yn * cos + _pair_swap(yn) * sin


def _in_proj_kernel(x_ref, g_ref, w_ref, cos_ref, sin_ref, gq_ref, gk_ref, oq_ref, ok_ref,
                    qa_ref, ka_ref, va_ref, qb_ref, kb_ref, vb_ref, ga_ref, gb_ref):
    x = x_ref[...]
    d = x.shape[1]
    ms = jnp.mean(x * x, axis=-1, keepdims=True)
    h = (x * lax.rsqrt(ms + EPS) * g_ref[...]).astype(BF16)

    def proj(lo, width):
        return jnp.dot(h, w_ref[:, lo:lo + width], preferred_element_type=F32)

    o = 0
    qa_ref[...] = (proj(o, Q_W) * (SCALE * LOG2E)).astype(BF16); o += Q_W
    ka_ref[...] = proj(o, KV_W).astype(BF16); o += KV_W
    va_ref[...] = proj(o, KV_W).astype(BF16); o += KV_W
    cos = cos_ref[...]
    sin = sin_ref[...]
    qb = proj(o, Q_W); o += Q_W
    cos_q = jnp.concatenate([cos] * (Q_W // LANES), axis=1)
    sin_q = jnp.concatenate([sin] * (Q_W // LANES), axis=1)
    qb_ref[...] = (_norm_rope(qb, gq_ref[...], oq_ref[...], cos_q, sin_q) * (SCALE * LOG2E)).astype(BF16)
    kb = proj(o, KV_W); o += KV_W
    kb_ref[...] = _norm_rope(kb, gk_ref[...], ok_ref[...], cos, sin).astype(BF16)
    vb_ref[...] = proj(o, KV_W).astype(BF16); o += KV_W
    ga_ref[...] = jax.nn.sigmoid(proj(o, d)).astype(BF16); o += d
    gb_ref[...] = jax.nn.sigmoid(proj(o, d)).astype(BF16)


def _rope_tables(seq_len):
    rows = seq_len // GRID_W
    row = jnp.repeat(jnp.arange(rows, dtype=F32), GRID_W)
    col = jnp.tile(jnp.arange(GRID_W, dtype=F32), rows)
    half = HEAD_DIM // 2
    inv = ROPE_THETA ** (-jnp.arange(0, half, 2, dtype=F32) / half)
    ang = jnp.concatenate([row[:, None] * inv, col[:, None] * inv], axis=-1)
    cos = jnp.repeat(jnp.cos(ang), 2, axis=-1)
    sin = jnp.repeat(jnp.sin(ang), 2, axis=-1) * jnp.tile(jnp.array([-1.0, 1.0], F32), half)
    reps = LANES // HEAD_DIM
    return jnp.tile(cos, (1, reps)), jnp.tile(sin, (1, reps))


def _block_ones(width):
    idx = np.arange(width) // HEAD_DIM
    return jnp.asarray(idx[:, None] == idx[None, :], dtype=BF16)


def _in_proj(x2, seq_len, w, tm):
    t, d = x2.shape
    in_w = w["w_in"].shape[1]
    cos, sin = _rope_tables(seq_len)
    nseq = seq_len // tm
    row = lambda i: (i, 0)
    pos = lambda i: (i % nseq, 0)
    outs = [(Q_W, BF16), (KV_W, BF16), (KV_W, BF16), (Q_W, BF16), (KV_W, BF16), (KV_W, BF16), (d, BF16), (d, BF16)]
    return pl.pallas_call(
        _in_proj_kernel,
        grid=(t // tm,),
        in_specs=[pl.BlockSpec((tm, d), row), _full((1, d)), _full((d, in_w)),
                  pl.BlockSpec((tm, LANES), pos), pl.BlockSpec((tm, LANES), pos),
                  _full((1, Q_W)), _full((1, KV_W)), _full((Q_W, Q_W)), _full((KV_W, KV_W))],
        out_specs=[pl.BlockSpec((tm, w), row) for w, _ in outs],
        out_shape=[jax.ShapeDtypeStruct((t, w), dt) for w, dt in outs],
        compiler_params=_params(("parallel",)),
        name="in_proj",
    )(x2, w["attn_g"], w["w_in"], cos, sin, w["gq"], w["gk"], _block_ones(Q_W), _block_ones(KV_W))


PAIRS = Q_HEADS // 2
N_EDGE = 4


def _win_bias():
    slopes = np.exp2(-8.0 * np.arange(1, Q_HEADS + 1, dtype=np.float32) / Q_HEADS).astype(np.float32)
    key = np.arange(3 * BLOCK)[:, None]
    dist = np.abs(np.arange(BLOCK)[None, :] - (key - BLOCK))
    tabs = []
    for edge in range(N_EDGE):
        alive = dist <= WINDOW
        if edge & 1:
            alive = alive & (key >= BLOCK)
        if edge & 2:
            alive = alive & (key < 2 * BLOCK)
        heads = [np.where(alive, -slopes[h] * np.float32(LOG2E) * dist.astype(np.float32), np.float32(NEG_INF))
                 for h in range(Q_HEADS)]
        tabs.append(np.stack([np.concatenate([heads[2 * p], heads[2 * p + 1]], axis=1) for p in range(PAIRS)]))
    return jnp.asarray(np.stack(tabs), dtype=F32)


def _win_attn_kernel(sink_ref, q_ref, kc_ref, kp_ref, kn_ref, vc_ref, vp_ref, vn_ref, bias_ref, o_ref,
                     kh_scr, vt_scr, qp_scr, s_scr, p_scr, ot_scr, *, blocks_per_seq, sub):
    i = pl.program_id(0)
    tq = q_ref.shape[0]
    ones = jnp.ones((ONES_ROWS, BLOCK), BF16)
    for off, n, kr, vr in ((0, BLOCK, kp_ref, vp_ref), (BLOCK, tq, kc_ref, vc_ref), (BLOCK + tq, BLOCK, kn_ref, vn_ref)):
        k = kr[...]
        vt = vr[...].astype(F32).T.astype(BF16)
        for g in range(KV_HEADS):
            kh_scr[g, off:off + n, :] = k[:, g * HEAD_DIM:(g + 1) * HEAD_DIM]
            vt_scr[g * VT_ROWS:g * VT_ROWS + HEAD_DIM, off:off + n] = vt[g * HEAD_DIM:(g + 1) * HEAD_DIM, :]
            for o in range(off, off + n, BLOCK):
                vt_scr[g * VT_ROWS + HEAD_DIM:(g + 1) * VT_ROWS, o:o + BLOCK] = ones
    qt = q_ref[...].astype(F32).T.astype(BF16)
    for j in range(sub):
        for p in range(PAIRS):
            for u in range(2):
                h = 2 * p + u
                qp_scr[j, p, :, u * BLOCK:(u + 1) * BLOCK] = qt[h * HEAD_DIM:(h + 1) * HEAD_DIM,
                                                                j * BLOCK:(j + 1) * BLOCK]
    lane = lax.broadcasted_iota(jnp.int32, (1, 2 * BLOCK), 1)
    for j in range(sub):
        x = j % 2
        blk = (i * sub + j) % blocks_per_seq
        edge = (blk == 0).astype(jnp.int32) + 2 * (blk == blocks_per_seq - 1).astype(jnp.int32)
        for p in range(PAIRS):
            g = (2 * p) // GROUP
            s = jnp.dot(kh_scr[g, j * BLOCK:(j + 3) * BLOCK, :], qp_scr[j, p], preferred_element_type=F32)
            s_scr[x, p] = s + bias_ref[edge, p]
        sink_terms = []
        for p in range(PAIRS):
            logits = s_scr[x, p]
            sink = jnp.where(lane < BLOCK, sink_ref[2 * p], sink_ref[2 * p + 1]) * LOG2E
            m = jnp.maximum(jnp.max(logits, axis=0, keepdims=True), sink)
            p_scr[x, p] = jnp.exp2(logits - m).astype(BF16)
            sink_terms.append(jnp.exp2(sink - m))
        for p in range(PAIRS):
            g = (2 * p) // GROUP
            ot = jnp.dot(vt_scr[g * VT_ROWS:(g + 1) * VT_ROWS, j * BLOCK:(j + 3) * BLOCK], p_scr[x, p],
                         preferred_element_type=F32)
            out = ot[0:HEAD_DIM] / (ot[HEAD_DIM:HEAD_DIM + 1] + sink_terms[p])
            for u in range(2):
                h = 2 * p + u
                ot_scr[h * HEAD_DIM:(h + 1) * HEAD_DIM, j * BLOCK:(j + 1) * BLOCK] = out[:, u * BLOCK:(u + 1) * BLOCK]
    o_ref[...] = ot_scr[...].T.astype(o_ref.dtype)


def _win_attn(qa, ka, va, sink, seq_len, tq):
    t = qa.shape[0]
    sub = tq // BLOCK
    nblk = t // BLOCK
    win = tq + 2 * BLOCK
    row = lambda i: (i, 0)
    prev = lambda i: (jnp.maximum(i * sub - 1, 0), 0)
    nxt = lambda i: (jnp.minimum((i + 1) * sub, nblk - 1), 0)
    kv_specs = [pl.BlockSpec((tq, KV_W), row), pl.BlockSpec((BLOCK, KV_W), prev), pl.BlockSpec((BLOCK, KV_W), nxt)]
    bias_spec = pl.BlockSpec((N_EDGE, PAIRS, 3 * BLOCK, 2 * BLOCK), lambda i: (0, 0, 0, 0),
                             pipeline_mode=pl.Buffered(1))
    return pl.pallas_call(
        functools.partial(_win_attn_kernel, blocks_per_seq=seq_len // BLOCK, sub=sub),
        grid=(t // tq,),
        in_specs=[pl.BlockSpec(memory_space=pltpu.SMEM), pl.BlockSpec((tq, Q_W), row)] + kv_specs + kv_specs
                 + [bias_spec],
        out_specs=pl.BlockSpec((tq, Q_W), row),
        out_shape=jax.ShapeDtypeStruct((t, Q_W), BF16),
        scratch_shapes=[pltpu.VMEM((KV_HEADS, win, HEAD_DIM), BF16), pltpu.VMEM((KV_HEADS * VT_ROWS, win), BF16),
                        pltpu.VMEM((sub, PAIRS, HEAD_DIM, 2 * BLOCK), BF16),
                        pltpu.VMEM((2, PAIRS, 3 * BLOCK, 2 * BLOCK), F32),
                        pltpu.VMEM((2, PAIRS, 3 * BLOCK, 2 * BLOCK), BF16), pltpu.VMEM((Q_W, tq), F32)],
        compiler_params=_params(("parallel",)),
        name="win_attn",
    )(sink, qa, ka, ka, ka, va, va, va, _win_bias())


def _glob_attn_kernel(q_ref, k_ref, v_ref, o_ref, kh_ref, vt_ref, qt_scr, s_scr, p_scr, cm_scr, m_scr, a_scr, acc_scr,
                      ot_scr, *, tk):
    tq = q_ref.shape[1]
    nk = k_ref.shape[1] // tk

    @pl.when(pl.program_id(1) == 0)
    def _():
        ones = jnp.ones((ONES_ROWS, tk), BF16)

        def fill(c, carry):
            off = pl.multiple_of(c * tk, tk)
            vt = v_ref[0, pl.ds(off, tk), :].astype(F32).T.astype(BF16)
            kc = k_ref[0, pl.ds(off, tk), :]
            for g in range(KV_HEADS):
                kh_ref[g, pl.ds(off, tk), :] = kc[:, g * HEAD_DIM:(g + 1) * HEAD_DIM]
                vt_ref[g * VT_ROWS:g * VT_ROWS + HEAD_DIM, pl.ds(off, tk)] = vt[g * HEAD_DIM:(g + 1) * HEAD_DIM, :]
                vt_ref[g * VT_ROWS + HEAD_DIM:(g + 1) * VT_ROWS, pl.ds(off, tk)] = ones
            return carry

        lax.fori_loop(0, nk, fill, 0)

    qt_scr[...] = q_ref[0].astype(F32).T.astype(BF16)

    for g in range(KV_HEADS):
        heads = range(g * GROUP, (g + 1) * GROUP)

        def stage_a(c, x, heads=heads, g=g):
            kc = kh_ref[g, pl.ds(pl.multiple_of(c * tk, tk), tk), :]
            for i, h in enumerate(heads):
                s = jnp.dot(kc, qt_scr[h * HEAD_DIM:(h + 1) * HEAD_DIM, :], preferred_element_type=F32)
                s_scr[x, i] = s.astype(s_scr.dtype)
                cm_scr[x, i] = jnp.max(s, axis=0, keepdims=True)

        def stage_b(x):
            for i in range(GROUP):
                m = m_scr[i]
                m_new = jnp.maximum(m, cm_scr[x, i])
                a_scr[x, i] = jnp.exp2(m - m_new)
                m_scr[i] = m_new
                p_scr[x, i] = jnp.exp2(s_scr[x, i] - m_new.astype(s_scr.dtype)).astype(BF16)

        def stage_c(c, x, g=g):
            vt = vt_ref[g * VT_ROWS:(g + 1) * VT_ROWS, pl.ds(pl.multiple_of(c * tk, tk), tk)]
            for i in range(GROUP):
                acc_scr[i] = a_scr[x, i] * acc_scr[i] + jnp.dot(vt, p_scr[x, i], preferred_element_type=F32)

        def step(c, x, last=False):
            stage_c(c - 1, 1 - x)
            stage_b(x)
            if not last:
                stage_a(c + 1, 1 - x)

        for i in range(GROUP):
            m_scr[i] = jnp.full((1, tq), -jnp.inf, F32)
            acc_scr[i] = jnp.zeros((VT_ROWS, tq), F32)
        stage_a(0, 0)
        stage_b(0)
        stage_a(1, 1)

        def pair(j, carry):
            step(2 * j + 1, 1)
            step(2 * j + 2, 0)
            return carry

        lax.fori_loop(0, (nk - 2) // 2, pair, 0, unroll=8)
        step(nk - 1, 1, last=True)
        stage_c(nk - 1, 1)
        for i, h in enumerate(heads):
            acc = acc_scr[i]
            ot_scr[h * HEAD_DIM:(h + 1) * HEAD_DIM, :] = acc[0:HEAD_DIM] / acc[HEAD_DIM:HEAD_DIM + 1]
    o_ref[0] = ot_scr[...].T.astype(o_ref.dtype)


def _glob_attn(qb, kb, vb, tq, tk):
    b, s, _ = qb.shape
    assert s % (2 * tk) == 0, "the chunk pipeline walks key chunks in pairs"
    return pl.pallas_call(
        functools.partial(_glob_attn_kernel, tk=tk),
        grid=(b, s // tq),
        in_specs=[pl.BlockSpec((1, tq, Q_W), lambda bi, qi: (bi, qi, 0)),
                  pl.BlockSpec((1, s, KV_W), lambda bi, qi: (bi, 0, 0)),
                  pl.BlockSpec((1, s, KV_W), lambda bi, qi: (bi, 0, 0))],
        out_specs=pl.BlockSpec((1, tq, Q_W), lambda bi, qi: (bi, qi, 0)),
        out_shape=jax.ShapeDtypeStruct((b, s, Q_W), BF16),
        scratch_shapes=[pltpu.VMEM((KV_HEADS, s, HEAD_DIM), BF16), pltpu.VMEM((KV_HEADS * VT_ROWS, s), BF16),
                        pltpu.VMEM((Q_W, tq), BF16),
                        pltpu.VMEM((2, GROUP, tk, tq), SCORE_DTYPE), pltpu.VMEM((2, GROUP, tk, tq), BF16),
                        pltpu.VMEM((2, GROUP, 1, tq), F32), pltpu.VMEM((GROUP, 1, tq), F32),
                        pltpu.VMEM((2, GROUP, 1, tq), F32), pltpu.VMEM((GROUP, VT_ROWS, tq), F32),
                        pltpu.VMEM((Q_W, tq), F32)],
        compiler_params=_params(("arbitrary", "arbitrary")),
        name="glob_attn",
    )(qb, kb, vb)


def _first_argmax(vals, idx, big):
    m = jnp.max(vals, axis=0, keepdims=True)
    return m, jnp.min(jnp.where(vals == m, idx, big), axis=0, keepdims=True)


def _route(lt):
    n = lt.shape[1]
    gl = lt[0:N_GROUPS]
    idx = lax.broadcasted_iota(jnp.int32, (N_GROUPS, n), 0)
    gmax, g_sel = _first_argmax(gl, idx, N_GROUPS)
    g_w = 1.0 / jnp.sum(jnp.exp(gl - gmax), axis=0, keepdims=True)
    e_sel = jnp.zeros((EXPERTS_PER_GROUP, n), F32)
    for g in range(N_GROUPS):
        lo = N_GROUPS + g * EXPERTS_PER_GROUP
        e_sel = jnp.where(g_sel == g, lt[lo:lo + EXPERTS_PER_GROUP], e_sel)
    ex = jnp.exp(e_sel - jnp.max(e_sel, axis=0, keepdims=True))
    e_prob = ex / jnp.sum(ex, axis=0, keepdims=True)
    p1, i1 = _first_argmax(e_prob, idx, EXPERTS_PER_GROUP)
    rest = jnp.where(idx == i1, -1.0, e_prob)
    p2, i2 = _first_argmax(rest, idx, EXPERTS_PER_GROUP)
    tot = p1 + p2
    row = lax.broadcasted_iota(jnp.int32, (2 * N_GROUPS, n), 0)
    out = jnp.where(row == g_sel, 1.0, 0.0)
    out = out + jnp.where(row == N_GROUPS + i1, g_w * (p1 / tot), 0.0)
    return out + jnp.where(row == N_GROUPS + i2, g_w * (p2 / tot), 0.0)


def _merge_kernel(x_ref, ya_ref, yb_ref, ga_ref, gb_ref, woa_ref, wob_ref, wout_ref, gf_ref, wr_ref, br_ref,
                  x1_ref, route_ref, routet_ref, cnt_ref):
    tm = x_ref.shape[0]
    a = jnp.dot(ya_ref[...], woa_ref[...], preferred_element_type=F32)
    b = jnp.dot(yb_ref[...], wob_ref[...], preferred_element_type=F32)
    merged = ga_ref[...].astype(F32) * a + gb_ref[...].astype(F32) * b
    x1 = x_ref[...] + jnp.dot(merged.astype(BF16), wout_ref[...], preferred_element_type=F32)
    x1_ref[...] = x1
    t = _rmsnorm(x1, gf_ref[...])
    t_hi, t_lo = _hi_lo(t)
    big = jnp.dot(t_hi, wr_ref[...], preferred_element_type=F32)
    small = jnp.dot(t_lo, wr_ref[:, 0:LANES], preferred_element_type=F32)
    logits = big[:, 0:LANES] + big[:, LANES:2 * LANES] + small + br_ref[...]
    block = _route(logits.T)
    route_t = jnp.concatenate([block, jnp.zeros((LANES - block.shape[0], tm), F32)], axis=0)
    routet_ref[...] = route_t
    route_ref[...] = route_t.T
    cnt = jnp.sum(block, axis=1, keepdims=True)
    cnt_ref[0] = jnp.broadcast_to(cnt, cnt_ref.shape[1:]).astype(jnp.int32)


def _router_weights(w_rg, b_rg, w_re, b_re):
    d = w_rg.shape[0]
    wr = jnp.zeros((d, LANES), F32).at[:, :N_GROUPS].set(w_rg).at[:, N_GROUPS:N_GROUPS + N_EXPERTS].set(w_re)
    br = jnp.zeros((1, LANES), F32).at[0, :N_GROUPS].set(b_rg).at[0, N_GROUPS:N_GROUPS + N_EXPERTS].set(b_re)
    return jnp.concatenate(_hi_lo(wr), axis=1), br


def _merge(x2, ya, yb, ga, gb, w, tm):
    t, d = x2.shape
    row = lambda i: (i, 0)
    return pl.pallas_call(
        _merge_kernel,
        grid=(t // tm,),
        in_specs=[pl.BlockSpec((tm, d), row), pl.BlockSpec((tm, Q_W), row), pl.BlockSpec((tm, Q_W), row),
                  pl.BlockSpec((tm, d), row), pl.BlockSpec((tm, d), row),
                  _full((Q_W, d)), _full((Q_W, d)), _full((d, d)), _full((1, d)), _full((d, 2 * LANES)),
                  _full((1, LANES))],
        out_specs=[pl.BlockSpec((tm, d), row), pl.BlockSpec((tm, LANES), row),
                   pl.BlockSpec((LANES, tm), lambda i: (0, i)), pl.BlockSpec((1, 8, LANES), lambda i: (i, 0, 0))],
        out_shape=[jax.ShapeDtypeStruct((t, d), F32), jax.ShapeDtypeStruct((t, LANES), F32),
                   jax.ShapeDtypeStruct((LANES, t), F32), jax.ShapeDtypeStruct((t // tm, 8, LANES), jnp.int32)],
        compiler_params=_params(("parallel",)),
        name="merge",
    )(x2, ya, yb, ga, gb, w["w_oa"], w["w_ob"], w["w_out"], w["ffn_g"], w["w_router"], w["b_router"])


ROW_ALIGN = 64
BIG_UNITS = 4
BIG_BLOCK = BIG_UNITS * ROW_ALIGN
WHOLE_UNITS = (3, 4, 5, 6)


def _rmsnorm(x, g):
    ms = jnp.mean(x * x, axis=-1, keepdims=True)
    return x * lax.rsqrt(ms + EPS) * g


def _moe_kernel(cnt_ref, x1_ref, route_ref, routet_ref, gf_ref, wg_ref, wu_ref, wd_ref, gfin_ref, o_ref,
                p_scr, q_scr, ts_scr, ws_scr):
    i = pl.program_id(0)
    g = pl.program_id(1)
    tm = x1_ref.shape[0]
    rows = p_scr.shape[0]
    offs, off = [], 0
    for gg in range(N_GROUPS):
        offs.append(off)
        off = off + (cnt_ref[i, gg] + (ROW_ALIGN - 1)) // ROW_ALIGN * ROW_ALIGN

    @pl.when(g == 0)
    def _():
        route = route_ref[...]
        lane = lax.broadcasted_iota(jnp.int32, (tm, LANES), 1)
        onehot = jnp.where(lane < N_GROUPS, route, 0.0)
        r_i = lax.broadcasted_iota(jnp.int32, (tm, tm), 0)
        c_i = lax.broadcasted_iota(jnp.int32, (tm, tm), 1)
        before = jnp.dot(jnp.where(c_i < r_i, 1.0, 0.0).astype(BF16), onehot.astype(BF16),
                         preferred_element_type=F32)
        goff = jnp.zeros((tm, LANES), F32)
        for gg in range(N_GROUPS):
            goff = jnp.where(lane == gg, jnp.asarray(offs[gg], jnp.int32).astype(F32), goff)
        dest_col = jnp.sum((before + goff) * onehot, axis=1, keepdims=True).astype(jnp.int32)
        q_scr[...] = jnp.where(dest_col == lax.broadcasted_iota(jnp.int32, (tm, rows), 1), 1.0, 0.0).astype(BF16)
        sub = lax.broadcasted_iota(jnp.int32, (16, tm), 0)
        onehot_t = jnp.where(sub < N_GROUPS, routet_ref[0:16, :], 0.0)
        before_t = jnp.dot(onehot_t.astype(BF16), jnp.where(r_i < c_i, 1.0, 0.0).astype(BF16),
                           preferred_element_type=F32)
        goff_t = jnp.zeros((16, tm), F32)
        for gg in range(N_GROUPS):
            goff_t = jnp.where(sub == gg, jnp.asarray(offs[gg], jnp.int32).astype(F32), goff_t)
        dest_row = jnp.sum((before_t + goff_t) * onehot_t, axis=0, keepdims=True).astype(jnp.int32)
        for r0 in range(0, rows, BIG_BLOCK):
            rid = lax.broadcasted_iota(jnp.int32, (BIG_BLOCK, tm), 0) + r0
            p_scr[r0:r0 + BIG_BLOCK, :] = jnp.where(dest_row == rid, 1.0, 0.0).astype(BF16)
        t = _rmsnorm(x1_ref[...], gf_ref[...]).astype(BF16)
        r_hi_lo = jnp.concatenate(_hi_lo(route), axis=1)
        gather = rows // 2
        for r0 in range(0, rows, gather):
            pb = p_scr[r0:r0 + gather, :]
            ts_scr[r0:r0 + gather, :] = jnp.dot(pb, t, preferred_element_type=F32).astype(BF16)
            w2 = jnp.dot(pb, r_hi_lo, preferred_element_type=F32)
            ws_scr[r0:r0 + gather, :] = w2[:, 0:LANES] + w2[:, LANES:2 * LANES]

    def ffn(r, m):
        tb = ts_scr[pl.ds(pl.multiple_of(r, ROW_ALIGN), m), :]
        wb = ws_scr[pl.ds(pl.multiple_of(r, ROW_ALIGN), m), :]
        y = jnp.zeros((m, o_ref.shape[1]), F32)
        for e in range(EXPERTS_PER_GROUP):
            a = jax.nn.silu(jnp.dot(tb, wg_ref[0, e], preferred_element_type=F32)) * jnp.dot(
                tb, wu_ref[0, e], preferred_element_type=F32)
            y = y + wb[:, N_GROUPS + e:N_GROUPS + e + 1] * jnp.dot(a.astype(BF16), wd_ref[0, e],
                                                                   preferred_element_type=F32)
        ts_scr[pl.ds(pl.multiple_of(r, ROW_ALIGN), m), :] = y.astype(BF16)

    start = offs[0]
    for gg in range(1, N_GROUPS):
        start = jnp.where(g == gg, offs[gg], start)
    n_small = (cnt_ref[i, g] + (ROW_ALIGN - 1)) // ROW_ALIGN

    for units in WHOLE_UNITS:
        @pl.when(n_small == units)
        def _(units=units):
            ffn(start, units * ROW_ALIGN)

    @pl.when((n_small < WHOLE_UNITS[0]) | (n_small > WHOLE_UNITS[-1]))
    def _():
        n_big = n_small // BIG_UNITS

        def big(j, carry):
            ffn(start + j * BIG_BLOCK, BIG_BLOCK)
            return carry

        lax.fori_loop(0, n_big, big, 0)
        for tail in range(1, BIG_UNITS):
            @pl.when(n_small % BIG_UNITS == tail)
            def _(tail=tail):
                ffn(start + n_big * BIG_BLOCK, tail * ROW_ALIGN)

    @pl.when(g == N_GROUPS - 1)
    def _():
        x2 = x1_ref[...] + jnp.dot(q_scr[...], ts_scr[...], preferred_element_type=F32)
        o_ref[...] = _rmsnorm(x2, gfin_ref[...])


def _moe(x1, route, routet, cnt, w, tm):
    t, d = x1.shape
    de = w["w_eg"].shape[-1]
    rows = tm + N_GROUPS * ROW_ALIGN
    row = lambda i, g, c: (i, 0)
    const = lambda i, g, c: (0, 0)
    wspec = lambda a, b: pl.BlockSpec((1, EXPERTS_PER_GROUP, a, b), lambda i, g, c: (g, 0, 0, 0))
    return pl.pallas_call(
        _moe_kernel,
        grid_spec=pltpu.PrefetchScalarGridSpec(
            num_scalar_prefetch=1,
            grid=(t // tm, N_GROUPS),
            in_specs=[pl.BlockSpec((tm, d), row), pl.BlockSpec((tm, LANES), row),
                      pl.BlockSpec((LANES, tm), lambda i, g, c: (0, i)), pl.BlockSpec((1, d), const),
                      wspec(d, de), wspec(d, de), wspec(de, d), pl.BlockSpec((1, d), const)],
            out_specs=pl.BlockSpec((tm, d), row),
            scratch_shapes=[pltpu.VMEM((rows, tm), BF16), pltpu.VMEM((tm, rows), BF16), pltpu.VMEM((rows, d), BF16),
                            pltpu.VMEM((rows, LANES), F32)]),
        out_shape=jax.ShapeDtypeStruct((t, d), F32),
        compiler_params=_params(("arbitrary", "arbitrary")),
        name="moe",
    )(cnt, x1, route, routet, w["ffn_g"], w["w_eg"], w["w_eu"], w["w_ed"], w["final_g"])


ROW_TILE = 1024
WIN_TILE = 512
GLOB_TQ = 256
GLOB_TK = 256


def _prepare(attn_g, w_in, sink, gq, gk, w_oa, w_ob, w_out, ffn_g, w_rg, b_rg, w_re, b_re, w_eg, w_eu, w_ed,
             final_g):
    d = w_in.shape[0]
    w_router, b_router = _router_weights(w_rg, b_rg, w_re, b_re)
    grouped = lambda a: a.astype(BF16).reshape((N_GROUPS, EXPERTS_PER_GROUP) + a.shape[1:])
    return dict(
        attn_g=attn_g.reshape(1, d), w_in=w_in.astype(BF16), sink=sink.astype(F32),
        gq=jnp.tile(gq.astype(F32), Q_HEADS).reshape(1, Q_W), gk=jnp.tile(gk.astype(F32), KV_HEADS).reshape(1, KV_W),
        w_oa=w_oa.astype(BF16), w_ob=w_ob.astype(BF16), w_out=w_out.astype(BF16), ffn_g=ffn_g.reshape(1, d),
        w_router=w_router, b_router=b_router, w_eg=grouped(w_eg), w_eu=grouped(w_eu), w_ed=grouped(w_ed),
        final_g=final_g.reshape(1, d))


def _trunk(x, w):
    b, s, d = x.shape
    t = b * s
    assert s % ROW_TILE == 0 and s % (2 * GLOB_TK) == 0, "sequence length must be a multiple of the row tile"
    x2 = x.reshape(t, d)
    qa, ka, va, qb, kb, vb, ga, gb = _in_proj(x2, s, w, ROW_TILE)
    ya = _win_attn(qa, ka, va, w["sink"], s, WIN_TILE)
    yb = _glob_attn(qb.reshape(b, s, Q_W), kb.reshape(b, s, KV_W), vb.reshape(b, s, KV_W),
                    GLOB_TQ, GLOB_TK).reshape(t, Q_W)
    x1, route, routet, cnt = _merge(x2, ya, yb, ga, gb, w, ROW_TILE)
    y = _moe(x1, route, routet, cnt[:, :N_GROUPS, 0], w, ROW_TILE)
    return y.reshape(b, s, d)


def kernel(x_prompt, x_sample, attn_norm_g, w_in, a_sink, b_q_norm_g, b_k_norm_g, w_oa, w_ob, w_out, ffn_norm_g,
           w_router_group, b_router_group, w_router_expert, b_router_expert, w_expert_gate, w_expert_up,
           w_expert_down, final_norm_g):
    assert attn_norm_g.shape[0] == 1, "single-layer trunk"
    w = _prepare(attn_norm_g[0], w_in[0], a_sink[0], b_q_norm_g[0], b_k_norm_g[0], w_oa[0], w_ob[0], w_out[0],
                 ffn_norm_g[0], w_router_group[0], b_router_group[0], w_router_expert[0], b_router_expert[0],
                 w_expert_gate[0], w_expert_up[0], w_expert_down[0], final_norm_g)
    return (_trunk(x_prompt, w), _trunk(x_sample, w))
```

```python
import functools

import jax
import jax.numpy as jnp
import numpy as np
from jax import lax
from jax.experimental import pallas as pl
from jax.experimental.pallas import tpu as pltpu

HEAD_DIM = 64
Q_HEADS = 8
KV_HEADS = 2
GROUP = Q_HEADS // KV_HEADS
Q_W = Q_HEADS * HEAD_DIM
KV_W = KV_HEADS * HEAD_DIM
WINDOW = 128
BLOCK = 128
GRID_W = 64
ROPE_THETA = 10000.0
N_GROUPS = 4
EXPERTS_PER_GROUP = 4
N_EXPERTS = N_GROUPS * EXPERTS_PER_GROUP
EPS = 1e-6
NEG_INF = -1e30
SCALE = HEAD_DIM ** -0.5
LOG2E = 1.4426950408889634
ONES_ROWS = 16
VT_ROWS = HEAD_DIM + ONES_ROWS
SCORE_DTYPE = jnp.bfloat16
LANES = 128

VMEM_LIMIT = 56 * 1024 * 1024

F32 = jnp.float32
BF16 = jnp.bfloat16


def _params(sem):
    return pltpu.CompilerParams(dimension_semantics=sem, vmem_limit_bytes=VMEM_LIMIT)


def _full(shape):
    return pl.BlockSpec(shape, lambda *_: (0,) * len(shape))


def _hi_lo(v):
    top = lax.bitcast_convert_type(lax.bitcast_convert_type(v, jnp.uint32) & jnp.uint32(0xFFFF0000), F32)
    return top.astype(BF16), (v - top).astype(BF16)


def _split_dot(y, ones_bd):
    hi, lo = _hi_lo(y)
    return jnp.dot(hi, ones_bd, preferred_element_type=F32) + jnp.dot(lo, ones_bd, preferred_element_type=F32)


def _pair_swap(y):
    n = y.shape[-1]
    lane = lax.broadcasted_iota(jnp.int32, y.shape, y.ndim - 1)
    nxt = pltpu.roll(y, n - 1, y.ndim - 1)
    prv = pltpu.roll(y, 1, y.ndim - 1)
    return jnp.where((lane & 1) == 0, nxt, prv)


def _norm_rope(y, gain, ones_bd, cos, sin):
    ms = _split_dot(y * y, ones_bd) * (1.0 / HEAD_DIM)
    yn = y * lax.rsqrt(ms + EPS) * gain
    return yn * cos + _pair_swap(yn) * sin


def _in_proj_kernel(x_ref, g_ref, w_ref, cos_ref, sin_ref, gq_ref, gk_ref, oq_ref, ok_ref,
                    qa_ref, ka_ref, va_ref, qb_ref, kb_ref, vb_ref, ga_ref, gb_ref):
    x = x_ref[...]
    d = x.shape[1]
    ms = jnp.mean(x * x, axis=-1, keepdims=True)
    h = (x * lax.rsqrt(ms + EPS) * g_ref[...]).astype(BF16)

    def proj(lo, width):
        return jnp.dot(h, w_ref[:, lo:lo + width], preferred_element_type=F32)

    o = 0
    qa_ref[...] = (proj(o, Q_W) * (SCALE * LOG2E)).astype(BF16); o += Q_W
    ka_ref[...] = proj(o, KV_W).astype(BF16); o += KV_W
    va_ref[...] = proj(o, KV_W).astype(BF16); o += KV_W
    cos = cos_ref[...]
    sin = sin_ref[...]
    qb = proj(o, Q_W); o += Q_W
    cos_q = jnp.concatenate([cos] * (Q_W // LANES), axis=1)
    sin_q = jnp.concatenate([sin] * (Q_W // LANES), axis=1)
    qb_ref[...] = (_norm_rope(qb, gq_ref[...], oq_ref[...], cos_q, sin_q) * (SCALE * LOG2E)).astype(BF16)
    kb = proj(o, KV_W); o += KV_W
    kb_ref[...] = _norm_rope(kb, gk_ref[...], ok_ref[...], cos, sin).astype(BF16)
    vb_ref[...] = proj(o, KV_W).astype(BF16); o += KV_W
    ga_ref[...] = jax.nn.sigmoid(proj(o, d)).astype(BF16); o += d
    gb_ref[...] = jax.nn.sigmoid(proj(o, d)).astype(BF16)


def _rope_tables(seq_len):
    rows = seq_len // GRID_W
    row = jnp.repeat(jnp.arange(rows, dtype=F32), GRID_W)
    col = jnp.tile(jnp.arange(GRID_W, dtype=F32), rows)
    half = HEAD_DIM // 2
    inv = ROPE_THETA ** (-jnp.arange(0, half, 2, dtype=F32) / half)
    ang = jnp.concatenate([row[:, None] * inv, col[:, None] * inv], axis=-1)
    cos = jnp.repeat(jnp.cos(ang), 2, axis=-1)
    sin = jnp.repeat(jnp.sin(ang), 2, axis=-1) * jnp.tile(jnp.array([-1.0, 1.0], F32), half)
    reps = LANES // HEAD_DIM
    return jnp.tile(cos, (1, reps)), jnp.tile(sin, (1, reps))


def _block_ones(width):
    idx = np.arange(width) // HEAD_DIM
    return jnp.asarray(idx[:, None] == idx[None, :], dtype=BF16)


def _in_proj(x2, seq_len, w, tm):
    t, d = x2.shape
    in_w = w["w_in"].shape[1]
    cos, sin = _rope_tables(seq_len)
    nseq = seq_len // tm
    row = lambda i: (i, 0)
    pos = lambda i: (i % nseq, 0)
    outs = [(Q_W, BF16), (KV_W, BF16), (KV_W, BF16), (Q_W, BF16), (KV_W, BF16), (KV_W, BF16), (d, BF16), (d, BF16)]
    return pl.pallas_call(
        _in_proj_kernel,
        grid=(t // tm,),
        in_specs=[pl.BlockSpec((tm, d), row), _full((1, d)), _full((d, in_w)),
                  pl.BlockSpec((tm, LANES), pos), pl.BlockSpec((tm, LANES), pos),
                  _full((1, Q_W)), _full((1, KV_W)), _full((Q_W, Q_W)), _full((KV_W, KV_W))],
        out_specs=[pl.BlockSpec((tm, w), row) for w, _ in outs],
        out_shape=[jax.ShapeDtypeStruct((t, w), dt) for w, dt in outs],
        compiler_params=_params(("parallel",)),
        name="in_proj",
    )(x2, w["attn_g"], w["w_in"], cos, sin, w["gq"], w["gk"], _block_ones(Q_W), _block_ones(KV_W))


PAIRS = Q_HEADS // 2
N_EDGE = 4


def _win_bias():
    slopes = np.exp2(-8.0 * np.arange(1, Q_HEADS + 1, dtype=np.float32) / Q_HEADS).astype(np.float32)
    key = np.arange(3 * BLOCK)[:, None]
    dist = np.abs(np.arange(BLOCK)[None, :] - (key - BLOCK))
    tabs = []
    for edge in range(N_EDGE):
        alive = dist <= WINDOW
        if edge & 1:
            alive = alive & (key >= BLOCK)
        if edge & 2:
            alive = alive & (key < 2 * BLOCK)
        heads = [np.where(alive, -slopes[h] * np.float32(LOG2E) * dist.astype(np.float32), np.float32(NEG_INF))
                 for h in range(Q_HEADS)]
        tabs.append(np.stack([np.concatenate([heads[2 * p], heads[2 * p + 1]], axis=1) for p in range(PAIRS)]))
    return jnp.asarray(np.stack(tabs), dtype=F32)


def _win_attn_kernel(sink_ref, q_ref, kc_ref, kp_ref, kn_ref, vc_ref, vp_ref, vn_ref, bias_ref, o_ref,
                     kh_scr, vt_scr, qp_scr, s_scr, p_scr, ot_scr, *, blocks_per_seq, sub):
    i = pl.program_id(0)
    tq = q_ref.shape[0]
    ones = jnp.ones((ONES_ROWS, BLOCK), BF16)
    for off, n, kr, vr in ((0, BLOCK, kp_ref, vp_ref), (BLOCK, tq, kc_ref, vc_ref), (BLOCK + tq, BLOCK, kn_ref, vn_ref)):
        k = kr[...]
        vt = vr[...].astype(F32).T.astype(BF16)
        for g in range(KV_HEADS):
            kh_scr[g, off:off + n, :] = k[:, g * HEAD_DIM:(g + 1) * HEAD_DIM]
            vt_scr[g * VT_ROWS:g * VT_ROWS + HEAD_DIM, off:off + n] = vt[g * HEAD_DIM:(g + 1) * HEAD_DIM, :]
            for o in range(off, off + n, BLOCK):
                vt_scr[g * VT_ROWS + HEAD_DIM:(g + 1) * VT_ROWS, o:o + BLOCK] = ones
    qt = q_ref[...].astype(F32).T.astype(BF16)
    for j in range(sub):
        for p in range(PAIRS):
            for u in range(2):
                h = 2 * p + u
                qp_scr[j, p, :, u * BLOCK:(u + 1) * BLOCK] = qt[h * HEAD_DIM:(h + 1) * HEAD_DIM,
                                                                j * BLOCK:(j + 1) * BLOCK]
    lane = lax.broadcasted_iota(jnp.int32, (1, 2 * BLOCK), 1)
    for j in range(sub):
        x = j % 2
        blk = (i * sub + j) % blocks_per_seq
        edge = (blk == 0).astype(jnp.int32) + 2 * (blk == blocks_per_seq - 1).astype(jnp.int32)
        for p in range(PAIRS):
            g = (2 * p) // GROUP
            s = jnp.dot(kh_scr[g, j * BLOCK:(j + 3) * BLOCK, :], qp_scr[j, p], preferred_element_type=F32)
            s_scr[x, p] = s + bias_ref[edge, p]
        sink_terms = []
        for p in range(PAIRS):
            logits = s_scr[x, p]
            sink = jnp.where(lane < BLOCK, sink_ref[2 * p], sink_ref[2 * p + 1]) * LOG2E
            m = jnp.maximum(jnp.max(logits, axis=0, keepdims=True), sink)
            p_scr[x, p] = jnp.exp2(logits - m).astype(BF16)
            sink_terms.append(jnp.exp2(sink - m))
        for p in range(PAIRS):
            g = (2 * p) // GROUP
            ot = jnp.dot(vt_scr[g * VT_ROWS:(g + 1) * VT_ROWS, j * BLOCK:(j + 3) * BLOCK], p_scr[x, p],
                         preferred_element_type=F32)
            out = ot[0:HEAD_DIM] / (ot[HEAD_DIM:HEAD_DIM + 1] + sink_terms[p])
            for u in range(2):
                h = 2 * p + u
                ot_scr[h * HEAD_DIM:(h + 1) * HEAD_DIM, j * BLOCK:(j + 1) * BLOCK] = out[:, u * BLOCK:(u + 1) * BLOCK]
    o_ref[...] = ot_scr[...].T.astype(o_ref.dtype)


def _win_attn(qa, ka, va, sink, seq_len, tq):
    t = qa.shape[0]
    sub = tq // BLOCK
    nblk = t // BLOCK
    win = tq + 2 * BLOCK
    row = lambda i: (i, 0)
    prev = lambda i: (jnp.maximum(i * sub - 1, 0), 0)
    nxt = lambda i: (jnp.minimum((i + 1) * sub, nblk - 1), 0)
    kv_specs = [pl.BlockSpec((tq, KV_W), row), pl.BlockSpec((BLOCK, KV_W), prev), pl.BlockSpec((BLOCK, KV_W), nxt)]
    bias_spec = pl.BlockSpec((N_EDGE, PAIRS, 3 * BLOCK, 2 * BLOCK), lambda i: (0, 0, 0, 0),
                             pipeline_mode=pl.Buffered(1))
    return pl.pallas_call(
        functools.partial(_win_attn_kernel, blocks_per_seq=seq_len // BLOCK, sub=sub),
        grid=(t // tq,),
        in_specs=[pl.BlockSpec(memory_space=pltpu.SMEM), pl.BlockSpec((tq, Q_W), row)] + kv_specs + kv_specs
                 + [bias_spec],
        out_specs=pl.BlockSpec((tq, Q_W), row),
        out_shape=jax.ShapeDtypeStruct((t, Q_W), BF16),
        scratch_shapes=[pltpu.VMEM((KV_HEADS, win, HEAD_DIM), BF16), pltpu.VMEM((KV_HEADS * VT_ROWS, win), BF16),
                        pltpu.VMEM((sub, PAIRS, HEAD_DIM, 2 * BLOCK), BF16),
                        pltpu.VMEM((2, PAIRS, 3 * BLOCK, 2 * BLOCK), F32),
                        pltpu.VMEM((2, PAIRS, 3 * BLOCK, 2 * BLOCK), BF16), pltpu.VMEM((Q_W, tq), F32)],
        compiler_params=_params(("parallel",)),
        name="win_attn",
    )(sink, qa, ka, ka, ka, va, va, va, _win_bias())


def _glob_attn_kernel(q_ref, k_ref, v_ref, o_ref, kh_ref, vt_ref, qt_scr, s_scr, p_scr, cm_scr, m_scr, a_scr, acc_scr,
                      ot_scr, *, tk):
    tq = q_ref.shape[1]
    nk = k_ref.shape[1] // tk

    @pl.when(pl.program_id(1) == 0)
    def _():
        ones = jnp.ones((ONES_ROWS, tk), BF16)

        def fill(c, carry):
            off = pl.multiple_of(c * tk, tk)
            vt = v_ref[0, pl.ds(off, tk), :].astype(F32).T.astype(BF16)
            kc = k_ref[0, pl.ds(off, tk), :]
            for g in range(KV_HEADS):
                kh_ref[g, pl.ds(off, tk), :] = kc[:, g * HEAD_DIM:(g + 1) * HEAD_DIM]
                vt_ref[g * VT_ROWS:g * VT_ROWS + HEAD_DIM, pl.ds(off, tk)] = vt[g * HEAD_DIM:(g + 1) * HEAD_DIM, :]
                vt_ref[g * VT_ROWS + HEAD_DIM:(g + 1) * VT_ROWS, pl.ds(off, tk)] = ones
            return carry

        lax.fori_loop(0, nk, fill, 0)

    qt_scr[...] = q_ref[0].astype(F32).T.astype(BF16)

    for g in range(KV_HEADS):
        heads = range(g * GROUP, (g + 1) * GROUP)

        def stage_a(c, x, heads=heads, g=g):
            kc = kh_ref[g, pl.ds(pl.multiple_of(c * tk, tk), tk), :]
            for i, h in enumerate(heads):
                s = jnp.dot(kc, qt_scr[h * HEAD_DIM:(h + 1) * HEAD_DIM, :], preferred_element_type=F32)
                s_scr[x, i] = s.astype(s_scr.dtype)
                cm_scr[x, i] = jnp.max(s, axis=0, keepdims=True)

        def stage_b(x):
            for i in range(GROUP):
                m = m_scr[i]
                m_new = jnp.maximum(m, cm_scr[x, i])
                a_scr[x, i] = jnp.exp2(m - m_new)
                m_scr[i] = m_new
                p_scr[x, i] = jnp.exp2(s_scr[x, i] - m_new.astype(s_scr.dtype)).astype(BF16)

        def stage_c(c, x, g=g):
            vt = vt_ref[g * VT_ROWS:(g + 1) * VT_ROWS, pl.ds(pl.multiple_of(c * tk, tk), tk)]
            for i in range(GROUP):
                acc_scr[i] = a_scr[x, i] * acc_scr[i] + jnp.dot(vt, p_scr[x, i], preferred_element_type=F32)

        def step(c, x, last=False):
            stage_c(c - 1, 1 - x)
            stage_b(x)
            if not last:
                stage_a(c + 1, 1 - x)

        for i in range(GROUP):
            m_scr[i] = jnp.full((1, tq), -jnp.inf, F32)
            acc_scr[i] = jnp.zeros((VT_ROWS, tq), F32)
        stage_a(0, 0)
        stage_b(0)
        stage_a(1, 1)

        def pair(j, carry):
            step(2 * j + 1, 1)
            step(2 * j + 2, 0)
            return carry

        lax.fori_loop(0, (nk - 2) // 2, pair, 0, unroll=8)
        step(nk - 1, 1, last=True)
        stage_c(nk - 1, 1)
        for i, h in enumerate(heads):
            acc = acc_scr[i]
            ot_scr[h * HEAD_DIM:(h + 1) * HEAD_DIM, :] = acc[0:HEAD_DIM] / acc[HEAD_DIM:HEAD_DIM + 1]
    o_ref[0] = ot_scr[...].T.astype(o_ref.dtype)


def _glob_attn(qb, kb, vb, tq, tk):
    b, s, _ = qb.shape
    assert s % (2 * tk) == 0, "the chunk pipeline walks key chunks in pairs"
    return pl.pallas_call(
        functools.partial(_glob_attn_kernel, tk=tk),
        grid=(b, s // tq),
        in_specs=[pl.BlockSpec((1, tq, Q_W), lambda bi, qi: (bi, qi, 0)),
                  pl.BlockSpec((1, s, KV_W), lambda bi, qi: (bi, 0, 0)),
                  pl.BlockSpec((1, s, KV_W), lambda bi, qi: (bi, 0, 0))],
        out_specs=pl.BlockSpec((1, tq, Q_W), lambda bi, qi: (bi, qi, 0)),
        out_shape=jax.ShapeDtypeStruct((b, s, Q_W), BF16),
        scratch_shapes=[pltpu.VMEM((KV_HEADS, s, HEAD_DIM), BF16), pltpu.VMEM((KV_HEADS * VT_ROWS, s), BF16),
                        pltpu.VMEM((Q_W, tq), BF16),
                        pltpu.VMEM((2, GROUP, tk, tq), SCORE_DTYPE), pltpu.VMEM((2, GROUP, tk, tq), BF16),
                        pltpu.VMEM((2, GROUP, 1, tq), F32), pltpu.VMEM((GROUP, 1, tq), F32),
                        pltpu.VMEM((2, GROUP, 1, tq), F32), pltpu.VMEM((GROUP, VT_ROWS, tq), F32),
                        pltpu.VMEM((Q_W, tq), F32)],
        compiler_params=_params(("arbitrary", "arbitrary")),
        name="glob_attn",
    )(qb, kb, vb)


def _first_argmax(vals, idx, big):
    m = jnp.max(vals, axis=0, keepdims=True)
    return m, jnp.min(jnp.where(vals == m, idx, big), axis=0, keepdims=True)


def _route(lt):
    n = lt.shape[1]
    gl = lt[0:N_GROUPS]
    idx = lax.broadcasted_iota(jnp.int32, (N_GROUPS, n), 0)
    gmax, g_sel = _first_argmax(gl, idx, N_GROUPS)
    g_w = 1.0 / jnp.sum(jnp.exp(gl - gmax), axis=0, keepdims=True)
    e_sel = jnp.zeros((EXPERTS_PER_GROUP, n), F32)
    for g in range(N_GROUPS):
        lo = N_GROUPS + g * EXPERTS_PER_GROUP
        e_sel = jnp.where(g_sel == g, lt[lo:lo + EXPERTS_PER_GROUP], e_sel)
    ex = jnp.exp(e_sel - jnp.max(e_sel, axis=0, keepdims=True))
    e_prob = ex / jnp.sum(ex, axis=0, keepdims=True)
    p1, i1 = _first_argmax(e_prob, idx, EXPERTS_PER_GROUP)
    rest = jnp.where(idx == i1, -1.0, e_prob)
    p2, i2 = _first_argmax(rest, idx, EXPERTS_PER_GROUP)
    tot = p1 + p2
    row = lax.broadcasted_iota(jnp.int32, (2 * N_GROUPS, n), 0)
    out = jnp.where(row == g_sel, 1.0, 0.0)
    out = out + jnp.where(row == N_GROUPS + i1, g_w * (p1 / tot), 0.0)
    return out + jnp.where(row == N_GROUPS + i2, g_w * (p2 / tot), 0.0)


def _merge_kernel(x_ref, ya_ref, yb_ref, ga_ref, gb_ref, woa_ref, wob_ref, wout_ref, gf_ref, wr_ref, br_ref,
                  x1_ref, route_ref, routet_ref, cnt_ref):
    tm = x_ref.shape[0]
    a = jnp.dot(ya_ref[...], woa_ref[...], preferred_element_type=F32)
    b = jnp.dot(yb_ref[...], wob_ref[...], preferred_element_type=F32)
    merged = ga_ref[...].astype(F32) * a + gb_ref[...].astype(F32) * b
    x1 = x_ref[...] + jnp.dot(merged.astype(BF16), wout_ref[...], preferred_element_type=F32)
    x1_ref[...] = x1
    t = _rmsnorm(x1, gf_ref[...])
    t_hi, t_lo = _hi_lo(t)
    big = jnp.dot(t_hi, wr_ref[...], preferred_element_type=F32)
    small = jnp.dot(t_lo, wr_ref[:, 0:LANES], preferred_element_type=F32)
    logits = big[:, 0:LANES] + big[:, LANES:2 * LANES] + small + br_ref[...]
    block = _route(logits.T)
    route_t = jnp.concatenate([block, jnp.zeros((LANES - block.shape[0], tm), F32)], axis=0)
    routet_ref[...] = route_t
    route_ref[...] = route_t.T
    cnt = jnp.sum(block, axis=1, keepdims=True)
    cnt_ref[0] = jnp.broadcast_to(cnt, cnt_ref.shape[1:]).astype(jnp.int32)


def _router_weights(w_rg, b_rg, w_re, b_re):
    d = w_rg.shape[0]
    wr = jnp.zeros((d, LANES), F32).at[:, :N_GROUPS].set(w_rg).at[:, N_GROUPS:N_GROUPS + N_EXPERTS].set(w_re)
    br = jnp.zeros((1, LANES), F32).at[0, :N_GROUPS].set(b_rg).at[0, N_GROUPS:N_GROUPS + N_EXPERTS].set(b_re)
    return jnp.concatenate(_hi_lo(wr), axis=1), br


def _merge(x2, ya, yb, ga, gb, w, tm):
    t, d = x2.shape
    row = lambda i: (i, 0)
    return pl.pallas_call(
        _merge_kernel,
        grid=(t // tm,),
        in_specs=[pl.BlockSpec((tm, d), row), pl.BlockSpec((tm, Q_W), row), pl.BlockSpec((tm, Q_W), row),
                  pl.BlockSpec((tm, d), row), pl.BlockSpec((tm, d), row),
                  _full((Q_W, d)), _full((Q_W, d)), _full((d, d)), _full((1, d)), _full((d, 2 * LANES)),
                  _full((1, LANES))],
        out_specs=[pl.BlockSpec((tm, d), row), pl.BlockSpec((tm, LANES), row),
                   pl.BlockSpec((LANES, tm), lambda i: (0, i)), pl.BlockSpec((1, 8, LANES), lambda i: (i, 0, 0))],
        out_shape=[jax.ShapeDtypeStruct((t, d), F32), jax.ShapeDtypeStruct((t, LANES), F32),
                   jax.ShapeDtypeStruct((LANES, t), F32), jax.ShapeDtypeStruct((t // tm, 8, LANES), jnp.int32)],
        compiler_params=_params(("parallel",)),
        name="merge",
    )(x2, ya, yb, ga, gb, w["w_oa"], w["w_ob"], w["w_out"], w["ffn_g"], w["w_router"], w["b_router"])


ROW_ALIGN = 32
BIG_UNITS = 8
BIG_BLOCK = BIG_UNITS * ROW_ALIGN
WHOLE_UNITS = (7, 8, 9, 10)
P_BUILD_ROWS = 128


def _rmsnorm(x, g):
    ms = jnp.mean(x * x, axis=-1, keepdims=True)
    return x * lax.rsqrt(ms + EPS) * g


def _moe_kernel(cnt_ref, x1_ref, route_ref, routet_ref, gf_ref, wg_ref, wu_ref, wd_ref, gfin_ref, o_ref,
                p_scr, q_scr, ts_scr, ws_scr):
    i = pl.program_id(0)
    g = pl.program_id(1)
    tm = x1_ref.shape[0]
    rows = p_scr.shape[0]
    offs, off = [], 0
    for gg in range(N_GROUPS):
        offs.append(off)
        off = off + (cnt_ref[i, gg] + (ROW_ALIGN - 1)) // ROW_ALIGN * ROW_ALIGN

    @pl.when(g == 0)
    def _():
        route = route_ref[...]
        lane = lax.broadcasted_iota(jnp.int32, (tm, LANES), 1)
        onehot = jnp.where(lane < N_GROUPS, route, 0.0)
        r_i = lax.broadcasted_iota(jnp.int32, (tm, tm), 0)
        c_i = lax.broadcasted_iota(jnp.int32, (tm, tm), 1)
        before = jnp.dot(jnp.where(c_i < r_i, 1.0, 0.0).astype(BF16), onehot.astype(BF16),
                         preferred_element_type=F32)
        goff = jnp.zeros((tm, LANES), F32)
        for gg in range(N_GROUPS):
            goff = jnp.where(lane == gg, jnp.asarray(offs[gg], jnp.int32).astype(F32), goff)
        dest_col = jnp.sum((before + goff) * onehot, axis=1, keepdims=True).astype(jnp.int32)
        q_scr[...] = jnp.where(dest_col == lax.broadcasted_iota(jnp.int32, (tm, rows), 1), 1.0, 0.0).astype(BF16)
        sub = lax.broadcasted_iota(jnp.int32, (16, tm), 0)
        onehot_t = jnp.where(sub < N_GROUPS, routet_ref[0:16, :], 0.0)
        before_t = jnp.dot(onehot_t.astype(BF16), jnp.where(r_i < c_i, 1.0, 0.0).astype(BF16),
                           preferred_element_type=F32)
        goff_t = jnp.zeros((16, tm), F32)
        for gg in range(N_GROUPS):
            goff_t = jnp.where(sub == gg, jnp.asarray(offs[gg], jnp.int32).astype(F32), goff_t)
        dest_row = jnp.sum((before_t + goff_t) * onehot_t, axis=0, keepdims=True).astype(jnp.int32)
        for r0 in range(0, rows, P_BUILD_ROWS):
            rid = lax.broadcasted_iota(jnp.int32, (P_BUILD_ROWS, tm), 0) + r0
            p_scr[r0:r0 + P_BUILD_ROWS, :] = jnp.where(dest_row == rid, 1.0, 0.0).astype(BF16)
        t = _rmsnorm(x1_ref[...], gf_ref[...]).astype(BF16)
        r_hi_lo = jnp.concatenate(_hi_lo(route), axis=1)
        gather = rows // 2
        for r0 in range(0, rows, gather):
            pb = p_scr[r0:r0 + gather, :]
            ts_scr[r0:r0 + gather, :] = jnp.dot(pb, t, preferred_element_type=F32).astype(BF16)
            w2 = jnp.dot(pb, r_hi_lo, preferred_element_type=F32)
            ws_scr[r0:r0 + gather, :] = w2[:, 0:LANES] + w2[:, LANES:2 * LANES]

    def ffn(r, m):
        tb = ts_scr[pl.ds(pl.multiple_of(r, ROW_ALIGN), m), :]
        wb = ws_scr[pl.ds(pl.multiple_of(r, ROW_ALIGN), m), :]
        y = jnp.zeros((m, o_ref.shape[1]), F32)
        for e in range(EXPERTS_PER_GROUP):
            a = jax.nn.silu(jnp.dot(tb, wg_ref[0, e], preferred_element_type=F32)) * jnp.dot(
                tb, wu_ref[0, e], preferred_element_type=F32)
            y = y + wb[:, N_GROUPS + e:N_GROUPS + e + 1] * jnp.dot(a.astype(BF16), wd_ref[0, e],
                                                                   preferred_element_type=F32)
        ts_scr[pl.ds(pl.multiple_of(r, ROW_ALIGN), m), :] = y.astype(BF16)

    start = offs[0]
    for gg in range(1, N_GROUPS):
        start = jnp.where(g == gg, offs[gg], start)
    n_small = (cnt_ref[i, g] + (ROW_ALIGN - 1)) // ROW_ALIGN

    for units in WHOLE_UNITS:
        @pl.when(n_small == units)
        def _(units=units):
            ffn(start, units * ROW_ALIGN)

    @pl.when((n_small < WHOLE_UNITS[0]) | (n_small > WHOLE_UNITS[-1]))
    def _():
        n_big = n_small // BIG_UNITS

        def big(j, carry):
            ffn(start + j * BIG_BLOCK, BIG_BLOCK)
            return carry

        lax.fori_loop(0, n_big, big, 0)

        def unit(j, carry):
            ffn(start + j * ROW_ALIGN, ROW_ALIGN)
            return carry

        lax.fori_loop(n_big * BIG_UNITS, n_small, unit, 0)

    @pl.when(g == N_GROUPS - 1)
    def _():
        x2 = x1_ref[...] + jnp.dot(q_scr[...], ts_scr[...], preferred_element_type=F32)
        o_ref[...] = _rmsnorm(x2, gfin_ref[...])


def _moe(x1, route, routet, cnt, w, tm):
    t, d = x1.shape
    de = w["w_eg"].shape[-1]
    rows = tm + N_GROUPS * ROW_ALIGN
    row = lambda i, g, c: (i, 0)
    const = lambda i, g, c: (0, 0)
    wspec = lambda a, b: pl.BlockSpec((1, EXPERTS_PER_GROUP, a, b), lambda i, g, c: (g, 0, 0, 0))
    return pl.pallas_call(
        _moe_kernel,
        grid_spec=pltpu.PrefetchScalarGridSpec(
            num_scalar_prefetch=1,
            grid=(t // tm, N_GROUPS),
            in_specs=[pl.BlockSpec((tm, d), row), pl.BlockSpec((tm, LANES), row),
                      pl.BlockSpec((LANES, tm), lambda i, g, c: (0, i)), pl.BlockSpec((1, d), const),
                      wspec(d, de), wspec(d, de), wspec(de, d), pl.BlockSpec((1, d), const)],
            out_specs=pl.BlockSpec((tm, d), row),
            scratch_shapes=[pltpu.VMEM((rows, tm), BF16), pltpu.VMEM((tm, rows), BF16), pltpu.VMEM((rows, d), BF16),
                            pltpu.VMEM((rows, LANES), F32)]),
        out_shape=jax.ShapeDtypeStruct((t, d), F32),
        compiler_params=_params(("arbitrary", "arbitrary")),
        name="moe",
    )(cnt, x1, route, routet, w["ffn_g"], w["w_eg"], w["w_eu"], w["w_ed"], w["final_g"])


ROW_TILE = 1024
WIN_TILE = 512
GLOB_TQ = 256
GLOB_TK = 256


def _prepare(attn_g, w_in, sink, gq, gk, w_oa, w_ob, w_out, ffn_g, w_rg, b_rg, w_re, b_re, w_eg, w_eu, w_ed,
             final_g):
    d = w_in.shape[0]
    w_router, b_router = _router_weights(w_rg, b_rg, w_re, b_re)
    grouped = lambda a: a.astype(BF16).reshape((N_GROUPS, EXPERTS_PER_GROUP) + a.shape[1:])
    return dict(
        attn_g=attn_g.reshape(1, d), w_in=w_in.astype(BF16), sink=sink.astype(F32),
        gq=jnp.tile(gq.astype(F32), Q_HEADS).reshape(1, Q_W), gk=jnp.tile(gk.astype(F32), KV_HEADS).reshape(1, KV_W),
        w_oa=w_oa.astype(BF16), w_ob=w_ob.astype(BF16), w_out=w_out.astype(BF16), ffn_g=ffn_g.reshape(1, d),
        w_router=w_router, b_router=b_router, w_eg=grouped(w_eg), w_eu=grouped(w_eu), w_ed=grouped(w_ed),
        final_g=final_g.reshape(1, d))


def _trunk(x, w):
    b, s, d = x.shape
    t = b * s
    assert s % ROW_TILE == 0 and s % (2 * GLOB_TK) == 0, "sequence length must be a multiple of the row tile"
    x2 = x.reshape(t, d)
    qa, ka, va, qb, kb, vb, ga, gb = _in_proj(x2, s, w, ROW_TILE)
    ya = _win_attn(qa, ka, va, w["sink"], s, WIN_TILE)
    yb = _glob_attn(qb.reshape(b, s, Q_W), kb.reshape(b, s, KV_W), vb.reshape(b, s, KV_W),
                    GLOB_TQ, GLOB_TK).reshape(t, Q_W)
    x1, route, routet, cnt = _merge(x2, ya, yb, ga, gb, w, ROW_TILE)
    y = _moe(x1, route, routet, cnt[:, :N_GROUPS, 0], w, ROW_TILE)
    return y.reshape(b, s, d)


def kernel(x_prompt, x_sample, attn_norm_g, w_in, a_sink, b_q_norm_g, b_k_norm_g, w_oa, w_ob, w_out, ffn_norm_g,
           w_router_group, b_router_group, w_router_expert, b_router_expert, w_expert_gate, w_expert_up,
           w_expert_down, final_norm_g):
    assert attn_norm_g.shape[0] == 1, "single-layer trunk"
    w = _prepare(attn_norm_g[0], w_in[0], a_sink[0], b_q_norm_g[0], b_k_norm_g[0], w_oa[0], w_ob[0], w_out[0],
                 ffn_norm_g[0], w_router_group[0], b_router_group[0], w_router_expert[0], b_router_expert[0],
                 w_expert_gate[0], w_expert_up[0], w_expert_down[0], final_norm_g)
    return (_trunk(x_prompt, w), _trunk(x_sample, w))
```

```python
import functools

import jax
import jax.numpy as jnp
import numpy as np
from jax import lax
from jax.experimental import pallas as pl
from jax.experimental.pallas import tpu as pltpu

HEAD_DIM = 64
Q_HEADS = 8
KV_HEADS = 2
GROUP = Q_HEADS // KV_HEADS
Q_W = Q_HEADS * HEAD_DIM
KV_W = KV_HEADS * HEAD_DIM
WINDOW = 128
BLOCK = 128
GRID_W = 64
ROPE_THETA = 10000.0
N_GROUPS = 4
EXPERTS_PER_GROUP = 4
N_EXPERTS = N_GROUPS * EXPERTS_PER_GROUP
EPS = 1e-6
NEG_INF = -1e30
SCALE = HEAD_DIM ** -0.5
LOG2E = 1.4426950408889634
ONES_ROWS = 16
VT_ROWS = HEAD_DIM + ONES_ROWS
SCORE_DTYPE = jnp.bfloat16
LANES = 128

VMEM_LIMIT = 56 * 1024 * 1024

F32 = jnp.float32
BF16 = jnp.bfloat16


def _params(sem):
    return pltpu.CompilerParams(dimension_semantics=sem, vmem_limit_bytes=VMEM_LIMIT)


def _full(shape):
    return pl.BlockSpec(shape, lambda *_: (0,) * len(shape))


def _hi_lo(v):
    top = lax.bitcast_convert_type(lax.bitcast_convert_type(v, jnp.uint32) & jnp.uint32(0xFFFF0000), F32)
    return top.astype(BF16), (v - top).astype(BF16)


def _split_dot(y, ones_bd):
    hi, lo = _hi_lo(y)
    return jnp.dot(hi, ones_bd, preferred_element_type=F32) + jnp.dot(lo, ones_bd, preferred_element_type=F32)


def _pair_swap(y):
    n = y.shape[-1]
    lane = lax.broadcasted_iota(jnp.int32, y.shape, y.ndim - 1)
    nxt = pltpu.roll(y, n - 1, y.ndim - 1)
    prv = pltpu.roll(y, 1, y.ndim - 1)
    return jnp.where((lane & 1) == 0, nxt, prv)


def _norm_rope(y, gain, ones_bd, cos, sin):
    ms = _split_dot(y * y, ones_bd) * (1.0 / HEAD_DIM)
    yn = y * lax.rsqrt(ms + EPS) * gain
    return yn * cos + _pair_swap(yn) * sin


def _in_proj_kernel(x_ref, g_ref, w_ref, cos_ref, sin_ref, gq_ref, gk_ref, oq_ref, ok_ref,
                    qa_ref, ka_ref, va_ref, qb_ref, kb_ref, vb_ref, ga_ref, gb_ref):
    x = x_ref[...]
    d = x.shape[1]
    ms = jnp.mean(x * x, axis=-1, keepdims=True)
    h = (x * lax.rsqrt(ms + EPS) * g_ref[...]).astype(BF16)

    def proj(lo, width):
        return jnp.dot(h, w_ref[:, lo:lo + width], preferred_element_type=F32)

    o = 0
    qa_ref[...] = (proj(o, Q_W) * (SCALE * LOG2E)).astype(BF16); o += Q_W
    ka_ref[...] = proj(o, KV_W).astype(BF16); o += KV_W
    va_ref[...] = proj(o, KV_W).astype(BF16); o += KV_W
    cos = cos_ref[...]
    sin = sin_ref[...]
    qb = proj(o, Q_W); o += Q_W
    cos_q = jnp.concatenate([cos] * (Q_W // LANES), axis=1)
    sin_q = jnp.concatenate([sin] * (Q_W // LANES), axis=1)
    qb_ref[...] = (_norm_rope(qb, gq_ref[...], oq_ref[...], cos_q, sin_q) * (SCALE * LOG2E)).astype(BF16)
    kb = proj(o, KV_W); o += KV_W
    kb_ref[...] = _norm_rope(kb, gk_ref[...], ok_ref[...], cos, sin).astype(BF16)
    vb_ref[...] = proj(o, KV_W).astype(BF16); o += KV_W
    ga_ref[...] = jax.nn.sigmoid(proj(o, d)).astype(BF16); o += d
    gb_ref[...] = jax.nn.sigmoid(proj(o, d)).astype(BF16)


def _rope_tables(seq_len):
    rows = seq_len // GRID_W
    row = jnp.repeat(jnp.arange(rows, dtype=F32), GRID_W)
    col = jnp.tile(jnp.arange(GRID_W, dtype=F32), rows)
    half = HEAD_DIM // 2
    inv = ROPE_THETA ** (-jnp.arange(0, half, 2, dtype=F32) / half)
    ang = jnp.concatenate([row[:, None] * inv, col[:, None] * inv], axis=-1)
    cos = jnp.repeat(jnp.cos(ang), 2, axis=-1)
    sin = jnp.repeat(jnp.sin(ang), 2, axis=-1) * jnp.tile(jnp.array([-1.0, 1.0], F32), half)
    reps = LANES // HEAD_DIM
    return jnp.tile(cos, (1, reps)), jnp.tile(sin, (1, reps))


def _block_ones(width):
    idx = np.arange(width) // HEAD_DIM
    return jnp.asarray(idx[:, None] == idx[None, :], dtype=BF16)


def _in_proj(x2, seq_len, w, tm):
    t, d = x2.shape
    in_w = w["w_in"].shape[1]
    cos, sin = _rope_tables(seq_len)
    nseq = seq_len // tm
    row = lambda i: (i, 0)
    pos = lambda i: (i % nseq, 0)
    outs = [(Q_W, BF16), (KV_W, BF16), (KV_W, BF16), (Q_W, BF16), (KV_W, BF16), (KV_W, BF16), (d, BF16), (d, BF16)]
    return pl.pallas_call(
        _in_proj_kernel,
        grid=(t // tm,),
        in_specs=[pl.BlockSpec((tm, d), row), _full((1, d)), _full((d, in_w)),
                  pl.BlockSpec((tm, LANES), pos), pl.BlockSpec((tm, LANES), pos),
                  _full((1, Q_W)), _full((1, KV_W)), _full((Q_W, Q_W)), _full((KV_W, KV_W))],
        out_specs=[pl.BlockSpec((tm, w), row) for w, _ in outs],
        out_shape=[jax.ShapeDtypeStruct((t, w), dt) for w, dt in outs],
        compiler_params=_params(("parallel",)),
        name="in_proj",
    )(x2, w["attn_g"], w["w_in"], cos, sin, w["gq"], w["gk"], _block_ones(Q_W), _block_ones(KV_W))


PAIRS = Q_HEADS // 2
N_EDGE = 4


def _win_bias():
    slopes = np.exp2(-8.0 * np.arange(1, Q_HEADS + 1, dtype=np.float32) / Q_HEADS).astype(np.float32)
    key = np.arange(3 * BLOCK)[:, None]
    dist = np.abs(np.arange(BLOCK)[None, :] - (key - BLOCK))
    tabs = []
    for edge in range(N_EDGE):
        alive = dist <= WINDOW
        if edge & 1:
            alive = alive & (key >= BLOCK)
        if edge & 2:
            alive = alive & (key < 2 * BLOCK)
        heads = [np.where(alive, -slopes[h] * np.float32(LOG2E) * dist.astype(np.float32), np.float32(NEG_INF))
                 for h in range(Q_HEADS)]
        tabs.append(np.stack([np.concatenate([heads[2 * p], heads[2 * p + 1]], axis=1) for p in range(PAIRS)]))
    return jnp.asarray(np.stack(tabs), dtype=F32)


def _win_attn_kernel(sink_ref, q_ref, kc_ref, kp_ref, kn_ref, vc_ref, vp_ref, vn_ref, bias_ref, o_ref,
                     kh_scr, vt_scr, qp_scr, s_scr, p_scr, ot_scr, *, blocks_per_seq, sub):
    i = pl.program_id(0)
    tq = q_ref.shape[0]
    ones = jnp.ones((ONES_ROWS, BLOCK), BF16)
    for off, n, kr, vr in ((0, BLOCK, kp_ref, vp_ref), (BLOCK, tq, kc_ref, vc_ref), (BLOCK + tq, BLOCK, kn_ref, vn_ref)):
        k = kr[...]
        vt = vr[...].astype(F32).T.astype(BF16)
        for g in range(KV_HEADS):
            kh_scr[g, off:off + n, :] = k[:, g * HEAD_DIM:(g + 1) * HEAD_DIM]
            vt_scr[g * VT_ROWS:g * VT_ROWS + HEAD_DIM, off:off + n] = vt[g * HEAD_DIM:(g + 1) * HEAD_DIM, :]
            for o in range(off, off + n, BLOCK):
                vt_scr[g * VT_ROWS + HEAD_DIM:(g + 1) * VT_ROWS, o:o + BLOCK] = ones
    qt = q_ref[...].astype(F32).T.astype(BF16)
    for j in range(sub):
        for p in range(PAIRS):
            for u in range(2):
                h = 2 * p + u
                qp_scr[j, p, :, u * BLOCK:(u + 1) * BLOCK] = qt[h * HEAD_DIM:(h + 1) * HEAD_DIM,
                                                                j * BLOCK:(j + 1) * BLOCK]
    lane = lax.broadcasted_iota(jnp.int32, (1, 2 * BLOCK), 1)
    for j in range(sub):
        x = j % 2
        blk = (i * sub + j) % blocks_per_seq
        edge = (blk == 0).astype(jnp.int32) + 2 * (blk == blocks_per_seq - 1).astype(jnp.int32)
        for p in range(PAIRS):
            g = (2 * p) // GROUP
            s = jnp.dot(kh_scr[g, j * BLOCK:(j + 3) * BLOCK, :], qp_scr[j, p], preferred_element_type=F32)
            s_scr[x, p] = s + bias_ref[edge, p]
        sink_terms = []
        for p in range(PAIRS):
            logits = s_scr[x, p]
            sink = jnp.where(lane < BLOCK, sink_ref[2 * p], sink_ref[2 * p + 1]) * LOG2E
            m = jnp.maximum(jnp.max(logits, axis=0, keepdims=True), sink)
            p_scr[x, p] = jnp.exp2(logits - m).astype(BF16)
            sink_terms.append(jnp.exp2(sink - m))
        for p in range(PAIRS):
            g = (2 * p) // GROUP
            ot = jnp.dot(vt_scr[g * VT_ROWS:(g + 1) * VT_ROWS, j * BLOCK:(j + 3) * BLOCK], p_scr[x, p],
                         preferred_element_type=F32)
            out = ot[0:HEAD_DIM] / (ot[HEAD_DIM:HEAD_DIM + 1] + sink_terms[p])
            for u in range(2):
                h = 2 * p + u
                ot_scr[h * HEAD_DIM:(h + 1) * HEAD_DIM, j * BLOCK:(j + 1) * BLOCK] = out[:, u * BLOCK:(u + 1) * BLOCK]
    o_ref[...] = ot_scr[...].T.astype(o_ref.dtype)


def _win_attn(qa, ka, va, sink, seq_len, tq):
    t = qa.shape[0]
    sub = tq // BLOCK
    nblk = t // BLOCK
    win = tq + 2 * BLOCK
    row = lambda i: (i, 0)
    prev = lambda i: (jnp.maximum(i * sub - 1, 0), 0)
    nxt = lambda i: (jnp.minimum((i + 1) * sub, nblk - 1), 0)
    kv_specs = [pl.BlockSpec((tq, KV_W), row), pl.BlockSpec((BLOCK, KV_W), prev), pl.BlockSpec((BLOCK, KV_W), nxt)]
    bias_spec = pl.BlockSpec((N_EDGE, PAIRS, 3 * BLOCK, 2 * BLOCK), lambda i: (0, 0, 0, 0),
                             pipeline_mode=pl.Buffered(1))
    return pl.pallas_call(
        functools.partial(_win_attn_kernel, blocks_per_seq=seq_len // BLOCK, sub=sub),
        grid=(t // tq,),
        in_specs=[pl.BlockSpec(memory_space=pltpu.SMEM), pl.BlockSpec((tq, Q_W), row)] + kv_specs + kv_specs
                 + [bias_spec],
        out_specs=pl.BlockSpec((tq, Q_W), row),
        out_shape=jax.ShapeDtypeStruct((t, Q_W), BF16),
        scratch_shapes=[pltpu.VMEM((KV_HEADS, win, HEAD_DIM), BF16), pltpu.VMEM((KV_HEADS * VT_ROWS, win), BF16),
                        pltpu.VMEM((sub, PAIRS, HEAD_DIM, 2 * BLOCK), BF16),
                        pltpu.VMEM((2, PAIRS, 3 * BLOCK, 2 * BLOCK), F32),
                        pltpu.VMEM((2, PAIRS, 3 * BLOCK, 2 * BLOCK), BF16), pltpu.VMEM((Q_W, tq), F32)],
        compiler_params=_params(("parallel",)),
        name="win_attn",
    )(sink, qa, ka, ka, ka, va, va, va, _win_bias())


def _glob_attn_kernel(q_ref, k_ref, v_ref, o_ref, kh_ref, vt_ref, qt_scr, s_scr, p_scr, cm_scr, m_scr, a_scr, acc_scr,
                      ot_scr, *, tk):
    tq = q_ref.shape[1]
    nk = k_ref.shape[1] // tk

    @pl.when(pl.program_id(1) == 0)
    def _():
        ones = jnp.ones((ONES_ROWS, tk), BF16)

        def fill(c, carry):
            off = pl.multiple_of(c * tk, tk)
            vt = v_ref[0, pl.ds(off, tk), :].astype(F32).T.astype(BF16)
            kc = k_ref[0, pl.ds(off, tk), :]
            for g in range(KV_HEADS):
                kh_ref[g, pl.ds(off, tk), :] = kc[:, g * HEAD_DIM:(g + 1) * HEAD_DIM]
                vt_ref[g * VT_ROWS:g * VT_ROWS + HEAD_DIM, pl.ds(off, tk)] = vt[g * HEAD_DIM:(g + 1) * HEAD_DIM, :]
                vt_ref[g * VT_ROWS + HEAD_DIM:(g + 1) * VT_ROWS, pl.ds(off, tk)] = ones
            return carry

        lax.fori_loop(0, nk, fill, 0)

    qt_scr[...] = q_ref[0].astype(F32).T.astype(BF16)

    for g in range(KV_HEADS):
        heads = range(g * GROUP, (g + 1) * GROUP)

        def stage_a(c, x, heads=heads, g=g):
            kc = kh_ref[g, pl.ds(pl.multiple_of(c * tk, tk), tk), :]
            for i, h in enumerate(heads):
                s = jnp.dot(kc, qt_scr[h * HEAD_DIM:(h + 1) * HEAD_DIM, :], preferred_element_type=F32)
                s_scr[x, i] = s.astype(s_scr.dtype)
                cm_scr[x, i] = jnp.max(s, axis=0, keepdims=True)

        def stage_b(x):
            for i in range(GROUP):
                m = m_scr[i]
                m_new = jnp.maximum(m, cm_scr[x, i])
                a_scr[x, i] = jnp.exp2(m - m_new)
                m_scr[i] = m_new
                p_scr[x, i] = jnp.exp2(s_scr[x, i] - m_new.astype(s_scr.dtype)).astype(BF16)

        def stage_c(c, x, g=g):
            vt = vt_ref[g * VT_ROWS:(g + 1) * VT_ROWS, pl.ds(pl.multiple_of(c * tk, tk), tk)]
            for i in range(GROUP):
                acc_scr[i] = a_scr[x, i] * acc_scr[i] + jnp.dot(vt, p_scr[x, i], preferred_element_type=F32)

        def step(c, x, last=False):
            stage_c(c - 1, 1 - x)
            stage_b(x)
            if not last:
                stage_a(c + 1, 1 - x)

        for i in range(GROUP):
            m_scr[i] = jnp.full((1, tq), -jnp.inf, F32)
            acc_scr[i] = jnp.zeros((VT_ROWS, tq), F32)
        stage_a(0, 0)
        stage_b(0)
        stage_a(1, 1)

        def pair(j, carry):
            step(2 * j + 1, 1)
            step(2 * j + 2, 0)
            return carry

        lax.fori_loop(0, (nk - 2) // 2, pair, 0, unroll=8)
        step(nk - 1, 1, last=True)
        stage_c(nk - 1, 1)
        for i, h in enumerate(heads):
            acc = acc_scr[i]
            ot_scr[h * HEAD_DIM:(h + 1) * HEAD_DIM, :] = acc[0:HEAD_DIM] / acc[HEAD_DIM:HEAD_DIM + 1]
    o_ref[0] = ot_scr[...].T.astype(o_ref.dtype)


def _glob_attn(qb, kb, vb, tq, tk):
    b, s, _ = qb.shape
    assert s % (2 * tk) == 0, "the chunk pipeline walks key chunks in pairs"
    return pl.pallas_call(
        functools.partial(_glob_attn_kernel, tk=tk),
        grid=(b, s // tq),
        in_specs=[pl.BlockSpec((1, tq, Q_W), lambda bi, qi: (bi, qi, 0)),
                  pl.BlockSpec((1, s, KV_W), lambda bi, qi: (bi, 0, 0)),
                  pl.BlockSpec((1, s, KV_W), lambda bi, qi: (bi, 0, 0))],
        out_specs=pl.BlockSpec((1, tq, Q_W), lambda bi, qi: (bi, qi, 0)),
        out_shape=jax.ShapeDtypeStruct((b, s, Q_W), BF16),
        scratch_shapes=[pltpu.VMEM((KV_HEADS, s, HEAD_DIM), BF16), pltpu.VMEM((KV_HEADS * VT_ROWS, s), BF16),
                        pltpu.VMEM((Q_W, tq), BF16),
                        pltpu.VMEM((2, GROUP, tk, tq), SCORE_DTYPE), pltpu.VMEM((2, GROUP, tk, tq), BF16),
                        pltpu.VMEM((2, GROUP, 1, tq), F32), pltpu.VMEM((GROUP, 1, tq), F32),
                        pltpu.VMEM((2, GROUP, 1, tq), F32), pltpu.VMEM((GROUP, VT_ROWS, tq), F32),
                        pltpu.VMEM((Q_W, tq), F32)],
        compiler_params=_params(("arbitrary", "arbitrary")),
        name="glob_attn",
    )(qb, kb, vb)


def _first_argmax(vals, idx, big):
    m = jnp.max(vals, axis=0, keepdims=True)
    return m, jnp.min(jnp.where(vals == m, idx, big), axis=0, keepdims=True)


def _route(lt):
    n = lt.shape[1]
    gl = lt[0:N_GROUPS]
    idx = lax.broadcasted_iota(jnp.int32, (N_GROUPS, n), 0)
    gmax, g_sel = _first_argmax(gl, idx, N_GROUPS)
    g_w = 1.0 / jnp.sum(jnp.exp(gl - gmax), axis=0, keepdims=True)
    e_sel = jnp.zeros((EXPERTS_PER_GROUP, n), F32)
    for g in range(N_GROUPS):
        lo = N_GROUPS + g * EXPERTS_PER_GROUP
        e_sel = jnp.where(g_sel == g, lt[lo:lo + EXPERTS_PER_GROUP], e_sel)
    ex = jnp.exp(e_sel - jnp.max(e_sel, axis=0, keepdims=True))
    e_prob = ex / jnp.sum(ex, axis=0, keepdims=True)
    p1, i1 = _first_argmax(e_prob, idx, EXPERTS_PER_GROUP)
    rest = jnp.where(idx == i1, -1.0, e_prob)
    p2, i2 = _first_argmax(rest, idx, EXPERTS_PER_GROUP)
    tot = p1 + p2
    row = lax.broadcasted_iota(jnp.int32, (2 * N_GROUPS, n), 0)
    out = jnp.where(row == g_sel, 1.0, 0.0)
    out = out + jnp.where(row == N_GROUPS + i1, g_w * (p1 / tot), 0.0)
    return out + jnp.where(row == N_GROUPS + i2, g_w * (p2 / tot), 0.0)


def _merge_kernel(x_ref, ya_ref, yb_ref, ga_ref, gb_ref, woa_ref, wob_ref, wout_ref, gf_ref, wr_ref, br_ref,
                  x1_ref, route_ref, routet_ref, cnt_ref):
    tm = x_ref.shape[0]
    a = jnp.dot(ya_ref[...], woa_ref[...], preferred_element_type=F32)
    b = jnp.dot(yb_ref[...], wob_ref[...], preferred_element_type=F32)
    merged = ga_ref[...].astype(F32) * a + gb_ref[...].astype(F32) * b
    x1 = x_ref[...] + jnp.dot(merged.astype(BF16), wout_ref[...], preferred_element_type=F32)
    x1_ref[...] = x1
    t = _rmsnorm(x1, gf_ref[...])
    t_hi, t_lo = _hi_lo(t)
    big = jnp.dot(t_hi, wr_ref[...], preferred_element_type=F32)
    small = jnp.dot(t_lo, wr_ref[:, 0:LANES], preferred_element_type=F32)
    logits = big[:, 0:LANES] + big[:, LANES:2 * LANES] + small + br_ref[...]
    block = _route(logits.T)
    route_t = jnp.concatenate([block, jnp.zeros((LANES - block.shape[0], tm), F32)], axis=0)
    routet_ref[...] = route_t
    route_ref[...] = route_t.T
    cnt = jnp.sum(block, axis=1, keepdims=True)
    cnt_ref[0] = jnp.broadcast_to(cnt, cnt_ref.shape[1:]).astype(jnp.int32)


def _router_weights(w_rg, b_rg, w_re, b_re):
    d = w_rg.shape[0]
    wr = jnp.zeros((d, LANES), F32).at[:, :N_GROUPS].set(w_rg).at[:, N_GROUPS:N_GROUPS + N_EXPERTS].set(w_re)
    br = jnp.zeros((1, LANES), F32).at[0, :N_GROUPS].set(b_rg).at[0, N_GROUPS:N_GROUPS + N_EXPERTS].set(b_re)
    return jnp.concatenate(_hi_lo(wr), axis=1), br


def _merge(x2, ya, yb, ga, gb, w, tm):
    t, d = x2.shape
    row = lambda i: (i, 0)
    return pl.pallas_call(
        _merge_kernel,
        grid=(t // tm,),
        in_specs=[pl.BlockSpec((tm, d), row), pl.BlockSpec((tm, Q_W), row), pl.BlockSpec((tm, Q_W), row),
                  pl.BlockSpec((tm, d), row), pl.BlockSpec((tm, d), row),
                  _full((Q_W, d)), _full((Q_W, d)), _full((d, d)), _full((1, d)), _full((d, 2 * LANES)),
                  _full((1, LANES))],
        out_specs=[pl.BlockSpec((tm, d), row), pl.BlockSpec((tm, LANES), row),
                   pl.BlockSpec((LANES, tm), lambda i: (0, i)), pl.BlockSpec((1, 8, LANES), lambda i: (i, 0, 0))],
        out_shape=[jax.ShapeDtypeStruct((t, d), F32), jax.ShapeDtypeStruct((t, LANES), F32),
                   jax.ShapeDtypeStruct((LANES, t), F32), jax.ShapeDtypeStruct((t // tm, 8, LANES), jnp.int32)],
        compiler_params=_params(("parallel",)),
        name="merge",
    )(x2, ya, yb, ga, gb, w["w_oa"], w["w_ob"], w["w_out"], w["ffn_g"], w["w_router"], w["b_router"])


ROW_ALIGN = 32
BIG_UNITS = 8
BIG_BLOCK = BIG_UNITS * ROW_ALIGN
WHOLE_UNITS = (7, 8, 9, 10)
P_BUILD_ROWS = 128


def _rmsnorm(x, g):
    ms = jnp.mean(x * x, axis=-1, keepdims=True)
    return x * lax.rsqrt(ms + EPS) * g


def _moe_kernel(cnt_ref, x1_ref, route_ref, routet_ref, gf_ref, wg_ref, wu_ref, wd_ref, gfin_ref, o_ref,
                p_scr, q_scr, ts_scr, ws_scr):
    i = pl.program_id(0)
    g = pl.program_id(1)
    tm = x1_ref.shape[0]
    rows = p_scr.shape[0]
    offs, off = [], 0
    for gg in range(N_GROUPS):
        offs.append(off)
        off = off + (cnt_ref[i, gg] + (ROW_ALIGN - 1)) // ROW_ALIGN * ROW_ALIGN

    @pl.when(g == 0)
    def _():
        route = route_ref[...]
        sub = lax.broadcasted_iota(jnp.int32, (16, tm), 0)
        pos = lax.broadcasted_iota(jnp.int32, (16, tm), 1)
        onehot_t = jnp.where(sub < N_GROUPS, routet_ref[0:16, :], 0.0)
        count = onehot_t
        shift = 1
        while shift < tm:
            count = count + jnp.where(pos >= shift, pltpu.roll(count, shift, 1), 0.0)
            shift *= 2
        goff_t = jnp.zeros((16, tm), F32)
        for gg in range(N_GROUPS):
            goff_t = jnp.where(sub == gg, jnp.asarray(offs[gg], jnp.int32).astype(F32), goff_t)
        dest = jnp.sum((count - onehot_t + goff_t) * onehot_t, axis=0, keepdims=True)
        dest_row = dest.astype(jnp.int32)
        dest_col = jnp.broadcast_to(dest, (LANES, tm)).T[:, 0:1].astype(jnp.int32)
        q_scr[...] = jnp.where(dest_col == lax.broadcasted_iota(jnp.int32, (tm, rows), 1), 1.0, 0.0).astype(BF16)
        for r0 in range(0, rows, P_BUILD_ROWS):
            rid = lax.broadcasted_iota(jnp.int32, (P_BUILD_ROWS, tm), 0) + r0
            p_scr[r0:r0 + P_BUILD_ROWS, :] = jnp.where(dest_row == rid, 1.0, 0.0).astype(BF16)
        t = _rmsnorm(x1_ref[...], gf_ref[...]).astype(BF16)
        r_hi_lo = jnp.concatenate(_hi_lo(route), axis=1)
        gather = rows // 2
        for r0 in range(0, rows, gather):
            pb = p_scr[r0:r0 + gather, :]
            ts_scr[r0:r0 + gather, :] = jnp.dot(pb, t, preferred_element_type=F32).astype(BF16)
            w2 = jnp.dot(pb, r_hi_lo, preferred_element_type=F32)
            ws_scr[r0:r0 + gather, :] = w2[:, 0:LANES] + w2[:, LANES:2 * LANES]

    def ffn(r, m):
        tb = ts_scr[pl.ds(pl.multiple_of(r, ROW_ALIGN), m), :]
        wb = ws_scr[pl.ds(pl.multiple_of(r, ROW_ALIGN), m), :]
        y = jnp.zeros((m, o_ref.shape[1]), F32)
        for e in range(EXPERTS_PER_GROUP):
            a = jax.nn.silu(jnp.dot(tb, wg_ref[0, e], preferred_element_type=F32)) * jnp.dot(
                tb, wu_ref[0, e], preferred_element_type=F32)
            y = y + wb[:, N_GROUPS + e:N_GROUPS + e + 1] * jnp.dot(a.astype(BF16), wd_ref[0, e],
                                                                   preferred_element_type=F32)
        ts_scr[pl.ds(pl.multiple_of(r, ROW_ALIGN), m), :] = y.astype(BF16)

    start = offs[0]
    for gg in range(1, N_GROUPS):
        start = jnp.where(g == gg, offs[gg], start)
    n_small = (cnt_ref[i, g] + (ROW_ALIGN - 1)) // ROW_ALIGN

    for units in WHOLE_UNITS:
        @pl.when(n_small == units)
        def _(units=units):
            ffn(start, units * ROW_ALIGN)

    @pl.when((n_small < WHOLE_UNITS[0]) | (n_small > WHOLE_UNITS[-1]))
    def _():
        n_big = n_small // BIG_UNITS

        def big(j, carry):
            ffn(start + j * BIG_BLOCK, BIG_BLOCK)
            return carry

        lax.fori_loop(0, n_big, big, 0)

        def unit(j, carry):
            ffn(start + j * ROW_ALIGN, ROW_ALIGN)
            return carry

        lax.fori_loop(n_big * BIG_UNITS, n_small, unit, 0)

    @pl.when(g == N_GROUPS - 1)
    def _():
        x2 = x1_ref[...] + jnp.dot(q_scr[...], ts_scr[...], preferred_element_type=F32)
        o_ref[...] = _rmsnorm(x2, gfin_ref[...])


def _moe(x1, route, routet, cnt, w, tm):
    t, d = x1.shape
    de = w["w_eg"].shape[-1]
    rows = tm + N_GROUPS * ROW_ALIGN
    row = lambda i, g, c: (i, 0)
    const = lambda i, g, c: (0, 0)
    wspec = lambda a, b: pl.BlockSpec((1, EXPERTS_PER_GROUP, a, b), lambda i, g, c: (g, 0, 0, 0))
    return pl.pallas_call(
        _moe_kernel,
        grid_spec=pltpu.PrefetchScalarGridSpec(
            num_scalar_prefetch=1,
            grid=(t // tm, N_GROUPS),
            in_specs=[pl.BlockSpec((tm, d), row), pl.BlockSpec((tm, LANES), row),
                      pl.BlockSpec((LANES, tm), lambda i, g, c: (0, i)), pl.BlockSpec((1, d), const),
                      wspec(d, de), wspec(d, de), wspec(de, d), pl.BlockSpec((1, d), const)],
            out_specs=pl.BlockSpec((tm, d), row),
            scratch_shapes=[pltpu.VMEM((rows, tm), BF16), pltpu.VMEM((tm, rows), BF16), pltpu.VMEM((rows, d), BF16),
                            pltpu.VMEM((rows, LANES), F32)]),
        out_shape=jax.ShapeDtypeStruct((t, d), F32),
        compiler_params=_params(("arbitrary", "arbitrary")),
        name="moe",
    )(cnt, x1, route, routet, w["ffn_g"], w["w_eg"], w["w_eu"], w["w_ed"], w["final_g"])


ROW_TILE = 1024
WIN_TILE = 512
GLOB_TQ = 256
GLOB_TK = 256


def _prepare(attn_g, w_in, sink, gq, gk, w_oa, w_ob, w_out, ffn_g, w_rg, b_rg, w_re, b_re, w_eg, w_eu, w_ed,
             final_g):
    d = w_in.shape[0]
    w_router, b_router = _router_weights(w_rg, b_rg, w_re, b_re)
    grouped = lambda a: a.astype(BF16).reshape((N_GROUPS, EXPERTS_PER_GROUP) + a.shape[1:])
    return dict(
        attn_g=attn_g.reshape(1, d), w_in=w_in.astype(BF16), sink=sink.astype(F32),
        gq=jnp.tile(gq.astype(F32), Q_HEADS).reshape(1, Q_W), gk=jnp.tile(gk.astype(F32), KV_HEADS).reshape(1, KV_W),
        w_oa=w_oa.astype(BF16), w_ob=w_ob.astype(BF16), w_out=w_out.astype(BF16), ffn_g=ffn_g.reshape(1, d),
        w_router=w_router, b_router=b_router, w_eg=grouped(w_eg), w_eu=grouped(w_eu), w_ed=grouped(w_ed),
        final_g=final_g.reshape(1, d))


def _trunk(x, w):
    b, s, d = x.shape
    t = b * s
    assert s % ROW_TILE == 0 and s % (2 * GLOB_TK) == 0, "sequence length must be a multiple of the row tile"
    x2 = x.reshape(t, d)
    qa, ka, va, qb, kb, vb, ga, gb = _in_proj(x2, s, w, ROW_TILE)
    ya = _win_attn(qa, ka, va, w["sink"], s, WIN_TILE)
    yb = _glob_attn(qb.reshape(b, s, Q_W), kb.reshape(b, s, KV_W), vb.reshape(b, s, KV_W),
                    GLOB_TQ, GLOB_TK).reshape(t, Q_W)
    x1, route, routet, cnt = _merge(x2, ya, yb, ga, gb, w, ROW_TILE)
    y = _moe(x1, route, routet, cnt[:, :N_GROUPS, 0], w, ROW_TILE)
    return y.reshape(b, s, d)


def kernel(x_prompt, x_sample, attn_norm_g, w_in, a_sink, b_q_norm_g, b_k_norm_g, w_oa, w_ob, w_out, ffn_norm_g,
           w_router_group, b_router_group, w_router_expert, b_router_expert, w_expert_gate, w_expert_up,
           w_expert_down, final_norm_g):
    assert attn_norm_g.shape[0] == 1, "single-layer trunk"
    w = _prepare(attn_norm_g[0], w_in[0], a_sink[0], b_q_norm_g[0], b_k_norm_g[0], w_oa[0], w_ob[0], w_out[0],
                 ffn_norm_g[0], w_router_group[0], b_router_group[0], w_router_expert[0], b_router_expert[0],
                 w_expert_gate[0], w_expert_up[0], w_expert_down[0], final_norm_g)
    return (_trunk(x_prompt, w), _trunk(x_sample, w))
```

```python
import functools

import jax
import jax.numpy as jnp
import numpy as np
from jax import lax
from jax.experimental import pallas as pl
from jax.experimental.pallas import tpu as pltpu

HEAD_DIM = 64
Q_HEADS = 8
KV_HEADS = 2
GROUP = Q_HEADS // KV_HEADS
Q_W = Q_HEADS * HEAD_DIM
KV_W = KV_HEADS * HEAD_DIM
QKV_W = 2 * (Q_W + 2 * KV_W)
WINDOW = 128
BLOCK = 128
GRID_W = 64
ROPE_THETA = 10000.0
N_GROUPS = 4
EXPERTS_PER_GROUP = 4
N_EXPERTS = N_GROUPS * EXPERTS_PER_GROUP
EPS = 1e-6
NEG_INF = -1e30
SCALE = HEAD_DIM ** -0.5
LOG2E = 1.4426950408889634
ONES_ROWS = 16
VT_ROWS = HEAD_DIM + ONES_ROWS
SCORE_DTYPE = jnp.bfloat16
LANES = 128

VMEM_LIMIT = 56 * 1024 * 1024

F32 = jnp.float32
BF16 = jnp.bfloat16


def _params(sem):
    return pltpu.CompilerParams(dimension_semantics=sem, vmem_limit_bytes=VMEM_LIMIT)


def _full(shape):
    return pl.BlockSpec(shape, lambda *_: (0,) * len(shape))


def _hi_lo(v):
    top = lax.bitcast_convert_type(lax.bitcast_convert_type(v, jnp.uint32) & jnp.uint32(0xFFFF0000), F32)
    return top.astype(BF16), (v - top).astype(BF16)


def _split_dot(y, ones_bd):
    hi, lo = _hi_lo(y)
    return jnp.dot(hi, ones_bd, preferred_element_type=F32) + jnp.dot(lo, ones_bd, preferred_element_type=F32)


def _pair_swap(y):
    n = y.shape[-1]
    lane = lax.broadcasted_iota(jnp.int32, y.shape, y.ndim - 1)
    nxt = pltpu.roll(y, n - 1, y.ndim - 1)
    prv = pltpu.roll(y, 1, y.ndim - 1)
    return jnp.where((lane & 1) == 0, nxt, prv)


def _norm_rope(y, gain, ones_bd, cos, sin):
    ms = _split_dot(y * y, ones_bd) * (1.0 / HEAD_DIM)
    yn = y * lax.rsqrt(ms + EPS) * gain
    return yn * cos + _pair_swap(yn) * sin


def _in_proj_kernel(x_ref, g_ref, w_ref, cos_ref, sin_ref, gq_ref, gk_ref, oq_ref, ok_ref,
                    qa_ref, ka_ref, va_ref, qb_ref, kb_ref, vb_ref):
    h = _rmsnorm(x_ref[...], g_ref[...]).astype(BF16)

    def proj(lo, width):
        return jnp.dot(h, w_ref[:, lo:lo + width], preferred_element_type=F32)

    o = 0
    qa_ref[...] = (proj(o, Q_W) * (SCALE * LOG2E)).astype(BF16); o += Q_W
    ka_ref[...] = proj(o, KV_W).astype(BF16); o += KV_W
    va_ref[...] = proj(o, KV_W).astype(BF16); o += KV_W
    cos = cos_ref[...]
    sin = sin_ref[...]
    qb = proj(o, Q_W); o += Q_W
    cos_q = jnp.concatenate([cos] * (Q_W // LANES), axis=1)
    sin_q = jnp.concatenate([sin] * (Q_W // LANES), axis=1)
    qb_ref[...] = (_norm_rope(qb, gq_ref[...], oq_ref[...], cos_q, sin_q) * (SCALE * LOG2E)).astype(BF16)
    kb = proj(o, KV_W); o += KV_W
    kb_ref[...] = _norm_rope(kb, gk_ref[...], ok_ref[...], cos, sin).astype(BF16)
    vb_ref[...] = proj(o, KV_W).astype(BF16)


def _rope_tables(seq_len):
    rows = seq_len // GRID_W
    row = jnp.repeat(jnp.arange(rows, dtype=F32), GRID_W)
    col = jnp.tile(jnp.arange(GRID_W, dtype=F32), rows)
    half = HEAD_DIM // 2
    inv = ROPE_THETA ** (-jnp.arange(0, half, 2, dtype=F32) / half)
    ang = jnp.concatenate([row[:, None] * inv, col[:, None] * inv], axis=-1)
    cos = jnp.repeat(jnp.cos(ang), 2, axis=-1)
    sin = jnp.repeat(jnp.sin(ang), 2, axis=-1) * jnp.tile(jnp.array([-1.0, 1.0], F32), half)
    reps = LANES // HEAD_DIM
    return jnp.tile(cos, (1, reps)), jnp.tile(sin, (1, reps))


def _block_ones(width):
    idx = np.arange(width) // HEAD_DIM
    return jnp.asarray(idx[:, None] == idx[None, :], dtype=BF16)


def _in_proj(x2, seq_len, w, tm):
    t, d = x2.shape
    in_w = w["w_qkv"].shape[1]
    cos, sin = _rope_tables(seq_len)
    nseq = seq_len // tm
    row = lambda i: (i, 0)
    pos = lambda i: (i % nseq, 0)
    outs = [(Q_W, BF16), (KV_W, BF16), (KV_W, BF16), (Q_W, BF16), (KV_W, BF16), (KV_W, BF16)]
    return pl.pallas_call(
        _in_proj_kernel,
        grid=(t // tm,),
        in_specs=[pl.BlockSpec((tm, d), row), _full((1, d)), _full((d, in_w)),
                  pl.BlockSpec((tm, LANES), pos), pl.BlockSpec((tm, LANES), pos),
                  _full((1, Q_W)), _full((1, KV_W)), _full((Q_W, Q_W)), _full((KV_W, KV_W))],
        out_specs=[pl.BlockSpec((tm, w), row) for w, _ in outs],
        out_shape=[jax.ShapeDtypeStruct((t, w), dt) for w, dt in outs],
        compiler_params=_params(("parallel",)),
        name="in_proj",
    )(x2, w["attn_g"], w["w_qkv"], cos, sin, w["gq"], w["gk"], _block_ones(Q_W), _block_ones(KV_W))


PAIRS = Q_HEADS // 2
N_EDGE = 4


def _win_bias():
    slopes = np.exp2(-8.0 * np.arange(1, Q_HEADS + 1, dtype=np.float32) / Q_HEADS).astype(np.float32)
    key = np.arange(3 * BLOCK)[:, None]
    dist = np.abs(np.arange(BLOCK)[None, :] - (key - BLOCK))
    tabs = []
    for edge in range(N_EDGE):
        alive = dist <= WINDOW
        if edge & 1:
            alive = alive & (key >= BLOCK)
        if edge & 2:
            alive = alive & (key < 2 * BLOCK)
        heads = [np.where(alive, -slopes[h] * np.float32(LOG2E) * dist.astype(np.float32), np.float32(NEG_INF))
                 for h in range(Q_HEADS)]
        tabs.append(np.stack([np.concatenate([heads[2 * p], heads[2 * p + 1]], axis=1) for p in range(PAIRS)]))
    return jnp.asarray(np.stack(tabs), dtype=F32)


def _win_attn_kernel(sink_ref, q_ref, kc_ref, kp_ref, kn_ref, vc_ref, vp_ref, vn_ref, bias_ref, o_ref,
                     kh_scr, vt_scr, qp_scr, s_scr, p_scr, ot_scr, *, blocks_per_seq, sub):
    i = pl.program_id(0)
    tq = q_ref.shape[0]
    ones = jnp.ones((ONES_ROWS, BLOCK), BF16)
    for off, n, kr, vr in ((0, BLOCK, kp_ref, vp_ref), (BLOCK, tq, kc_ref, vc_ref), (BLOCK + tq, BLOCK, kn_ref, vn_ref)):
        k = kr[...]
        vt = vr[...].astype(F32).T.astype(BF16)
        for g in range(KV_HEADS):
            kh_scr[g, off:off + n, :] = k[:, g * HEAD_DIM:(g + 1) * HEAD_DIM]
            vt_scr[g * VT_ROWS:g * VT_ROWS + HEAD_DIM, off:off + n] = vt[g * HEAD_DIM:(g + 1) * HEAD_DIM, :]
            for o in range(off, off + n, BLOCK):
                vt_scr[g * VT_ROWS + HEAD_DIM:(g + 1) * VT_ROWS, o:o + BLOCK] = ones
    qt = q_ref[...].astype(F32).T.astype(BF16)
    for j in range(sub):
        for p in range(PAIRS):
            for u in range(2):
                h = 2 * p + u
                qp_scr[j, p, :, u * BLOCK:(u + 1) * BLOCK] = qt[h * HEAD_DIM:(h + 1) * HEAD_DIM,
                                                                j * BLOCK:(j + 1) * BLOCK]
    lane = lax.broadcasted_iota(jnp.int32, (1, 2 * BLOCK), 1)
    for j in range(sub):
        x = j % 2
        blk = (i * sub + j) % blocks_per_seq
        edge = (blk == 0).astype(jnp.int32) + 2 * (blk == blocks_per_seq - 1).astype(jnp.int32)
        for p in range(PAIRS):
            g = (2 * p) // GROUP
            s = jnp.dot(kh_scr[g, j * BLOCK:(j + 3) * BLOCK, :], qp_scr[j, p], preferred_element_type=F32)
            s_scr[x, p] = s + bias_ref[edge, p]
        sink_terms = []
        for p in range(PAIRS):
            logits = s_scr[x, p]
            sink = jnp.where(lane < BLOCK, sink_ref[2 * p], sink_ref[2 * p + 1]) * LOG2E
            m = jnp.maximum(jnp.max(logits, axis=0, keepdims=True), sink)
            p_scr[x, p] = jnp.exp2(logits - m).astype(BF16)
            sink_terms.append(jnp.exp2(sink - m))
        for p in range(PAIRS):
            g = (2 * p) // GROUP
            ot = jnp.dot(vt_scr[g * VT_ROWS:(g + 1) * VT_ROWS, j * BLOCK:(j + 3) * BLOCK], p_scr[x, p],
                         preferred_element_type=F32)
            out = ot[0:HEAD_DIM] / (ot[HEAD_DIM:HEAD_DIM + 1] + sink_terms[p])
            for u in range(2):
                h = 2 * p + u
                ot_scr[h * HEAD_DIM:(h + 1) * HEAD_DIM, j * BLOCK:(j + 1) * BLOCK] = out[:, u * BLOCK:(u + 1) * BLOCK]
    o_ref[...] = ot_scr[...].T.astype(o_ref.dtype)


def _win_attn(qa, ka, va, sink, seq_len, tq):
    t = qa.shape[0]
    sub = tq // BLOCK
    nblk = t // BLOCK
    win = tq + 2 * BLOCK
    row = lambda i: (i, 0)
    prev = lambda i: (jnp.maximum(i * sub - 1, 0), 0)
    nxt = lambda i: (jnp.minimum((i + 1) * sub, nblk - 1), 0)
    kv_specs = [pl.BlockSpec((tq, KV_W), row), pl.BlockSpec((BLOCK, KV_W), prev), pl.BlockSpec((BLOCK, KV_W), nxt)]
    bias_spec = pl.BlockSpec((N_EDGE, PAIRS, 3 * BLOCK, 2 * BLOCK), lambda i: (0, 0, 0, 0),
                             pipeline_mode=pl.Buffered(1))
    return pl.pallas_call(
        functools.partial(_win_attn_kernel, blocks_per_seq=seq_len // BLOCK, sub=sub),
        grid=(t // tq,),
        in_specs=[pl.BlockSpec(memory_space=pltpu.SMEM), pl.BlockSpec((tq, Q_W), row)] + kv_specs + kv_specs
                 + [bias_spec],
        out_specs=pl.BlockSpec((tq, Q_W), row),
        out_shape=jax.ShapeDtypeStruct((t, Q_W), BF16),
        scratch_shapes=[pltpu.VMEM((KV_HEADS, win, HEAD_DIM), BF16), pltpu.VMEM((KV_HEADS * VT_ROWS, win), BF16),
                        pltpu.VMEM((sub, PAIRS, HEAD_DIM, 2 * BLOCK), BF16),
                        pltpu.VMEM((2, PAIRS, 3 * BLOCK, 2 * BLOCK), F32),
                        pltpu.VMEM((2, PAIRS, 3 * BLOCK, 2 * BLOCK), BF16), pltpu.VMEM((Q_W, tq), F32)],
        compiler_params=_params(("parallel",)),
        name="win_attn",
    )(sink, qa, ka, ka, ka, va, va, va, _win_bias())


def _glob_attn_kernel(q_ref, k_ref, v_ref, o_ref, kh_ref, vt_ref, qt_scr, s_scr, p_scr, cm_scr, m_scr, a_scr, acc_scr,
                      ot_scr, *, tk):
    tq = q_ref.shape[1]
    nk = k_ref.shape[1] // tk

    @pl.when(pl.program_id(1) == 0)
    def _():
        ones = jnp.ones((ONES_ROWS, tk), BF16)

        def fill(c, carry):
            off = pl.multiple_of(c * tk, tk)
            vt = v_ref[0, pl.ds(off, tk), :].astype(F32).T.astype(BF16)
            kc = k_ref[0, pl.ds(off, tk), :]
            for g in range(KV_HEADS):
                kh_ref[g, pl.ds(off, tk), :] = kc[:, g * HEAD_DIM:(g + 1) * HEAD_DIM]
                vt_ref[g * VT_ROWS:g * VT_ROWS + HEAD_DIM, pl.ds(off, tk)] = vt[g * HEAD_DIM:(g + 1) * HEAD_DIM, :]
                vt_ref[g * VT_ROWS + HEAD_DIM:(g + 1) * VT_ROWS, pl.ds(off, tk)] = ones
            return carry

        lax.fori_loop(0, nk, fill, 0)

    qt_scr[...] = q_ref[0].astype(F32).T.astype(BF16)

    for g in range(KV_HEADS):
        heads = range(g * GROUP, (g + 1) * GROUP)

        def stage_a(c, x, heads=heads, g=g):
            kc = kh_ref[g, pl.ds(pl.multiple_of(c * tk, tk), tk), :]
            for i, h in enumerate(heads):
                s = jnp.dot(kc, qt_scr[h * HEAD_DIM:(h + 1) * HEAD_DIM, :], preferred_element_type=F32)
                s_scr[x, i] = s.astype(s_scr.dtype)
                cm_scr[x, i] = jnp.max(s, axis=0, keepdims=True)

        def stage_b(x):
            for i in range(GROUP):
                m = m_scr[i]
                m_new = jnp.maximum(m, cm_scr[x, i])
                a_scr[x, i] = jnp.exp2(m - m_new)
                m_scr[i] = m_new
                p_scr[x, i] = jnp.exp2(s_scr[x, i] - m_new.astype(s_scr.dtype)).astype(BF16)

        def stage_c(c, x, g=g):
            vt = vt_ref[g * VT_ROWS:(g + 1) * VT_ROWS, pl.ds(pl.multiple_of(c * tk, tk), tk)]
            for i in range(GROUP):
                acc_scr[i] = a_scr[x, i] * acc_scr[i] + jnp.dot(vt, p_scr[x, i], preferred_element_type=F32)

        def step(c, x, last=False):
            stage_c(c - 1, 1 - x)
            stage_b(x)
            if not last:
                stage_a(c + 1, 1 - x)

        for i in range(GROUP):
            m_scr[i] = jnp.full((1, tq), -jnp.inf, F32)
            acc_scr[i] = jnp.zeros((VT_ROWS, tq), F32)
        stage_a(0, 0)
        stage_b(0)
        stage_a(1, 1)

        def pair(j, carry):
            step(2 * j + 1, 1)
            step(2 * j + 2, 0)
            return carry

        lax.fori_loop(0, (nk - 2) // 2, pair, 0, unroll=8)
        step(nk - 1, 1, last=True)
        stage_c(nk - 1, 1)
        for i, h in enumerate(heads):
            acc = acc_scr[i]
            ot_scr[h * HEAD_DIM:(h + 1) * HEAD_DIM, :] = acc[0:HEAD_DIM] / acc[HEAD_DIM:HEAD_DIM + 1]
    o_ref[0] = ot_scr[...].T.astype(o_ref.dtype)


def _glob_attn(qb, kb, vb, tq, tk):
    b, s, _ = qb.shape
    assert s % (2 * tk) == 0, "the chunk pipeline walks key chunks in pairs"
    return pl.pallas_call(
        functools.partial(_glob_attn_kernel, tk=tk),
        grid=(b, s // tq),
        in_specs=[pl.BlockSpec((1, tq, Q_W), lambda bi, qi: (bi, qi, 0)),
                  pl.BlockSpec((1, s, KV_W), lambda bi, qi: (bi, 0, 0)),
                  pl.BlockSpec((1, s, KV_W), lambda bi, qi: (bi, 0, 0))],
        out_specs=pl.BlockSpec((1, tq, Q_W), lambda bi, qi: (bi, qi, 0)),
        out_shape=jax.ShapeDtypeStruct((b, s, Q_W), BF16),
        scratch_shapes=[pltpu.VMEM((KV_HEADS, s, HEAD_DIM), BF16), pltpu.VMEM((KV_HEADS * VT_ROWS, s), BF16),
                        pltpu.VMEM((Q_W, tq), BF16),
                        pltpu.VMEM((2, GROUP, tk, tq), SCORE_DTYPE), pltpu.VMEM((2, GROUP, tk, tq), BF16),
                        pltpu.VMEM((2, GROUP, 1, tq), F32), pltpu.VMEM((GROUP, 1, tq), F32),
                        pltpu.VMEM((2, GROUP, 1, tq), F32), pltpu.VMEM((GROUP, VT_ROWS, tq), F32),
                        pltpu.VMEM((Q_W, tq), F32)],
        compiler_params=_params(("arbitrary", "arbitrary")),
        name="glob_attn",
    )(qb, kb, vb)


def _first_argmax(vals, idx, big):
    m = jnp.max(vals, axis=0, keepdims=True)
    return m, jnp.min(jnp.where(vals == m, idx, big), axis=0, keepdims=True)


def _route(lt):
    n = lt.shape[1]
    gl = lt[0:N_GROUPS]
    idx = lax.broadcasted_iota(jnp.int32, (N_GROUPS, n), 0)
    gmax, g_sel = _first_argmax(gl, idx, N_GROUPS)
    g_w = 1.0 / jnp.sum(jnp.exp(gl - gmax), axis=0, keepdims=True)
    e_sel = jnp.zeros((EXPERTS_PER_GROUP, n), F32)
    for g in range(N_GROUPS):
        lo = N_GROUPS + g * EXPERTS_PER_GROUP
        e_sel = jnp.where(g_sel == g, lt[lo:lo + EXPERTS_PER_GROUP], e_sel)
    ex = jnp.exp(e_sel - jnp.max(e_sel, axis=0, keepdims=True))
    e_prob = ex / jnp.sum(ex, axis=0, keepdims=True)
    p1, i1 = _first_argmax(e_prob, idx, EXPERTS_PER_GROUP)
    rest = jnp.where(idx == i1, -1.0, e_prob)
    p2, i2 = _first_argmax(rest, idx, EXPERTS_PER_GROUP)
    tot = p1 + p2
    row = lax.broadcasted_iota(jnp.int32, (2 * N_GROUPS, n), 0)
    out = jnp.where(row == g_sel, 1.0, 0.0)
    out = out + jnp.where(row == N_GROUPS + i1, g_w * (p1 / tot), 0.0)
    return out + jnp.where(row == N_GROUPS + i2, g_w * (p2 / tot), 0.0)


def _merge_kernel(x_ref, ya_ref, yb_ref, ag_ref, wgate_ref, woa_ref, wob_ref, wout_ref, gf_ref, wr_ref, br_ref,
                  x1_ref, route_ref, routet_ref, cnt_ref):
    tm, d = x_ref.shape
    x = x_ref[...]
    h = _rmsnorm(x, ag_ref[...]).astype(BF16)
    a = jax.nn.sigmoid(jnp.dot(h, wgate_ref[:, 0:d], preferred_element_type=F32)) * jnp.dot(
        ya_ref[...], woa_ref[...], preferred_element_type=F32)
    b = jax.nn.sigmoid(jnp.dot(h, wgate_ref[:, d:2 * d], preferred_element_type=F32)) * jnp.dot(
        yb_ref[...], wob_ref[...], preferred_element_type=F32)
    x1 = x + jnp.dot((a + b).astype(BF16), wout_ref[...], preferred_element_type=F32)
    x1_ref[...] = x1
    t = _rmsnorm(x1, gf_ref[...])
    t_hi, t_lo = _hi_lo(t)
    big = jnp.dot(t_hi, wr_ref[...], preferred_element_type=F32)
    small = jnp.dot(t_lo, wr_ref[:, 0:LANES], preferred_element_type=F32)
    logits = big[:, 0:LANES] + big[:, LANES:2 * LANES] + small + br_ref[...]
    block = _route(logits.T)
    route_t = jnp.concatenate([block, jnp.zeros((LANES - block.shape[0], tm), F32)], axis=0)
    routet_ref[...] = route_t
    route_ref[...] = route_t.T
    cnt = jnp.sum(block, axis=1, keepdims=True)
    cnt_ref[0] = jnp.broadcast_to(cnt, cnt_ref.shape[1:]).astype(jnp.int32)


def _router_weights(w_rg, b_rg, w_re, b_re):
    d = w_rg.shape[0]
    wr = jnp.zeros((d, LANES), F32).at[:, :N_GROUPS].set(w_rg).at[:, N_GROUPS:N_GROUPS + N_EXPERTS].set(w_re)
    br = jnp.zeros((1, LANES), F32).at[0, :N_GROUPS].set(b_rg).at[0, N_GROUPS:N_GROUPS + N_EXPERTS].set(b_re)
    return jnp.concatenate(_hi_lo(wr), axis=1), br


def _merge(x2, ya, yb, w, tm):
    t, d = x2.shape
    row = lambda i: (i, 0)
    return pl.pallas_call(
        _merge_kernel,
        grid=(t // tm,),
        in_specs=[pl.BlockSpec((tm, d), row), pl.BlockSpec((tm, Q_W), row), pl.BlockSpec((tm, Q_W), row),
                  _full((1, d)), _full((d, 2 * d)),
                  _full((Q_W, d)), _full((Q_W, d)), _full((d, d)), _full((1, d)), _full((d, 2 * LANES)),
                  _full((1, LANES))],
        out_specs=[pl.BlockSpec((tm, d), row), pl.BlockSpec((tm, LANES), row),
                   pl.BlockSpec((LANES, tm), lambda i: (0, i)), pl.BlockSpec((1, 8, LANES), lambda i: (i, 0, 0))],
        out_shape=[jax.ShapeDtypeStruct((t, d), F32), jax.ShapeDtypeStruct((t, LANES), F32),
                   jax.ShapeDtypeStruct((LANES, t), F32), jax.ShapeDtypeStruct((t // tm, 8, LANES), jnp.int32)],
        compiler_params=_params(("parallel",)),
        name="merge",
    )(x2, ya, yb, w["attn_g"], w["w_gates"], w["w_oa"], w["w_ob"], w["w_out"], w["ffn_g"], w["w_router"],
      w["b_router"])


ROW_ALIGN = 32
BIG_UNITS = 8
BIG_BLOCK = BIG_UNITS * ROW_ALIGN
WHOLE_UNITS = (7, 8, 9, 10)
P_BUILD_ROWS = 128


def _rmsnorm(x, g):
    ms = jnp.mean(x * x, axis=-1, keepdims=True)
    return x * lax.rsqrt(ms + EPS) * g


def _moe_kernel(cnt_ref, x1_ref, route_ref, routet_ref, gf_ref, wg_ref, wu_ref, wd_ref, gfin_ref, o_ref,
                p_scr, q_scr, ts_scr, ws_scr):
    i = pl.program_id(0)
    g = pl.program_id(1)
    tm = x1_ref.shape[0]
    rows = p_scr.shape[0]
    offs, off = [], 0
    for gg in range(N_GROUPS):
        offs.append(off)
        off = off + (cnt_ref[i, gg] + (ROW_ALIGN - 1)) // ROW_ALIGN * ROW_ALIGN

    @pl.when(g == 0)
    def _():
        route = route_ref[...]
        sub = lax.broadcasted_iota(jnp.int32, (16, tm), 0)
        pos = lax.broadcasted_iota(jnp.int32, (16, tm), 1)
        onehot_t = jnp.where(sub < N_GROUPS, routet_ref[0:16, :], 0.0)
        count = onehot_t
        shift = 1
        while shift < tm:
            count = count + jnp.where(pos >= shift, pltpu.roll(count, shift, 1), 0.0)
            shift *= 2
        goff_t = jnp.zeros((16, tm), F32)
        for gg in range(N_GROUPS):
            goff_t = jnp.where(sub == gg, jnp.asarray(offs[gg], jnp.int32).astype(F32), goff_t)
        dest = jnp.sum((count - onehot_t + goff_t) * onehot_t, axis=0, keepdims=True)
        dest_row = dest.astype(jnp.int32)
        dest_col = jnp.broadcast_to(dest, (LANES, tm)).T[:, 0:1].astype(jnp.int32)
        q_scr[...] = jnp.where(dest_col == lax.broadcasted_iota(jnp.int32, (tm, rows), 1), 1.0, 0.0).astype(BF16)
        for r0 in range(0, rows, P_BUILD_ROWS):
            rid = lax.broadcasted_iota(jnp.int32, (P_BUILD_ROWS, tm), 0) + r0
            p_scr[r0:r0 + P_BUILD_ROWS, :] = jnp.where(dest_row == rid, 1.0, 0.0).astype(BF16)
        t = _rmsnorm(x1_ref[...], gf_ref[...]).astype(BF16)
        r_hi_lo = jnp.concatenate(_hi_lo(route), axis=1)
        gather = rows // 2
        for r0 in range(0, rows, gather):
            pb = p_scr[r0:r0 + gather, :]
            ts_scr[r0:r0 + gather, :] = jnp.dot(pb, t, preferred_element_type=F32).astype(BF16)
            w2 = jnp.dot(pb, r_hi_lo, preferred_element_type=F32)
            ws_scr[r0:r0 + gather, :] = w2[:, 0:LANES] + w2[:, LANES:2 * LANES]

    def ffn(r, m):
        tb = ts_scr[pl.ds(pl.multiple_of(r, ROW_ALIGN), m), :]
        wb = ws_scr[pl.ds(pl.multiple_of(r, ROW_ALIGN), m), :]
        y = jnp.zeros((m, o_ref.shape[1]), F32)
        for e in range(EXPERTS_PER_GROUP):
            a = jax.nn.silu(jnp.dot(tb, wg_ref[0, e], preferred_element_type=F32)) * jnp.dot(
                tb, wu_ref[0, e], preferred_element_type=F32)
            y = y + wb[:, N_GROUPS + e:N_GROUPS + e + 1] * jnp.dot(a.astype(BF16), wd_ref[0, e],
                                                                   preferred_element_type=F32)
        ts_scr[pl.ds(pl.multiple_of(r, ROW_ALIGN), m), :] = y.astype(BF16)

    start = offs[0]
    for gg in range(1, N_GROUPS):
        start = jnp.where(g == gg, offs[gg], start)
    n_small = (cnt_ref[i, g] + (ROW_ALIGN - 1)) // ROW_ALIGN

    for units in WHOLE_UNITS:
        @pl.when(n_small == units)
        def _(units=units):
            ffn(start, units * ROW_ALIGN)

    @pl.when((n_small < WHOLE_UNITS[0]) | (n_small > WHOLE_UNITS[-1]))
    def _():
        n_big = n_small // BIG_UNITS

        def big(j, carry):
            ffn(start + j * BIG_BLOCK, BIG_BLOCK)
            return carry

        lax.fori_loop(0, n_big, big, 0)

        def unit(j, carry):
            ffn(start + j * ROW_ALIGN, ROW_ALIGN)
            return carry

        lax.fori_loop(n_big * BIG_UNITS, n_small, unit, 0)

    @pl.when(g == N_GROUPS - 1)
    def _():
        x2 = x1_ref[...] + jnp.dot(q_scr[...], ts_scr[...], preferred_element_type=F32)
        o_ref[...] = _rmsnorm(x2, gfin_ref[...])


def _moe(x1, route, routet, cnt, w, tm):
    t, d = x1.shape
    de = w["w_eg"].shape[-1]
    rows = tm + N_GROUPS * ROW_ALIGN
    row = lambda i, g, c: (i, 0)
    const = lambda i, g, c: (0, 0)
    wspec = lambda a, b: pl.BlockSpec((1, EXPERTS_PER_GROUP, a, b), lambda i, g, c: (g, 0, 0, 0))
    return pl.pallas_call(
        _moe_kernel,
        grid_spec=pltpu.PrefetchScalarGridSpec(
            num_scalar_prefetch=1,
            grid=(t // tm, N_GROUPS),
            in_specs=[pl.BlockSpec((tm, d), row), pl.BlockSpec((tm, LANES), row),
                      pl.BlockSpec((LANES, tm), lambda i, g, c: (0, i)), pl.BlockSpec((1, d), const),
                      wspec(d, de), wspec(d, de), wspec(de, d), pl.BlockSpec((1, d), const)],
            out_specs=pl.BlockSpec((tm, d), row),
            scratch_shapes=[pltpu.VMEM((rows, tm), BF16), pltpu.VMEM((tm, rows), BF16), pltpu.VMEM((rows, d), BF16),
                            pltpu.VMEM((rows, LANES), F32)]),
        out_shape=jax.ShapeDtypeStruct((t, d), F32),
        compiler_params=_params(("arbitrary", "arbitrary")),
        name="moe",
    )(cnt, x1, route, routet, w["ffn_g"], w["w_eg"], w["w_eu"], w["w_ed"], w["final_g"])


ROW_TILE = 1024
WIN_TILE = 512
GLOB_TQ = 256
GLOB_TK = 256


def _prepare(attn_g, w_in, sink, gq, gk, w_oa, w_ob, w_out, ffn_g, w_rg, b_rg, w_re, b_re, w_eg, w_eu, w_ed,
             final_g):
    d = w_in.shape[0]
    w_router, b_router = _router_weights(w_rg, b_rg, w_re, b_re)
    grouped = lambda a: a.astype(BF16).reshape((N_GROUPS, EXPERTS_PER_GROUP) + a.shape[1:])
    return dict(
        attn_g=attn_g.reshape(1, d), w_qkv=w_in[:, :QKV_W].astype(BF16), w_gates=w_in[:, QKV_W:].astype(BF16),
        sink=sink.astype(F32),
        gq=jnp.tile(gq.astype(F32), Q_HEADS).reshape(1, Q_W), gk=jnp.tile(gk.astype(F32), KV_HEADS).reshape(1, KV_W),
        w_oa=w_oa.astype(BF16), w_ob=w_ob.astype(BF16), w_out=w_out.astype(BF16), ffn_g=ffn_g.reshape(1, d),
        w_router=w_router, b_router=b_router, w_eg=grouped(w_eg), w_eu=grouped(w_eu), w_ed=grouped(w_ed),
        final_g=final_g.reshape(1, d))


def _trunk(x, w):
    b, s, d = x.shape
    t = b * s
    assert s % ROW_TILE == 0 and s % (2 * GLOB_TK) == 0, "sequence length must be a multiple of the row tile"
    x2 = x.reshape(t, d)
    qa, ka, va, qb, kb, vb = _in_proj(x2, s, w, ROW_TILE)
    ya = _win_attn(qa, ka, va, w["sink"], s, WIN_TILE)
    yb = _glob_attn(qb.reshape(b, s, Q_W), kb.reshape(b, s, KV_W), vb.reshape(b, s, KV_W),
                    GLOB_TQ, GLOB_TK).reshape(t, Q_W)
    x1, route, routet, cnt = _merge(x2, ya, yb, w, ROW_TILE)
    y = _moe(x1, route, routet, cnt[:, :N_GROUPS, 0], w, ROW_TILE)
    return y.reshape(b, s, d)


def kernel(x_prompt, x_sample, attn_norm_g, w_in, a_sink, b_q_norm_g, b_k_norm_g, w_oa, w_ob, w_out, ffn_norm_g,
           w_router_group, b_router_group, w_router_expert, b_router_expert, w_expert_gate, w_expert_up,
           w_expert_down, final_norm_g):
    assert attn_norm_g.shape[0] == 1, "single-layer trunk"
    w = _prepare(attn_norm_g[0], w_in[0], a_sink[0], b_q_norm_g[0], b_k_norm_g[0], w_oa[0], w_ob[0], w_out[0],
                 ffn_norm_g[0], w_router_group[0], b_router_group[0], w_router_expert[0], b_router_expert[0],
                 w_expert_gate[0], w_expert_up[0], w_expert_down[0], final_norm_g)
    return (_trunk(x_prompt, w), _trunk(x_sample, w))
```

```python
import functools

import jax
import jax.numpy as jnp
import numpy as np
from jax import lax
from jax.experimental import pallas as pl
from jax.experimental.pallas import tpu as pltpu

HEAD_DIM = 64
Q_HEADS = 8
KV_HEADS = 2
GROUP = Q_HEADS // KV_HEADS
Q_W = Q_HEADS * HEAD_DIM
KV_W = KV_HEADS * HEAD_DIM
WINDOW = 128
BLOCK = 128
GRID_W = 64
ROPE_THETA = 10000.0
N_GROUPS = 4
EXPERTS_PER_GROUP = 4
N_EXPERTS = N_GROUPS * EXPERTS_PER_GROUP
EPS = 1e-6
NEG_INF = -1e30
SCALE = HEAD_DIM ** -0.5
LOG2E = 1.4426950408889634
ONES_ROWS = 16
VT_ROWS = HEAD_DIM + ONES_ROWS
SCORE_DTYPE = jnp.bfloat16
LANES = 128

VMEM_LIMIT = 56 * 1024 * 1024

F32 = jnp.float32
BF16 = jnp.bfloat16


def _params(sem):
    return pltpu.CompilerParams(dimension_semantics=sem, vmem_limit_bytes=VMEM_LIMIT)


def _full(shape):
    return pl.BlockSpec(shape, lambda *_: (0,) * len(shape))


def _hi_lo(v):
    top = lax.bitcast_convert_type(lax.bitcast_convert_type(v, jnp.uint32) & jnp.uint32(0xFFFF0000), F32)
    return top.astype(BF16), (v - top).astype(BF16)


def _split_dot(y, ones_bd):
    hi, lo = _hi_lo(y)
    return jnp.dot(hi, ones_bd, preferred_element_type=F32) + jnp.dot(lo, ones_bd, preferred_element_type=F32)


def _pair_swap(y):
    n = y.shape[-1]
    lane = lax.broadcasted_iota(jnp.int32, y.shape, y.ndim - 1)
    nxt = pltpu.roll(y, n - 1, y.ndim - 1)
    prv = pltpu.roll(y, 1, y.ndim - 1)
    return jnp.where((lane & 1) == 0, nxt, prv)


def _norm_rope(y, gain, ones_bd, cos, sin):
    ms = _split_dot(y * y, ones_bd) * (1.0 / HEAD_DIM)
    yn = y * lax.rsqrt(ms + EPS) * gain
    return yn * cos + _pair_swap(yn) * sin


def _in_proj_kernel(x_ref, g_ref, w_ref, cos_ref, sin_ref, gq_ref, gk_ref, oq_ref, ok_ref,
                    qa_ref, ka_ref, va_ref, qb_ref, kb_ref, vb_ref, ga_ref, gb_ref):
    x = x_ref[...]
    d = x.shape[1]
    ms = jnp.mean(x * x, axis=-1, keepdims=True)
    h = (x * lax.rsqrt(ms + EPS) * g_ref[...]).astype(BF16)

    def proj(lo, width):
        return jnp.dot(h, w_ref[:, lo:lo + width], preferred_element_type=F32)

    o = 0
    qa_ref[...] = (proj(o, Q_W) * (SCALE * LOG2E)).astype(BF16); o += Q_W
    ka_ref[...] = proj(o, KV_W).astype(BF16); o += KV_W
    va_ref[...] = proj(o, KV_W).astype(BF16); o += KV_W
    cos = cos_ref[...]
    sin = sin_ref[...]
    qb = proj(o, Q_W); o += Q_W
    cos_q = jnp.concatenate([cos] * (Q_W // LANES), axis=1)
    sin_q = jnp.concatenate([sin] * (Q_W // LANES), axis=1)
    qb_ref[...] = (_norm_rope(qb, gq_ref[...], oq_ref[...], cos_q, sin_q) * (SCALE * LOG2E)).astype(BF16)
    kb = proj(o, KV_W); o += KV_W
    kb_ref[...] = _norm_rope(kb, gk_ref[...], ok_ref[...], cos, sin).astype(BF16)
    vb_ref[...] = proj(o, KV_W).astype(BF16); o += KV_W
    ga_ref[...] = jax.nn.sigmoid(proj(o, d)).astype(BF16); o += d
    gb_ref[...] = jax.nn.sigmoid(proj(o, d)).astype(BF16)


def _rope_tables(seq_len):
    rows = seq_len // GRID_W
    row = jnp.repeat(jnp.arange(rows, dtype=F32), GRID_W)
    col = jnp.tile(jnp.arange(GRID_W, dtype=F32), rows)
    half = HEAD_DIM // 2
    inv = ROPE_THETA ** (-jnp.arange(0, half, 2, dtype=F32) / half)
    ang = jnp.concatenate([row[:, None] * inv, col[:, None] * inv], axis=-1)
    cos = jnp.repeat(jnp.cos(ang), 2, axis=-1)
    sin = jnp.repeat(jnp.sin(ang), 2, axis=-1) * jnp.tile(jnp.array([-1.0, 1.0], F32), half)
    reps = LANES // HEAD_DIM
    return jnp.tile(cos, (1, reps)), jnp.tile(sin, (1, reps))


def _block_ones(width):
    idx = np.arange(width) // HEAD_DIM
    return jnp.asarray(idx[:, None] == idx[None, :], dtype=BF16)


def _in_proj(x2, seq_len, w, tm):
    t, d = x2.shape
    in_w = w["w_in"].shape[1]
    cos, sin = _rope_tables(seq_len)
    nseq = seq_len // tm
    row = lambda i: (i, 0)
    pos = lambda i: (i % nseq, 0)
    outs = [(Q_W, BF16), (KV_W, BF16), (KV_W, BF16), (Q_W, BF16), (KV_W, BF16), (KV_W, BF16), (d, BF16), (d, BF16)]
    return pl.pallas_call(
        _in_proj_kernel,
        grid=(t // tm,),
        in_specs=[pl.BlockSpec((tm, d), row), _full((1, d)), _full((d, in_w)),
                  pl.BlockSpec((tm, LANES), pos), pl.BlockSpec((tm, LANES), pos),
                  _full((1, Q_W)), _full((1, KV_W)), _full((Q_W, Q_W)), _full((KV_W, KV_W))],
        out_specs=[pl.BlockSpec((tm, w), row) for w, _ in outs],
        out_shape=[jax.ShapeDtypeStruct((t, w), dt) for w, dt in outs],
        compiler_params=_params(("parallel",)),
        name="in_proj",
    )(x2, w["attn_g"], w["w_in"], cos, sin, w["gq"], w["gk"], _block_ones(Q_W), _block_ones(KV_W))


PAIRS = Q_HEADS // 2
N_EDGE = 4


def _win_bias():
    slopes = np.exp2(-8.0 * np.arange(1, Q_HEADS + 1, dtype=np.float32) / Q_HEADS).astype(np.float32)
    key = np.arange(3 * BLOCK)[:, None]
    dist = np.abs(np.arange(BLOCK)[None, :] - (key - BLOCK))
    tabs = []
    for edge in range(N_EDGE):
        alive = dist <= WINDOW
        if edge & 1:
            alive = alive & (key >= BLOCK)
        if edge & 2:
            alive = alive & (key < 2 * BLOCK)
        heads = [np.where(alive, -slopes[h] * np.float32(LOG2E) * dist.astype(np.float32), np.float32(NEG_INF))
                 for h in range(Q_HEADS)]
        tabs.append(np.stack([np.concatenate([heads[2 * p], heads[2 * p + 1]], axis=1) for p in range(PAIRS)]))
    return jnp.asarray(np.stack(tabs), dtype=F32)


def _win_attn_kernel(sink_ref, q_ref, kc_ref, kp_ref, kn_ref, vc_ref, vp_ref, vn_ref, bias_ref, o_ref,
                     kh_scr, vt_scr, qp_scr, s_scr, p_scr, ot_scr, *, blocks_per_seq, sub):
    i = pl.program_id(0)
    tq = q_ref.shape[0]
    ones = jnp.ones((ONES_ROWS, BLOCK), BF16)
    for off, n, kr, vr in ((0, BLOCK, kp_ref, vp_ref), (BLOCK, tq, kc_ref, vc_ref), (BLOCK + tq, BLOCK, kn_ref, vn_ref)):
        k = kr[...]
        vt = vr[...].astype(F32).T.astype(BF16)
        for g in range(KV_HEADS):
            kh_scr[g, off:off + n, :] = k[:, g * HEAD_DIM:(g + 1) * HEAD_DIM]
            vt_scr[g * VT_ROWS:g * VT_ROWS + HEAD_DIM, off:off + n] = vt[g * HEAD_DIM:(g + 1) * HEAD_DIM, :]
            for o in range(off, off + n, BLOCK):
                vt_scr[g * VT_ROWS + HEAD_DIM:(g + 1) * VT_ROWS, o:o + BLOCK] = ones
    qt = q_ref[...].astype(F32).T.astype(BF16)
    for j in range(sub):
        for p in range(PAIRS):
            for u in range(2):
                h = 2 * p + u
                qp_scr[j, p, :, u * BLOCK:(u + 1) * BLOCK] = qt[h * HEAD_DIM:(h + 1) * HEAD_DIM,
                                                                j * BLOCK:(j + 1) * BLOCK]
    lane = lax.broadcasted_iota(jnp.int32, (1, 2 * BLOCK), 1)
    for j in range(sub):
        x = j % 2
        blk = (i * sub + j) % blocks_per_seq
        edge = (blk == 0).astype(jnp.int32) + 2 * (blk == blocks_per_seq - 1).astype(jnp.int32)
        for p in range(PAIRS):
            g = (2 * p) // GROUP
            s = jnp.dot(kh_scr[g, j * BLOCK:(j + 3) * BLOCK, :], qp_scr[j, p], preferred_element_type=F32)
            s_scr[x, p] = s + bias_ref[edge, p]
        sink_terms = []
        for p in range(PAIRS):
            logits = s_scr[x, p]
            sink = jnp.where(lane < BLOCK, sink_ref[2 * p], sink_ref[2 * p + 1]) * LOG2E
            m = jnp.maximum(jnp.max(logits, axis=0, keepdims=True), sink)
            p_scr[x, p] = jnp.exp2(logits - m).astype(BF16)
            sink_terms.append(jnp.exp2(sink - m))
        for p in range(PAIRS):
            g = (2 * p) // GROUP
            ot = jnp.dot(vt_scr[g * VT_ROWS:(g + 1) * VT_ROWS, j * BLOCK:(j + 3) * BLOCK], p_scr[x, p],
                         preferred_element_type=F32)
            out = ot[0:HEAD_DIM] / (ot[HEAD_DIM:HEAD_DIM + 1] + sink_terms[p])
            for u in range(2):
                h = 2 * p + u
                ot_scr[h * HEAD_DIM:(h + 1) * HEAD_DIM, j * BLOCK:(j + 1) * BLOCK] = out[:, u * BLOCK:(u + 1) * BLOCK]
    o_ref[...] = ot_scr[...].T.astype(o_ref.dtype)


def _win_attn(qa, ka, va, sink, seq_len, tq):
    t = qa.shape[0]
    sub = tq // BLOCK
    nblk = t // BLOCK
    win = tq + 2 * BLOCK
    row = lambda i: (i, 0)
    prev = lambda i: (jnp.maximum(i * sub - 1, 0), 0)
    nxt = lambda i: (jnp.minimum((i + 1) * sub, nblk - 1), 0)
    kv_specs = [pl.BlockSpec((tq, KV_W), row), pl.BlockSpec((BLOCK, KV_W), prev), pl.BlockSpec((BLOCK, KV_W), nxt)]
    bias_spec = pl.BlockSpec((N_EDGE, PAIRS, 3 * BLOCK, 2 * BLOCK), lambda i: (0, 0, 0, 0),
                             pipeline_mode=pl.Buffered(1))
    return pl.pallas_call(
        functools.partial(_win_attn_kernel, blocks_per_seq=seq_len // BLOCK, sub=sub),
        grid=(t // tq,),
        in_specs=[pl.BlockSpec(memory_space=pltpu.SMEM), pl.BlockSpec((tq, Q_W), row)] + kv_specs + kv_specs
                 + [bias_spec],
        out_specs=pl.BlockSpec((tq, Q_W), row),
        out_shape=jax.ShapeDtypeStruct((t, Q_W), BF16),
        scratch_shapes=[pltpu.VMEM((KV_HEADS, win, HEAD_DIM), BF16), pltpu.VMEM((KV_HEADS * VT_ROWS, win), BF16),
                        pltpu.VMEM((sub, PAIRS, HEAD_DIM, 2 * BLOCK), BF16),
                        pltpu.VMEM((2, PAIRS, 3 * BLOCK, 2 * BLOCK), F32),
                        pltpu.VMEM((2, PAIRS, 3 * BLOCK, 2 * BLOCK), BF16), pltpu.VMEM((Q_W, tq), F32)],
        compiler_params=_params(("parallel",)),
        name="win_attn",
    )(sink, qa, ka, ka, ka, va, va, va, _win_bias())


def _glob_attn_kernel(q_ref, k_ref, v_ref, o_ref, kh_ref, vt_ref, qt_scr, s_scr, p_scr, cm_scr, m_scr, a_scr, acc_scr,
                      ot_scr, *, tk):
    tq = q_ref.shape[1]
    nk = k_ref.shape[1] // tk

    @pl.when(pl.program_id(1) == 0)
    def _():
        ones = jnp.ones((ONES_ROWS, tk), BF16)

        def fill(c, carry):
            off = pl.multiple_of(c * tk, tk)
            vt = v_ref[0, pl.ds(off, tk), :].astype(F32).T.astype(BF16)
            kc = k_ref[0, pl.ds(off, tk), :]
            for g in range(KV_HEADS):
                kh_ref[g, pl.ds(off, tk), :] = kc[:, g * HEAD_DIM:(g + 1) * HEAD_DIM]
                vt_ref[g * VT_ROWS:g * VT_ROWS + HEAD_DIM, pl.ds(off, tk)] = vt[g * HEAD_DIM:(g + 1) * HEAD_DIM, :]
                vt_ref[g * VT_ROWS + HEAD_DIM:(g + 1) * VT_ROWS, pl.ds(off, tk)] = ones
            return carry

        lax.fori_loop(0, nk, fill, 0)

    qt_scr[...] = q_ref[0].astype(F32).T.astype(BF16)

    for g in range(KV_HEADS):
        heads = range(g * GROUP, (g + 1) * GROUP)

        def stage_a(c, x, heads=heads, g=g):
            kc = kh_ref[g, pl.ds(pl.multiple_of(c * tk, tk), tk), :]
            for i, h in enumerate(heads):
                s = jnp.dot(kc, qt_scr[h * HEAD_DIM:(h + 1) * HEAD_DIM, :], preferred_element_type=F32)
                s_scr[x, i] = s.astype(s_scr.dtype)
                cm_scr[x, i] = jnp.max(s, axis=0, keepdims=True)

        def stage_b(x):
            for i in range(GROUP):
                m = m_scr[i]
                m_new = jnp.maximum(m, cm_scr[x, i])
                a_scr[x, i] = jnp.exp2(m - m_new)
                m_scr[i] = m_new
                p_scr[x, i] = jnp.exp2(s_scr[x, i] - m_new.astype(s_scr.dtype)).astype(BF16)

        def stage_c(c, x, g=g):
            vt = vt_ref[g * VT_ROWS:(g + 1) * VT_ROWS, pl.ds(pl.multiple_of(c * tk, tk), tk)]
            for i in range(GROUP):
                acc_scr[i] = a_scr[x, i] * acc_scr[i] + jnp.dot(vt, p_scr[x, i], preferred_element_type=F32)

        def step(c, x, last=False):
            stage_c(c - 1, 1 - x)
            stage_b(x)
            if not last:
                stage_a(c + 1, 1 - x)

        for i in range(GROUP):
            m_scr[i] = jnp.full((1, tq), -jnp.inf, F32)
            acc_scr[i] = jnp.zeros((VT_ROWS, tq), F32)
        stage_a(0, 0)
        stage_b(0)
        stage_a(1, 1)

        def pair(j, carry):
            step(2 * j + 1, 1)
            step(2 * j + 2, 0)
            return carry

        lax.fori_loop(0, (nk - 2) // 2, pair, 0, unroll=8)
        step(nk - 1, 1, last=True)
        stage_c(nk - 1, 1)
        for i, h in enumerate(heads):
            acc = acc_scr[i]
            ot_scr[h * HEAD_DIM:(h + 1) * HEAD_DIM, :] = acc[0:HEAD_DIM] / acc[HEAD_DIM:HEAD_DIM + 1]
    o_ref[0] = ot_scr[...].T.astype(o_ref.dtype)


def _glob_attn(qb, kb, vb, tq, tk):
    b, s, _ = qb.shape
    assert s % (2 * tk) == 0, "the chunk pipeline walks key chunks in pairs"
    return pl.pallas_call(
        functools.partial(_glob_attn_kernel, tk=tk),
        grid=(b, s // tq),
        in_specs=[pl.BlockSpec((1, tq, Q_W), lambda bi, qi: (bi, qi, 0)),
                  pl.BlockSpec((1, s, KV_W), lambda bi, qi: (bi, 0, 0)),
                  pl.BlockSpec((1, s, KV_W), lambda bi, qi: (bi, 0, 0))],
        out_specs=pl.BlockSpec((1, tq, Q_W), lambda bi, qi: (bi, qi, 0)),
        out_shape=jax.ShapeDtypeStruct((b, s, Q_W), BF16),
        scratch_shapes=[pltpu.VMEM((KV_HEADS, s, HEAD_DIM), BF16), pltpu.VMEM((KV_HEADS * VT_ROWS, s), BF16),
                        pltpu.VMEM((Q_W, tq), BF16),
                        pltpu.VMEM((2, GROUP, tk, tq), SCORE_DTYPE), pltpu.VMEM((2, GROUP, tk, tq), BF16),
                        pltpu.VMEM((2, GROUP, 1, tq), F32), pltpu.VMEM((GROUP, 1, tq), F32),
                        pltpu.VMEM((2, GROUP, 1, tq), F32), pltpu.VMEM((GROUP, VT_ROWS, tq), F32),
                        pltpu.VMEM((Q_W, tq), F32)],
        compiler_params=_params(("arbitrary", "arbitrary")),
        name="glob_attn",
    )(qb, kb, vb)


def _first_argmax(vals, idx, big):
    m = jnp.max(vals, axis=0, keepdims=True)
    return m, jnp.min(jnp.where(vals == m, idx, big), axis=0, keepdims=True)


def _route(lt):
    n = lt.shape[1]
    gl = lt[0:N_GROUPS]
    idx = lax.broadcasted_iota(jnp.int32, (N_GROUPS, n), 0)
    gmax, g_sel = _first_argmax(gl, idx, N_GROUPS)
    g_w = 1.0 / jnp.sum(jnp.exp(gl - gmax), axis=0, keepdims=True)
    e_sel = jnp.zeros((EXPERTS_PER_GROUP, n), F32)
    for g in range(N_GROUPS):
        lo = N_GROUPS + g * EXPERTS_PER_GROUP
        e_sel = jnp.where(g_sel == g, lt[lo:lo + EXPERTS_PER_GROUP], e_sel)
    ex = jnp.exp(e_sel - jnp.max(e_sel, axis=0, keepdims=True))
    e_prob = ex / jnp.sum(ex, axis=0, keepdims=True)
    p1, i1 = _first_argmax(e_prob, idx, EXPERTS_PER_GROUP)
    rest = jnp.where(idx == i1, -1.0, e_prob)
    p2, i2 = _first_argmax(rest, idx, EXPERTS_PER_GROUP)
    tot = p1 + p2
    row = lax.broadcasted_iota(jnp.int32, (2 * N_GROUPS, n), 0)
    out = jnp.where(row == g_sel, 1.0, 0.0)
    out = out + jnp.where(row == N_GROUPS + i1, g_w * (p1 / tot), 0.0)
    return out + jnp.where(row == N_GROUPS + i2, g_w * (p2 / tot), 0.0)


def _merge_kernel(x_ref, ya_ref, yb_ref, ga_ref, gb_ref, woa_ref, wob_ref, wout_ref, gf_ref, wr_ref, br_ref,
                  x1_ref, route_ref, routet_ref, cnt_ref):
    tm = x_ref.shape[0]
    a = jnp.dot(ya_ref[...], woa_ref[...], preferred_element_type=F32)
    b = jnp.dot(yb_ref[...], wob_ref[...], preferred_element_type=F32)
    merged = ga_ref[...].astype(F32) * a + gb_ref[...].astype(F32) * b
    x1 = x_ref[...] + jnp.dot(merged.astype(BF16), wout_ref[...], preferred_element_type=F32)
    x1_ref[...] = x1
    t = _rmsnorm(x1, gf_ref[...])
    t_hi, t_lo = _hi_lo(t)
    big = jnp.dot(t_hi, wr_ref[...], preferred_element_type=F32)
    small = jnp.dot(t_lo, wr_ref[:, 0:LANES], preferred_element_type=F32)
    logits = big[:, 0:LANES] + big[:, LANES:2 * LANES] + small + br_ref[...]
    block = _route(logits.T)
    route_t = jnp.concatenate([block, jnp.zeros((LANES - block.shape[0], tm), F32)], axis=0)
    routet_ref[...] = route_t
    route_ref[...] = route_t.T
    cnt = jnp.sum(block, axis=1, keepdims=True)
    cnt_ref[0] = jnp.broadcast_to(cnt, cnt_ref.shape[1:]).astype(jnp.int32)


def _router_weights(w_rg, b_rg, w_re, b_re):
    d = w_rg.shape[0]
    wr = jnp.zeros((d, LANES), F32).at[:, :N_GROUPS].set(w_rg).at[:, N_GROUPS:N_GROUPS + N_EXPERTS].set(w_re)
    br = jnp.zeros((1, LANES), F32).at[0, :N_GROUPS].set(b_rg).at[0, N_GROUPS:N_GROUPS + N_EXPERTS].set(b_re)
    return jnp.concatenate(_hi_lo(wr), axis=1), br


def _merge(x2, ya, yb, ga, gb, w, tm):
    t, d = x2.shape
    row = lambda i: (i, 0)
    return pl.pallas_call(
        _merge_kernel,
        grid=(t // tm,),
        in_specs=[pl.BlockSpec((tm, d), row), pl.BlockSpec((tm, Q_W), row), pl.BlockSpec((tm, Q_W), row),
                  pl.BlockSpec((tm, d), row), pl.BlockSpec((tm, d), row),
                  _full((Q_W, d)), _full((Q_W, d)), _full((d, d)), _full((1, d)), _full((d, 2 * LANES)),
                  _full((1, LANES))],
        out_specs=[pl.BlockSpec((tm, d), row), pl.BlockSpec((tm, LANES), row),
                   pl.BlockSpec((LANES, tm), lambda i: (0, i)), pl.BlockSpec((1, 8, LANES), lambda i: (i, 0, 0))],
        out_shape=[jax.ShapeDtypeStruct((t, d), F32), jax.ShapeDtypeStruct((t, LANES), F32),
                   jax.ShapeDtypeStruct((LANES, t), F32), jax.ShapeDtypeStruct((t // tm, 8, LANES), jnp.int32)],
        compiler_params=_params(("parallel",)),
        name="merge",
    )(x2, ya, yb, ga, gb, w["w_oa"], w["w_ob"], w["w_out"], w["ffn_g"], w["w_router"], w["b_router"])


ROW_ALIGN = 32
BIG_UNITS = 8
BIG_BLOCK = BIG_UNITS * ROW_ALIGN
WHOLE_UNITS = (7, 8, 9, 10)
P_BUILD_ROWS = 128


def _rmsnorm(x, g):
    ms = jnp.mean(x * x, axis=-1, keepdims=True)
    return x * lax.rsqrt(ms + EPS) * g


def _moe_kernel(cnt_ref, x1_ref, route_ref, routet_ref, gf_ref, wg_ref, wu_ref, wd_ref, gfin_ref, o_ref,
                p_scr, q_scr, ts_scr, ws_scr):
    i = pl.program_id(0)
    g = pl.program_id(1)
    tm = x1_ref.shape[0]
    rows = p_scr.shape[0]
    offs, off = [], 0
    for gg in range(N_GROUPS):
        offs.append(off)
        off = off + (cnt_ref[i, gg] + (ROW_ALIGN - 1)) // ROW_ALIGN * ROW_ALIGN

    @pl.when(g == 0)
    def _():
        route = route_ref[...]
        sub = lax.broadcasted_iota(jnp.int32, (16, tm), 0)
        pos = lax.broadcasted_iota(jnp.int32, (16, tm), 1)
        onehot_t = jnp.where(sub < N_GROUPS, routet_ref[0:16, :], 0.0)
        count = onehot_t
        shift = 1
        while shift < tm:
            count = count + jnp.where(pos >= shift, pltpu.roll(count, shift, 1), 0.0)
            shift *= 2
        goff_t = jnp.zeros((16, tm), F32)
        for gg in range(N_GROUPS):
            goff_t = jnp.where(sub == gg, jnp.asarray(offs[gg], jnp.int32).astype(F32), goff_t)
        dest = jnp.sum((count - onehot_t + goff_t) * onehot_t, axis=0, keepdims=True)
        dest_row = dest.astype(jnp.int32)
        dest_col = jnp.broadcast_to(dest, (LANES, tm)).T[:, 0:1].astype(jnp.int32)
        q_scr[...] = jnp.where(dest_col == lax.broadcasted_iota(jnp.int32, (tm, rows), 1), 1.0, 0.0).astype(BF16)
        for r0 in range(0, rows, P_BUILD_ROWS):
            rid = lax.broadcasted_iota(jnp.int32, (P_BUILD_ROWS, tm), 0) + r0
            p_scr[r0:r0 + P_BUILD_ROWS, :] = jnp.where(dest_row == rid, 1.0, 0.0).astype(BF16)
        t = _rmsnorm(x1_ref[...], gf_ref[...]).astype(BF16)
        r_hi_lo = jnp.concatenate(_hi_lo(route), axis=1)
        gather = rows // 2
        for r0 in range(0, rows, gather):
            pb = p_scr[r0:r0 + gather, :]
            ts_scr[r0:r0 + gather, :] = jnp.dot(pb, t, preferred_element_type=F32).astype(BF16)
            w2 = jnp.dot(pb, r_hi_lo, preferred_element_type=F32)
            ws_scr[r0:r0 + gather, :] = w2[:, 0:LANES] + w2[:, LANES:2 * LANES]

    def ffn(r, m):
        tb = ts_scr[pl.ds(pl.multiple_of(r, ROW_ALIGN), m), :]
        wb = ws_scr[pl.ds(pl.multiple_of(r, ROW_ALIGN), m), :]
        y = jnp.zeros((m, o_ref.shape[1]), F32)
        for e in range(EXPERTS_PER_GROUP):
            a = jax.nn.silu(jnp.dot(tb, wg_ref[0, e], preferred_element_type=F32)) * jnp.dot(
                tb, wu_ref[0, e], preferred_element_type=F32)
            y = y + wb[:, N_GROUPS + e:N_GROUPS + e + 1] * jnp.dot(a.astype(BF16), wd_ref[0, e],
                                                                   preferred_element_type=F32)
        ts_scr[pl.ds(pl.multiple_of(r, ROW_ALIGN), m), :] = y.astype(BF16)

    start = offs[0]
    for gg in range(1, N_GROUPS):
        start = jnp.where(g == gg, offs[gg], start)
    n_small = (cnt_ref[i, g] + (ROW_ALIGN - 1)) // ROW_ALIGN

    for units in WHOLE_UNITS:
        @pl.when(n_small == units)
        def _(units=units):
            ffn(start, units * ROW_ALIGN)

    @pl.when((n_small < WHOLE_UNITS[0]) | (n_small > WHOLE_UNITS[-1]))
    def _():
        n_big = n_small // BIG_UNITS

        def big(j, carry):
            ffn(start + j * BIG_BLOCK, BIG_BLOCK)
            return carry

        lax.fori_loop(0, n_big, big, 0)

        def unit(j, carry):
            ffn(start + j * ROW_ALIGN, ROW_ALIGN)
            return carry

        lax.fori_loop(n_big * BIG_UNITS, n_small, unit, 0)

    @pl.when(g == N_GROUPS - 1)
    def _():
        x2 = x1_ref[...] + jnp.dot(q_scr[...], ts_scr[...], preferred_element_type=F32)
        o_ref[...] = _rmsnorm(x2, gfin_ref[...])


def _moe(x1, route, routet, cnt, w, tm):
    t, d = x1.shape
    de = w["w_eg"].shape[-1]
    rows = tm + N_GROUPS * ROW_ALIGN
    row = lambda i, g, c: (i, 0)
    const = lambda i, g, c: (0, 0)
    wspec = lambda a, b: pl.BlockSpec((1, EXPERTS_PER_GROUP, a, b), lambda i, g, c: (g, 0, 0, 0))
    return pl.pallas_call(
        _moe_kernel,
        grid_spec=pltpu.PrefetchScalarGridSpec(
            num_scalar_prefetch=1,
            grid=(t // tm, N_GROUPS),
            in_specs=[pl.BlockSpec((tm, d), row), pl.BlockSpec((tm, LANES), row),
                      pl.BlockSpec((LANES, tm), lambda i, g, c: (0, i)), pl.BlockSpec((1, d), const),
                      wspec(d, de), wspec(d, de), wspec(de, d), pl.BlockSpec((1, d), const)],
            out_specs=pl.BlockSpec((tm, d), row),
            scratch_shapes=[pltpu.VMEM((rows, tm), BF16), pltpu.VMEM((tm, rows), BF16), pltpu.VMEM((rows, d), BF16),
                            pltpu.VMEM((rows, LANES), F32)]),
        out_shape=jax.ShapeDtypeStruct((t, d), F32),
        compiler_params=_params(("arbitrary", "arbitrary")),
        name="moe",
    )(cnt, x1, route, routet, w["ffn_g"], w["w_eg"], w["w_eu"], w["w_ed"], w["final_g"])


ROW_TILE = 1024
WIN_TILE = 1024
GLOB_TQ = 256
GLOB_TK = 256


def _prepare(attn_g, w_in, sink, gq, gk, w_oa, w_ob, w_out, ffn_g, w_rg, b_rg, w_re, b_re, w_eg, w_eu, w_ed,
             final_g):
    d = w_in.shape[0]
    w_router, b_router = _router_weights(w_rg, b_rg, w_re, b_re)
    grouped = lambda a: a.astype(BF16).reshape((N_GROUPS, EXPERTS_PER_GROUP) + a.shape[1:])
    return dict(
        attn_g=attn_g.reshape(1, d), w_in=w_in.astype(BF16), sink=sink.astype(F32),
        gq=jnp.tile(gq.astype(F32), Q_HEADS).reshape(1, Q_W), gk=jnp.tile(gk.astype(F32), KV_HEADS).reshape(1, KV_W),
        w_oa=w_oa.astype(BF16), w_ob=w_ob.astype(BF16), w_out=w_out.astype(BF16), ffn_g=ffn_g.reshape(1, d),
        w_router=w_router, b_router=b_router, w_eg=grouped(w_eg), w_eu=grouped(w_eu), w_ed=grouped(w_ed),
        final_g=final_g.reshape(1, d))


def _trunk(x, w):
    b, s, d = x.shape
    t = b * s
    assert s % ROW_TILE == 0 and s % (2 * GLOB_TK) == 0, "sequence length must be a multiple of the row tile"
    x2 = x.reshape(t, d)
    qa, ka, va, qb, kb, vb, ga, gb = _in_proj(x2, s, w, ROW_TILE)
    ya = _win_attn(qa, ka, va, w["sink"], s, WIN_TILE)
    yb = _glob_attn(qb.reshape(b, s, Q_W), kb.reshape(b, s, KV_W), vb.reshape(b, s, KV_W),
                    GLOB_TQ, GLOB_TK).reshape(t, Q_W)
    x1, route, routet, cnt = _merge(x2, ya, yb, ga, gb, w, ROW_TILE)
    y = _moe(x1, route, routet, cnt[:, :N_GROUPS, 0], w, ROW_TILE)
    return y.reshape(b, s, d)


def kernel(x_prompt, x_sample, attn_norm_g, w_in, a_sink, b_q_norm_g, b_k_norm_g, w_oa, w_ob, w_out, ffn_norm_g,
           w_router_group, b_router_group, w_router_expert, b_router_expert, w_expert_gate, w_expert_up,
           w_expert_down, final_norm_g):
    assert attn_norm_g.shape[0] == 1, "single-layer trunk"
    w = _prepare(attn_norm_g[0], w_in[0], a_sink[0], b_q_norm_g[0], b_k_norm_g[0], w_oa[0], w_ob[0], w_out[0],
                 ffn_norm_g[0], w_router_group[0], b_router_group[0], w_router_expert[0], b_router_expert[0],
                 w_expert_gate[0], w_expert_up[0], w_expert_down[0], final_norm_g)
    return (_trunk(x_prompt, w), _trunk(x_sample, w))
```

```python
import functools

import jax
import jax.numpy as jnp
import numpy as np
from jax import lax
from jax.experimental import pallas as pl
from jax.experimental.pallas import tpu as pltpu

HEAD_DIM = 64
Q_HEADS = 8
KV_HEADS = 2
GROUP = Q_HEADS // KV_HEADS
Q_W = Q_HEADS * HEAD_DIM
KV_W = KV_HEADS * HEAD_DIM
WINDOW = 128
BLOCK = 128
GRID_W = 64
ROPE_THETA = 10000.0
N_GROUPS = 4
EXPERTS_PER_GROUP = 4
N_EXPERTS = N_GROUPS * EXPERTS_PER_GROUP
EPS = 1e-6
NEG_INF = -1e30
SCALE = HEAD_DIM ** -0.5
LOG2E = 1.4426950408889634
ONES_ROWS = 16
VT_ROWS = HEAD_DIM + ONES_ROWS
SCORE_DTYPE = jnp.bfloat16
LANES = 128

VMEM_LIMIT = 56 * 1024 * 1024

F32 = jnp.float32
BF16 = jnp.bfloat16


def _params(sem):
    return pltpu.CompilerParams(dimension_semantics=sem, vmem_limit_bytes=VMEM_LIMIT)


def _full(shape):
    return pl.BlockSpec(shape, lambda *_: (0,) * len(shape))


def _hi_lo(v):
    top = lax.bitcast_convert_type(lax.bitcast_convert_type(v, jnp.uint32) & jnp.uint32(0xFFFF0000), F32)
    return top.astype(BF16), (v - top).astype(BF16)


def _split_dot(y, ones_bd):
    hi, lo = _hi_lo(y)
    return jnp.dot(hi, ones_bd, preferred_element_type=F32) + jnp.dot(lo, ones_bd, preferred_element_type=F32)


def _pair_swap(y):
    n = y.shape[-1]
    lane = lax.broadcasted_iota(jnp.int32, y.shape, y.ndim - 1)
    nxt = pltpu.roll(y, n - 1, y.ndim - 1)
    prv = pltpu.roll(y, 1, y.ndim - 1)
    return jnp.where((lane & 1) == 0, nxt, prv)


def _norm_rope(y, gain, ones_bd, cos, sin):
    ms = _split_dot(y * y, ones_bd) * (1.0 / HEAD_DIM)
    yn = y * lax.rsqrt(ms + EPS) * gain
    return yn * cos + _pair_swap(yn) * sin


def _in_proj_kernel(x_ref, g_ref, w_ref, cos_ref, sin_ref, gq_ref, gk_ref, oq_ref, ok_ref,
                    qa_ref, ka_ref, va_ref, qb_ref, kb_ref, vb_ref, ga_ref, gb_ref):
    x = x_ref[...]
    d = x.shape[1]
    ms = jnp.mean(x * x, axis=-1, keepdims=True)
    h = (x * lax.rsqrt(ms + EPS) * g_ref[...]).astype(BF16)

    def proj(lo, width):
        return jnp.dot(h, w_ref[:, lo:lo + width], preferred_element_type=F32)

    o = 0
    qa_ref[...] = (proj(o, Q_W) * (SCALE * LOG2E)).astype(BF16); o += Q_W
    ka_ref[...] = proj(o, KV_W).astype(BF16); o += KV_W
    va_ref[...] = proj(o, KV_W).astype(BF16); o += KV_W
    cos = cos_ref[...]
    sin = sin_ref[...]
    qb = proj(o, Q_W); o += Q_W
    cos_q = jnp.concatenate([cos] * (Q_W // LANES), axis=1)
    sin_q = jnp.concatenate([sin] * (Q_W // LANES), axis=1)
    qb_ref[...] = (_norm_rope(qb, gq_ref[...], oq_ref[...], cos_q, sin_q) * (SCALE * LOG2E)).astype(BF16)
    kb = proj(o, KV_W); o += KV_W
    kb_ref[...] = _norm_rope(kb, gk_ref[...], ok_ref[...], cos, sin).astype(BF16)
    vb_ref[...] = proj(o, KV_W).astype(BF16); o += KV_W
    ga_ref[...] = jax.nn.sigmoid(proj(o, d)).astype(BF16); o += d
    gb_ref[...] = jax.nn.sigmoid(proj(o, d)).astype(BF16)


def _rope_tables(seq_len):
    rows = seq_len // GRID_W
    row = jnp.repeat(jnp.arange(rows, dtype=F32), GRID_W)
    col = jnp.tile(jnp.arange(GRID_W, dtype=F32), rows)
    half = HEAD_DIM // 2
    inv = ROPE_THETA ** (-jnp.arange(0, half, 2, dtype=F32) / half)
    ang = jnp.concatenate([row[:, None] * inv, col[:, None] * inv], axis=-1)
    cos = jnp.repeat(jnp.cos(ang), 2, axis=-1)
    sin = jnp.repeat(jnp.sin(ang), 2, axis=-1) * jnp.tile(jnp.array([-1.0, 1.0], F32), half)
    reps = LANES // HEAD_DIM
    return jnp.tile(cos, (1, reps)), jnp.tile(sin, (1, reps))


def _block_ones(width):
    idx = np.arange(width) // HEAD_DIM
    return jnp.asarray(idx[:, None] == idx[None, :], dtype=BF16)


def _in_proj(x2, seq_len, w, tm):
    t, d = x2.shape
    in_w = w["w_in"].shape[1]
    cos, sin = _rope_tables(seq_len)
    nseq = seq_len // tm
    row = lambda i: (i, 0)
    pos = lambda i: (i % nseq, 0)
    outs = [(Q_W, BF16), (KV_W, BF16), (KV_W, BF16), (Q_W, BF16), (KV_W, BF16), (KV_W, BF16), (d, BF16), (d, BF16)]
    return pl.pallas_call(
        _in_proj_kernel,
        grid=(t // tm,),
        in_specs=[pl.BlockSpec((tm, d), row), _full((1, d)), _full((d, in_w)),
                  pl.BlockSpec((tm, LANES), pos), pl.BlockSpec((tm, LANES), pos),
                  _full((1, Q_W)), _full((1, KV_W)), _full((Q_W, Q_W)), _full((KV_W, KV_W))],
        out_specs=[pl.BlockSpec((tm, w), row) for w, _ in outs],
        out_shape=[jax.ShapeDtypeStruct((t, w), dt) for w, dt in outs],
        compiler_params=_params(("parallel",)),
        name="in_proj",
    )(x2, w["attn_g"], w["w_in"], cos, sin, w["gq"], w["gk"], _block_ones(Q_W), _block_ones(KV_W))


PAIRS = Q_HEADS // 2
N_EDGE = 4


def _win_bias():
    slopes = np.exp2(-8.0 * np.arange(1, Q_HEADS + 1, dtype=np.float32) / Q_HEADS).astype(np.float32)
    key = np.arange(3 * BLOCK)[:, None]
    dist = np.abs(np.arange(BLOCK)[None, :] - (key - BLOCK))
    tabs = []
    for edge in range(N_EDGE):
        alive = dist <= WINDOW
        if edge & 1:
            alive = alive & (key >= BLOCK)
        if edge & 2:
            alive = alive & (key < 2 * BLOCK)
        heads = [np.where(alive, -slopes[h] * np.float32(LOG2E) * dist.astype(np.float32), np.float32(NEG_INF))
                 for h in range(Q_HEADS)]
        tabs.append(np.stack([np.concatenate([heads[2 * p], heads[2 * p + 1]], axis=1) for p in range(PAIRS)]))
    return jnp.asarray(np.stack(tabs), dtype=F32)


def _win_attn_kernel(sink_ref, q_ref, kc_ref, kp_ref, kn_ref, vc_ref, vp_ref, vn_ref, bias_ref, o_ref,
                     kh_scr, vt_scr, qp_scr, s_scr, p_scr, ot_scr, *, blocks_per_seq, sub):
    i = pl.program_id(0)
    tq = q_ref.shape[0]
    ones = jnp.ones((ONES_ROWS, BLOCK), BF16)
    for off, n, kr, vr in ((0, BLOCK, kp_ref, vp_ref), (BLOCK, tq, kc_ref, vc_ref), (BLOCK + tq, BLOCK, kn_ref, vn_ref)):
        k = kr[...]
        vt = vr[...].astype(F32).T.astype(BF16)
        for g in range(KV_HEADS):
            kh_scr[g, off:off + n, :] = k[:, g * HEAD_DIM:(g + 1) * HEAD_DIM]
            vt_scr[g * VT_ROWS:g * VT_ROWS + HEAD_DIM, off:off + n] = vt[g * HEAD_DIM:(g + 1) * HEAD_DIM, :]
            for o in range(off, off + n, BLOCK):
                vt_scr[g * VT_ROWS + HEAD_DIM:(g + 1) * VT_ROWS, o:o + BLOCK] = ones
    qt = q_ref[...].astype(F32).T.astype(BF16)
    for j in range(sub):
        for p in range(PAIRS):
            for u in range(2):
                h = 2 * p + u
                qp_scr[j, p, :, u * BLOCK:(u + 1) * BLOCK] = qt[h * HEAD_DIM:(h + 1) * HEAD_DIM,
                                                                j * BLOCK:(j + 1) * BLOCK]
    lane = lax.broadcasted_iota(jnp.int32, (1, 2 * BLOCK), 1)
    for j in range(sub):
        x = j % 2
        blk = (i * sub + j) % blocks_per_seq
        edge = (blk == 0).astype(jnp.int32) + 2 * (blk == blocks_per_seq - 1).astype(jnp.int32)
        for p in range(PAIRS):
            g = (2 * p) // GROUP
            s = jnp.dot(kh_scr[g, j * BLOCK:(j + 3) * BLOCK, :], qp_scr[j, p], preferred_element_type=F32)
            s_scr[x, p] = s + bias_ref[edge, p]
        sink_terms = []
        for p in range(PAIRS):
            logits = s_scr[x, p]
            sink = jnp.where(lane < BLOCK, sink_ref[2 * p], sink_ref[2 * p + 1]) * LOG2E
            m = jnp.maximum(jnp.max(logits, axis=0, keepdims=True), sink)
            p_scr[x, p] = jnp.exp2(logits - m).astype(BF16)
            sink_terms.append(jnp.exp2(sink - m))
        for p in range(PAIRS):
            g = (2 * p) // GROUP
            ot = jnp.dot(vt_scr[g * VT_ROWS:(g + 1) * VT_ROWS, j * BLOCK:(j + 3) * BLOCK], p_scr[x, p],
                         preferred_element_type=F32)
            out = ot[0:HEAD_DIM] / (ot[HEAD_DIM:HEAD_DIM + 1] + sink_terms[p])
            for u in range(2):
                h = 2 * p + u
                ot_scr[h * HEAD_DIM:(h + 1) * HEAD_DIM, j * BLOCK:(j + 1) * BLOCK] = out[:, u * BLOCK:(u + 1) * BLOCK]
    o_ref[...] = ot_scr[...].T.astype(o_ref.dtype)


def _win_attn(qa, ka, va, sink, seq_len, tq):
    t = qa.shape[0]
    sub = tq // BLOCK
    nblk = t // BLOCK
    win = tq + 2 * BLOCK
    row = lambda i: (i, 0)
    prev = lambda i: (jnp.maximum(i * sub - 1, 0), 0)
    nxt = lambda i: (jnp.minimum((i + 1) * sub, nblk - 1), 0)
    kv_specs = [pl.BlockSpec((tq, KV_W), row), pl.BlockSpec((BLOCK, KV_W), prev), pl.BlockSpec((BLOCK, KV_W), nxt)]
    bias_spec = pl.BlockSpec((N_EDGE, PAIRS, 3 * BLOCK, 2 * BLOCK), lambda i: (0, 0, 0, 0),
                             pipeline_mode=pl.Buffered(1))
    return pl.pallas_call(
        functools.partial(_win_attn_kernel, blocks_per_seq=seq_len // BLOCK, sub=sub),
        grid=(t // tq,),
        in_specs=[pl.BlockSpec(memory_space=pltpu.SMEM), pl.BlockSpec((tq, Q_W), row)] + kv_specs + kv_specs
                 + [bias_spec],
        out_specs=pl.BlockSpec((tq, Q_W), row),
        out_shape=jax.ShapeDtypeStruct((t, Q_W), BF16),
        scratch_shapes=[pltpu.VMEM((KV_HEADS, win, HEAD_DIM), BF16), pltpu.VMEM((KV_HEADS * VT_ROWS, win), BF16),
                        pltpu.VMEM((sub, PAIRS, HEAD_DIM, 2 * BLOCK), BF16),
                        pltpu.VMEM((2, PAIRS, 3 * BLOCK, 2 * BLOCK), F32),
                        pltpu.VMEM((2, PAIRS, 3 * BLOCK, 2 * BLOCK), BF16), pltpu.VMEM((Q_W, tq), F32)],
        compiler_params=_params(("parallel",)),
        name="win_attn",
    )(sink, qa, ka, ka, ka, va, va, va, _win_bias())


def _glob_attn_kernel(q_ref, k_ref, v_ref, o_ref, kh_ref, vt_ref, qt_scr, s_scr, p_scr, cm_scr, m_scr, a_scr, acc_scr,
                      ot_scr, *, tk):
    tq = q_ref.shape[1]
    nk = k_ref.shape[1] // tk

    @pl.when(pl.program_id(1) == 0)
    def _():
        ones = jnp.ones((ONES_ROWS, tk), BF16)

        def fill(c, carry):
            off = pl.multiple_of(c * tk, tk)
            vt = v_ref[0, pl.ds(off, tk), :].astype(F32).T.astype(BF16)
            kc = k_ref[0, pl.ds(off, tk), :]
            for g in range(KV_HEADS):
                kh_ref[g, pl.ds(off, tk), :] = kc[:, g * HEAD_DIM:(g + 1) * HEAD_DIM]
                vt_ref[g * VT_ROWS:g * VT_ROWS + HEAD_DIM, pl.ds(off, tk)] = vt[g * HEAD_DIM:(g + 1) * HEAD_DIM, :]
                vt_ref[g * VT_ROWS + HEAD_DIM:(g + 1) * VT_ROWS, pl.ds(off, tk)] = ones
            return carry

        lax.fori_loop(0, nk, fill, 0)

    qt_scr[...] = q_ref[0].astype(F32).T.astype(BF16)

    for g in range(KV_HEADS):
        heads = range(g * GROUP, (g + 1) * GROUP)

        def stage_a(c, x, heads=heads, g=g):
            kc = kh_ref[g, pl.ds(pl.multiple_of(c * tk, tk), tk), :]
            for i, h in enumerate(heads):
                s = jnp.dot(kc, qt_scr[h * HEAD_DIM:(h + 1) * HEAD_DIM, :], preferred_element_type=F32)
                s_scr[x, i] = s.astype(s_scr.dtype)
                cm_scr[x, i] = jnp.max(s, axis=0, keepdims=True)

        def stage_b(x):
            for i in range(GROUP):
                m = m_scr[i]
                m_new = jnp.maximum(m, cm_scr[x, i])
                a_scr[x, i] = jnp.exp2(m - m_new)
                m_scr[i] = m_new
                p_scr[x, i] = jnp.exp2(s_scr[x, i] - m_new.astype(s_scr.dtype)).astype(BF16)

        def stage_c(c, x, g=g):
            vt = vt_ref[g * VT_ROWS:(g + 1) * VT_ROWS, pl.ds(pl.multiple_of(c * tk, tk), tk)]
            for i in range(GROUP):
                acc_scr[i] = a_scr[x, i] * acc_scr[i] + jnp.dot(vt, p_scr[x, i], preferred_element_type=F32)

        def step(c, x, last=False):
            stage_c(c - 1, 1 - x)
            stage_b(x)
            if not last:
                stage_a(c + 1, 1 - x)

        for i in range(GROUP):
            m_scr[i] = jnp.full((1, tq), -jnp.inf, F32)
            acc_scr[i] = jnp.zeros((VT_ROWS, tq), F32)
        stage_a(0, 0)
        stage_b(0)
        stage_a(1, 1)

        def pair(j, carry):
            step(2 * j + 1, 1)
            step(2 * j + 2, 0)
            return carry

        lax.fori_loop(0, (nk - 2) // 2, pair, 0, unroll=8)
        step(nk - 1, 1, last=True)
        stage_c(nk - 1, 1)
        for i, h in enumerate(heads):
            acc = acc_scr[i]
            ot_scr[h * HEAD_DIM:(h + 1) * HEAD_DIM, :] = acc[0:HEAD_DIM] / acc[HEAD_DIM:HEAD_DIM + 1]
    o_ref[0] = ot_scr[...].T.astype(o_ref.dtype)


def _glob_attn(qb, kb, vb, tq, tk):
    b, s, _ = qb.shape
    assert s % (2 * tk) == 0, "the chunk pipeline walks key chunks in pairs"
    return pl.pallas_call(
        functools.partial(_glob_attn_kernel, tk=tk),
        grid=(b, s // tq),
        in_specs=[pl.BlockSpec((1, tq, Q_W), lambda bi, qi: (bi, qi, 0)),
                  pl.BlockSpec((1, s, KV_W), lambda bi, qi: (bi, 0, 0)),
                  pl.BlockSpec((1, s, KV_W), lambda bi, qi: (bi, 0, 0))],
        out_specs=pl.BlockSpec((1, tq, Q_W), lambda bi, qi: (bi, qi, 0)),
        out_shape=jax.ShapeDtypeStruct((b, s, Q_W), BF16),
        scratch_shapes=[pltpu.VMEM((KV_HEADS, s, HEAD_DIM), BF16), pltpu.VMEM((KV_HEADS * VT_ROWS, s), BF16),
                        pltpu.VMEM((Q_W, tq), BF16),
                        pltpu.VMEM((2, GROUP, tk, tq), SCORE_DTYPE), pltpu.VMEM((2, GROUP, tk, tq), BF16),
                        pltpu.VMEM((2, GROUP, 1, tq), F32), pltpu.VMEM((GROUP, 1, tq), F32),
                        pltpu.VMEM((2, GROUP, 1, tq), F32), pltpu.VMEM((GROUP, VT_ROWS, tq), F32),
                        pltpu.VMEM((Q_W, tq), F32)],
        compiler_params=_params(("arbitrary", "arbitrary")),
        name="glob_attn",
    )(qb, kb, vb)


def _first_argmax(vals, idx, big):
    m = jnp.max(vals, axis=0, keepdims=True)
    return m, jnp.min(jnp.where(vals == m, idx, big), axis=0, keepdims=True)


def _route(lt):
    n = lt.shape[1]
    gl = lt[0:N_GROUPS]
    idx = lax.broadcasted_iota(jnp.int32, (N_GROUPS, n), 0)
    gmax, g_sel = _first_argmax(gl, idx, N_GROUPS)
    g_w = 1.0 / jnp.sum(jnp.exp(gl - gmax), axis=0, keepdims=True)
    e_sel = jnp.zeros((EXPERTS_PER_GROUP, n), F32)
    for g in range(N_GROUPS):
        lo = N_GROUPS + g * EXPERTS_PER_GROUP
        e_sel = jnp.where(g_sel == g, lt[lo:lo + EXPERTS_PER_GROUP], e_sel)
    ex = jnp.exp(e_sel - jnp.max(e_sel, axis=0, keepdims=True))
    e_prob = ex / jnp.sum(ex, axis=0, keepdims=True)
    p1, i1 = _first_argmax(e_prob, idx, EXPERTS_PER_GROUP)
    rest = jnp.where(idx == i1, -1.0, e_prob)
    p2, i2 = _first_argmax(rest, idx, EXPERTS_PER_GROUP)
    tot = p1 + p2
    row = lax.broadcasted_iota(jnp.int32, (2 * N_GROUPS, n), 0)
    out = jnp.where(row == g_sel, 1.0, 0.0)
    out = out + jnp.where(row == N_GROUPS + i1, g_w * (p1 / tot), 0.0)
    return out + jnp.where(row == N_GROUPS + i2, g_w * (p2 / tot), 0.0)


def _merge_kernel(x_ref, ya_ref, yb_ref, ga_ref, gb_ref, woa_ref, wob_ref, wout_ref, gf_ref, wr_ref, br_ref,
                  x1_ref, route_ref, routet_ref, cnt_ref):
    tm = x_ref.shape[0]
    a = jnp.dot(ya_ref[...], woa_ref[...], preferred_element_type=F32)
    b = jnp.dot(yb_ref[...], wob_ref[...], preferred_element_type=F32)
    merged = ga_ref[...].astype(F32) * a + gb_ref[...].astype(F32) * b
    x1 = x_ref[...] + jnp.dot(merged.astype(BF16), wout_ref[...], preferred_element_type=F32)
    x1_ref[...] = x1
    t = _rmsnorm(x1, gf_ref[...])
    t_hi, t_lo = _hi_lo(t)
    big = jnp.dot(t_hi, wr_ref[...], preferred_element_type=F32)
    small = jnp.dot(t_lo, wr_ref[:, 0:LANES], preferred_element_type=F32)
    logits = big[:, 0:LANES] + big[:, LANES:2 * LANES] + small + br_ref[...]
    block = _route(logits.T)
    route_t = jnp.concatenate([block, jnp.zeros((LANES - block.shape[0], tm), F32)], axis=0)
    routet_ref[...] = route_t
    route_ref[...] = route_t.T
    cnt = jnp.sum(block, axis=1, keepdims=True)
    cnt_ref[0] = jnp.broadcast_to(cnt, cnt_ref.shape[1:]).astype(jnp.int32)


def _router_weights(w_rg, b_rg, w_re, b_re):
    d = w_rg.shape[0]
    wr = jnp.zeros((d, LANES), F32).at[:, :N_GROUPS].set(w_rg).at[:, N_GROUPS:N_GROUPS + N_EXPERTS].set(w_re)
    br = jnp.zeros((1, LANES), F32).at[0, :N_GROUPS].set(b_rg).at[0, N_GROUPS:N_GROUPS + N_EXPERTS].set(b_re)
    return jnp.concatenate(_hi_lo(wr), axis=1), br


def _merge(x2, ya, yb, ga, gb, w, tm):
    t, d = x2.shape
    row = lambda i: (i, 0)
    return pl.pallas_call(
        _merge_kernel,
        grid=(t // tm,),
        in_specs=[pl.BlockSpec((tm, d), row), pl.BlockSpec((tm, Q_W), row), pl.BlockSpec((tm, Q_W), row),
                  pl.BlockSpec((tm, d), row), pl.BlockSpec((tm, d), row),
                  _full((Q_W, d)), _full((Q_W, d)), _full((d, d)), _full((1, d)), _full((d, 2 * LANES)),
                  _full((1, LANES))],
        out_specs=[pl.BlockSpec((tm, d), row), pl.BlockSpec((tm, LANES), row),
                   pl.BlockSpec((LANES, tm), lambda i: (0, i)), pl.BlockSpec((1, 8, LANES), lambda i: (i, 0, 0))],
        out_shape=[jax.ShapeDtypeStruct((t, d), F32), jax.ShapeDtypeStruct((t, LANES), F32),
                   jax.ShapeDtypeStruct((LANES, t), F32), jax.ShapeDtypeStruct((t // tm, 8, LANES), jnp.int32)],
        compiler_params=_params(("parallel",)),
        name="merge",
    )(x2, ya, yb, ga, gb, w["w_oa"], w["w_ob"], w["w_out"], w["ffn_g"], w["w_router"], w["b_router"])


ROW_ALIGN = 32
BIG_UNITS = 8
BIG_BLOCK = BIG_UNITS * ROW_ALIGN
WHOLE_UNITS = (7, 8, 9, 10)
P_BUILD_ROWS = 128


def _rmsnorm(x, g):
    ms = jnp.mean(x * x, axis=-1, keepdims=True)
    return x * lax.rsqrt(ms + EPS) * g


def _moe_kernel(cnt_ref, x1_ref, route_ref, routet_ref, gf_ref, wg_ref, wu_ref, wd_ref, gfin_ref, o_ref,
                p_scr, q_scr, ts_scr, ws_scr):
    i = pl.program_id(0)
    g = pl.program_id(1)
    tm = x1_ref.shape[0]
    rows = p_scr.shape[0]
    offs, off = [], 0
    for gg in range(N_GROUPS):
        offs.append(off)
        off = off + (cnt_ref[i, gg] + (ROW_ALIGN - 1)) // ROW_ALIGN * ROW_ALIGN

    @pl.when(g == 0)
    def _():
        route = route_ref[...]
        sub = lax.broadcasted_iota(jnp.int32, (16, tm), 0)
        pos = lax.broadcasted_iota(jnp.int32, (16, tm), 1)
        onehot_t = jnp.where(sub < N_GROUPS, routet_ref[0:16, :], 0.0)
        count = onehot_t
        shift = 1
        while shift < tm:
            count = count + jnp.where(pos >= shift, pltpu.roll(count, shift, 1), 0.0)
            shift *= 2
        goff_t = jnp.zeros((16, tm), F32)
        for gg in range(N_GROUPS):
            goff_t = jnp.where(sub == gg, jnp.asarray(offs[gg], jnp.int32).astype(F32), goff_t)
        dest = jnp.sum((count - onehot_t + goff_t) * onehot_t, axis=0, keepdims=True)
        dest_row = dest.astype(jnp.int32)
        dest_col = jnp.broadcast_to(dest, (LANES, tm)).T[:, 0:1].astype(jnp.int32)
        q_scr[...] = jnp.where(dest_col == lax.broadcasted_iota(jnp.int32, (tm, rows), 1), 1.0, 0.0).astype(BF16)
        for r0 in range(0, rows, P_BUILD_ROWS):
            rid = lax.broadcasted_iota(jnp.int32, (P_BUILD_ROWS, tm), 0) + r0
            p_scr[r0:r0 + P_BUILD_ROWS, :] = jnp.where(dest_row == rid, 1.0, 0.0).astype(BF16)
        t = _rmsnorm(x1_ref[...], gf_ref[...]).astype(BF16)
        r_hi_lo = jnp.concatenate(_hi_lo(route), axis=1)
        gather = rows // 2
        for r0 in range(0, rows, gather):
            pb = p_scr[r0:r0 + gather, :]
            ts_scr[r0:r0 + gather, :] = jnp.dot(pb, t, preferred_element_type=F32).astype(BF16)
            w2 = jnp.dot(pb, r_hi_lo, preferred_element_type=F32)
            ws_scr[r0:r0 + gather, :] = w2[:, 0:LANES] + w2[:, LANES:2 * LANES]

    def ffn(r, m):
        tb = ts_scr[pl.ds(pl.multiple_of(r, ROW_ALIGN), m), :]
        wb = ws_scr[pl.ds(pl.multiple_of(r, ROW_ALIGN), m), :]
        y = jnp.zeros((m, o_ref.shape[1]), F32)
        for e in range(EXPERTS_PER_GROUP):
            a = jax.nn.silu(jnp.dot(tb, wg_ref[0, e], preferred_element_type=F32)) * jnp.dot(
                tb, wu_ref[0, e], preferred_element_type=F32)
            y = y + wb[:, N_GROUPS + e:N_GROUPS + e + 1] * jnp.dot(a.astype(BF16), wd_ref[0, e],
                                                                   preferred_element_type=F32)
        ts_scr[pl.ds(pl.multiple_of(r, ROW_ALIGN), m), :] = y.astype(BF16)

    start = offs[0]
    for gg in range(1, N_GROUPS):
        start = jnp.where(g == gg, offs[gg], start)
    n_small = (cnt_ref[i, g] + (ROW_ALIGN - 1)) // ROW_ALIGN

    for units in WHOLE_UNITS:
        @pl.when(n_small == units)
        def _(units=units):
            ffn(start, units * ROW_ALIGN)

    @pl.when((n_small < WHOLE_UNITS[0]) | (n_small > WHOLE_UNITS[-1]))
    def _():
        n_big = n_small // BIG_UNITS

        def big(j, carry):
            ffn(start + j * BIG_BLOCK, BIG_BLOCK)
            return carry

        lax.fori_loop(0, n_big, big, 0)

        def unit(j, carry):
            ffn(start + j * ROW_ALIGN, ROW_ALIGN)
            return carry

        lax.fori_loop(n_big * BIG_UNITS, n_small, unit, 0)

    @pl.when(g == N_GROUPS - 1)
    def _():
        x2 = x1_ref[...] + jnp.dot(q_scr[...], ts_scr[...], preferred_element_type=F32)
        o_ref[...] = _rmsnorm(x2, gfin_ref[...])


def _moe(x1, route, routet, cnt, w, tm):
    t, d = x1.shape
    de = w["w_eg"].shape[-1]
    rows = tm + N_GROUPS * ROW_ALIGN
    row = lambda i, g, c: (i, 0)
    const = lambda i, g, c: (0, 0)
    wspec = lambda a, b: pl.BlockSpec((1, EXPERTS_PER_GROUP, a, b), lambda i, g, c: (g, 0, 0, 0))
    return pl.pallas_call(
        _moe_kernel,
        grid_spec=pltpu.PrefetchScalarGridSpec(
            num_scalar_prefetch=1,
            grid=(t // tm, N_GROUPS),
            in_specs=[pl.BlockSpec((tm, d), row), pl.BlockSpec((tm, LANES), row),
                      pl.BlockSpec((LANES, tm), lambda i, g, c: (0, i)), pl.BlockSpec((1, d), const),
                      wspec(d, de), wspec(d, de), wspec(de, d), pl.BlockSpec((1, d), const)],
            out_specs=pl.BlockSpec((tm, d), row),
            scratch_shapes=[pltpu.VMEM((rows, tm), BF16), pltpu.VMEM((tm, rows), BF16), pltpu.VMEM((rows, d), BF16),
                            pltpu.VMEM((rows, LANES), F32)]),
        out_shape=jax.ShapeDtypeStruct((t, d), F32),
        compiler_params=_params(("arbitrary", "arbitrary")),
        name="moe",
    )(cnt, x1, route, routet, w["ffn_g"], w["w_eg"], w["w_eu"], w["w_ed"], w["final_g"])


ROW_TILE = 1024
WIN_TILE = 2048
GLOB_TQ = 256
GLOB_TK = 256


def _prepare(attn_g, w_in, sink, gq, gk, w_oa, w_ob, w_out, ffn_g, w_rg, b_rg, w_re, b_re, w_eg, w_eu, w_ed,
             final_g):
    d = w_in.shape[0]
    w_router, b_router = _router_weights(w_rg, b_rg, w_re, b_re)
    grouped = lambda a: a.astype(BF16).reshape((N_GROUPS, EXPERTS_PER_GROUP) + a.shape[1:])
    return dict(
        attn_g=attn_g.reshape(1, d), w_in=w_in.astype(BF16), sink=sink.astype(F32),
        gq=jnp.tile(gq.astype(F32), Q_HEADS).reshape(1, Q_W), gk=jnp.tile(gk.astype(F32), KV_HEADS).reshape(1, KV_W),
        w_oa=w_oa.astype(BF16), w_ob=w_ob.astype(BF16), w_out=w_out.astype(BF16), ffn_g=ffn_g.reshape(1, d),
        w_router=w_router, b_router=b_router, w_eg=grouped(w_eg), w_eu=grouped(w_eu), w_ed=grouped(w_ed),
        final_g=final_g.reshape(1, d))


def _trunk(x, w):
    b, s, d = x.shape
    t = b * s
    assert s % ROW_TILE == 0 and s % (2 * GLOB_TK) == 0, "sequence length must be a multiple of the row tile"
    x2 = x.reshape(t, d)
    qa, ka, va, qb, kb, vb, ga, gb = _in_proj(x2, s, w, ROW_TILE)
    ya = _win_attn(qa, ka, va, w["sink"], s, WIN_TILE)
    yb = _glob_attn(qb.reshape(b, s, Q_W), kb.reshape(b, s, KV_W), vb.reshape(b, s, KV_W),
                    GLOB_TQ, GLOB_TK).reshape(t, Q_W)
    x1, route, routet, cnt = _merge(x2, ya, yb, ga, gb, w, ROW_TILE)
    y = _moe(x1, route, routet, cnt[:, :N_GROUPS, 0], w, ROW_TILE)
    return y.reshape(b, s, d)


def kernel(x_prompt, x_sample, attn_norm_g, w_in, a_sink, b_q_norm_g, b_k_norm_g, w_oa, w_ob, w_out, ffn_norm_g,
           w_router_group, b_router_group, w_router_expert, b_router_expert, w_expert_gate, w_expert_up,
           w_expert_down, final_norm_g):
    assert attn_norm_g.shape[0] == 1, "single-layer trunk"
    w = _prepare(attn_norm_g[0], w_in[0], a_sink[0], b_q_norm_g[0], b_k_norm_g[0], w_oa[0], w_ob[0], w_out[0],
                 ffn_norm_g[0], w_router_group[0], b_router_group[0], w_router_expert[0], b_router_expert[0],
                 w_expert_gate[0], w_expert_up[0], w_expert_down[0], final_norm_g)
    return (_trunk(x_prompt, w), _trunk(x_sample, w))
```
